```python
import math
import jax
import jax.numpy as jnp
from jax import lax
import numpy as np

D_MODEL = 1024
BATCH = 16
SEQ = 4096
DEPTH = 2

EPS = 1e-6
N_BRANCH = 3
GDN_HEADS = 4
GDN_DK = 128
GDN_DV = 128
GDN_CONV = 4
GDN_CHUNK = 64
GDN_KEY_W = GDN_HEADS * GDN_DK
GDN_VAL_W = GDN_HEADS * GDN_DV
HGRN_HEADS = 4
HGRN_DK = 128
HGRN_DV = 128
HGRN_CHUNK = 16
HGRN_KEY_W = HGRN_HEADS * HGRN_DK
HGRN_VAL_W = HGRN_HEADS * HGRN_DV
SSD_HEADS = 8
SSD_HEAD_DIM = 64
SSD_GROUPS = 2
SSD_HEADS_PER_GROUP = SSD_HEADS // SSD_GROUPS
SSD_STATE = 128
SSD_CONV = 4
SSD_CHUNK = 64
SSD_INNER = SSD_HEADS * SSD_HEAD_DIM
SSD_XBC_W = SSD_INNER + 2 * SSD_GROUPS * SSD_STATE
FFN_HIDDEN = 2816
FFN_CONV = 3

SPLIT_SIZES = (
    2 * GDN_KEY_W + GDN_VAL_W,
    GDN_HEADS,
    GDN_HEADS,
    GDN_VAL_W,
    HGRN_KEY_W,
    HGRN_KEY_W,
    HGRN_VAL_W,
    HGRN_VAL_W,
    SSD_INNER,
    SSD_XBC_W,
    SSD_HEADS,
    N_BRANCH * D_MODEL,
)
N_IN = sum(SPLIT_SIZES)

kernel_name = "hybrid_gdn_hgrn2_ssd_adaln_block"


def _f32(t):
    return t.astype(jnp.float32)


def rms_norm(x, w):
    xf = x.astype(jnp.float32)
    y = xf * lax.rsqrt(jnp.mean(xf * xf, axis=-1, keepdims=True) + EPS)
    return (y * w.astype(jnp.float32)).astype(x.dtype)


def l2_normalize(x):
    return x * lax.rsqrt(jnp.sum(x * x, axis=-1, keepdims=True) + EPS)


def modulate(h, shift, scale):
    return h * (1.0 + scale[:, None, :]) + shift[:, None, :]


def causal_dwconv(x, w, b=None):
    k_width = w.shape[0]
    s = x.shape[1]
    xp = jnp.pad(x, ((0, 0), (k_width - 1, 0), (0, 0)))
    y = w[k_width - 1] * x
    for k in range(k_width - 1):
        y = y + w[k] * xp[:, k:k + s]
    if b is not None:
        y = y + b
    return y


def to_chunks(t, c):
    b, s = t.shape[:2]
    t = t.reshape((b, s // c, c) + t.shape[2:])
    t = jnp.swapaxes(t, 0, 1)
    return jnp.swapaxes(t, 2, 3)


def from_chunks(t):
    t = jnp.swapaxes(jnp.swapaxes(t, 2, 3), 0, 1)
    return t.reshape((t.shape[0], t.shape[1] * t.shape[2]) + t.shape[3:])


def gated_delta_chunked(q, k, v, g, beta):
    b, s, h, dk = q.shape
    dv = v.shape[-1]
    c = GDN_CHUNK
    qc = to_chunks(q * (dk ** -0.5), c)
    kc = to_chunks(k, c)
    vc = to_chunks(v, c)
    bc = to_chunks(beta, c)
    big_g = jnp.cumsum(to_chunks(g, c), axis=-1)
    incl = jnp.tril(jnp.ones((c, c), bool))
    strict = jnp.tril(jnp.ones((c, c), bool), -1)
    diff = big_g[..., :, None] - big_g[..., None, :]
    decay = jnp.where(incl, jnp.exp(jnp.where(incl, diff, 0.0)), 0.0)
    kb = kc * bc[..., None]
    m = jnp.where(strict, jnp.einsum('nbhlk,nbhsk->nbhls', kb, kc) * decay, 0.0)
    a = m + jnp.eye(c, dtype=m.dtype)
    rhs = jnp.concatenate([vc * bc[..., None], kb * jnp.exp(big_g)[..., None]], axis=-1)
    sol = lax.linalg.triangular_solve(a, rhs, left_side=True, lower=True, unit_diagonal=True)
    u, w = sol[..., :dv], sol[..., dv:]
    attn = jnp.einsum('nbhlk,nbhsk->nbhls', qc, kc) * decay
    qg = qc * jnp.exp(big_g)[..., None]
    k_end = kc * jnp.exp(big_g[..., -1:] - big_g)[..., None]
    g_end = jnp.exp(big_g[..., -1])

    def step(state, inp):
        qg_i, k_end_i, u_i, w_i, attn_i, g_end_i = inp
        v_new = u_i - jnp.einsum('bhlk,bhkv->bhlv', w_i, state)
        o = (jnp.einsum('bhlk,bhkv->bhlv', qg_i, state)
             + jnp.einsum('bhls,bhsv->bhlv', attn_i, v_new))
        state = state * g_end_i[..., None, None] + jnp.einsum('bhsk,bhsv->bhkv', k_end_i, v_new)
        return state, o

    state0 = jnp.zeros((b, h, dk, dv), q.dtype)
    _, o = lax.scan(step, state0, (qg, k_end, u, w, attn, g_end))
    return from_chunks(o)


def hgrn2_chunked(q, k, v, logf):
    b, s, h, dk = q.shape
    dv = v.shape[-1]
    c = HGRN_CHUNK
    qc, kc, vc = to_chunks(q, c), to_chunks(k, c), to_chunks(v, c)
    big_g = jnp.cumsum(to_chunks(logf, c), axis=-2)
    g_ref = big_g[..., c // 2 - 1:c // 2, :]
    incl = jnp.tril(jnp.ones((c, c), bool))
    scores = jnp.einsum('nbhlk,nbhsk->nbhls', qc * jnp.exp(big_g - g_ref), kc * jnp.exp(g_ref - big_g))
    attn = jnp.where(incl, scores, 0.0)
    o_intra = jnp.einsum('nbhls,nbhsv->nbhlv', attn, vc)
    qg = qc * jnp.exp(big_g)
    k_end = kc * jnp.exp(big_g[..., -1:, :] - big_g)
    g_end = jnp.exp(big_g[..., -1, :])

    def step(state, inp):
        qg_i, k_end_i, v_i, g_end_i = inp
        o = jnp.einsum('bhlk,bhkv->bhlv', qg_i, state)
        state = state * g_end_i[..., None] + jnp.einsum('bhsk,bhsv->bhkv', k_end_i, v_i)
        return state, o

    state0 = jnp.zeros((b, h, dk, dv), q.dtype)
    _, o_inter = lax.scan(step, state0, (qg, k_end, vc, g_end))
    return from_chunks(o_intra + o_inter)


def ssd_chunked(xs, da, bm, cm):
    b, s, g, hg, p = xs.shape
    n_state = bm.shape[-1]
    c = SSD_CHUNK
    n = s // c
    xc = jnp.swapaxes(xs.reshape(b, n, c, g, hg, p), 0, 1)
    bc = jnp.swapaxes(bm.reshape(b, n, c, g, n_state), 0, 1)
    cc = jnp.swapaxes(cm.reshape(b, n, c, g, n_state), 0, 1)
    acs = jnp.cumsum(jnp.swapaxes(da.reshape(b, n, c, g, hg), 0, 1), axis=2)
    incl = jnp.tril(jnp.ones((c, c), bool))[:, :, None, None]
    diff = acs[:, :, :, None] - acs[:, :, None, :]
    seg = jnp.where(incl, jnp.exp(jnp.where(incl, diff, 0.0)), 0.0)
    cb = jnp.einsum('nblgd,nbsgd->nblsg', cc, bc)
    y_diag = jnp.einsum('nblsg,nblsgh,nbsghp->nblghp', cb, seg, xc)

    def step(state, inp):
        x_i, b_i, c_i, acs_i = inp
        y_off = jnp.einsum('blgd,bghpd,blgh->blghp', c_i, state, jnp.exp(acs_i))
        last = acs_i[:, -1]
        state = (state * jnp.exp(last)[..., None, None]
                 + jnp.einsum('bsgd,bsgh,bsghp->bghpd', b_i, jnp.exp(last[:, None] - acs_i), x_i))
        return state, y_off

    state0 = jnp.zeros((b, g, hg, p, n_state), xs.dtype)
    _, y_off = lax.scan(step, state0, (xc, bc, cc, acs))
    return jnp.swapaxes(y_diag + y_off, 0, 1).reshape(b, s, g, hg, p)


def gdn_branch(qkv_raw, a_raw, b_raw, z_raw, conv_w, a_log, dt_bias, norm_w):
    bsz, s, _ = qkv_raw.shape
    qkv = jax.nn.silu(causal_dwconv(qkv_raw, conv_w))
    q, k, v = jnp.split(qkv, [GDN_KEY_W, 2 * GDN_KEY_W], axis=-1)
    q = l2_normalize(q.reshape(bsz, s, GDN_HEADS, GDN_DK))
    k = l2_normalize(k.reshape(bsz, s, GDN_HEADS, GDN_DK))
    v = v.reshape(bsz, s, GDN_HEADS, GDN_DV)
    beta = jax.nn.sigmoid(b_raw)
    g = -jnp.exp(a_log) * jax.nn.softplus(a_raw + dt_bias)
    o = gated_delta_chunked(q, k, v, g, beta)
    o = rms_norm(o, norm_w) * jax.nn.silu(z_raw.reshape(bsz, s, GDN_HEADS, GDN_DV))
    return o.reshape(bsz, s, GDN_VAL_W)


def hgrn2_branch(q_raw, f_raw, i_raw, g_raw, lb, norm_w):
    bsz, s, _ = q_raw.shape
    shp = (bsz, s, HGRN_HEADS, HGRN_DK)
    q = jax.nn.silu(q_raw).reshape(shp)
    logf = jnp.log(lb + (1.0 - lb) * jax.nn.sigmoid(f_raw)).reshape(shp)
    k = ((1.0 - lb) * jax.nn.sigmoid(-f_raw)).reshape(shp)
    v = i_raw.reshape(bsz, s, HGRN_HEADS, HGRN_DV)
    o = hgrn2_chunked(q, k, v, logf)
    o = rms_norm(o, norm_w) * jax.nn.silu(g_raw.reshape(bsz, s, HGRN_HEADS, HGRN_DV))
    return o.reshape(bsz, s, HGRN_VAL_W)


def ssd_branch(z_raw, xbc_raw, dt_raw, conv_w, conv_b, a_log, dt_bias, d_skip, norm_w):
    bsz, s, _ = xbc_raw.shape
    xbc = jax.nn.silu(causal_dwconv(xbc_raw, conv_w, conv_b))
    xs, bm, cm = jnp.split(xbc, [SSD_INNER, SSD_INNER + SSD_GROUPS * SSD_STATE], axis=-1)
    xs = xs.reshape(bsz, s, SSD_GROUPS, SSD_HEADS_PER_GROUP, SSD_HEAD_DIM)
    bm = bm.reshape(bsz, s, SSD_GROUPS, SSD_STATE)
    cm = cm.reshape(bsz, s, SSD_GROUPS, SSD_STATE)
    dt = jax.nn.softplus(dt_raw + dt_bias).reshape(bsz, s, SSD_GROUPS, SSD_HEADS_PER_GROUP)
    a = -jnp.exp(a_log).reshape(SSD_GROUPS, SSD_HEADS_PER_GROUP)
    y = ssd_chunked(xs * dt[..., None], dt * a, bm, cm)
    y = y + d_skip.reshape(SSD_GROUPS, SSD_HEADS_PER_GROUP)[..., None] * xs
    group_w = SSD_HEADS_PER_GROUP * SSD_HEAD_DIM
    y = y.reshape(bsz, s, SSD_GROUPS, group_w)
    z = z_raw.reshape(bsz, s, SSD_GROUPS, group_w)
    y = rms_norm(y * jax.nn.silu(z), norm_w.reshape(SSD_GROUPS, group_w))
    return y.reshape(bsz, s, SSD_INNER)


def token_mixing(h, lb, w_in, gdn_conv_w, gdn_a_log, gdn_dt_bias, gdn_norm_w, hgrn_norm_w,
                 ssd_conv_w, ssd_conv_b, ssd_a_log, ssd_dt_bias, ssd_d, ssd_norm_w,
                 w_br_a, w_br_b, w_br_c, w_out):
    bsz, s, _ = h.shape
    dtype = h.dtype
    split_at = [int(i) for i in np.cumsum(SPLIT_SIZES)[:-1]]
    parts = jnp.split(h @ w_in, split_at, axis=-1)
    (gdn_qkv, gdn_a, gdn_b, gdn_z, hg_q, hg_f, hg_i, hg_g,
     ssd_z, ssd_xbc, ssd_dt, gate_raw) = parts
    o_a = gdn_branch(_f32(gdn_qkv), _f32(gdn_a), _f32(gdn_b), _f32(gdn_z),
                     _f32(gdn_conv_w), _f32(gdn_a_log), _f32(gdn_dt_bias), _f32(gdn_norm_w))
    o_b = hgrn2_branch(_f32(hg_q), _f32(hg_f), _f32(hg_i), _f32(hg_g), lb, _f32(hgrn_norm_w))
    o_c = ssd_branch(_f32(ssd_z), _f32(ssd_xbc), _f32(ssd_dt), _f32(ssd_conv_w), _f32(ssd_conv_b),
                     _f32(ssd_a_log), _f32(ssd_dt_bias), _f32(ssd_d), _f32(ssd_norm_w))
    gates = jax.nn.sigmoid(gate_raw).reshape(bsz, s, N_BRANCH, D_MODEL)
    merged = (gates[:, :, 0] * (o_a.astype(dtype) @ w_br_a)
              + gates[:, :, 1] * (o_b.astype(dtype) @ w_br_b)
              + gates[:, :, 2] * (o_c.astype(dtype) @ w_br_c))
    return merged @ w_out


def conv_glu_ffn(h, w_up, conv_w, conv_b, w_down):
    u = causal_dwconv(h @ w_up, conv_w, conv_b)
    gate, val = jnp.split(u, 2, axis=-1)
    return (jax.nn.silu(gate) * val) @ w_down


def _log_uniform_dt_bias(key, shape):
    lo, hi = math.log(1e-3), math.log(1e-1)
    dt = jnp.exp(jax.random.uniform(key, shape) * (hi - lo) + lo)
    return dt + jnp.log(-jnp.expm1(-dt))


def setup_inputs(seed: int = 0) -> dict:
    key = jax.random.key(seed)
    ks = jax.random.split(key, 32)
    nrm = jax.random.normal
    d = D_MODEL
    f2 = 2 * FFN_HIDDEN
    gdn_qkv_w = 2 * GDN_KEY_W + GDN_VAL_W
    return {
        "x": nrm(ks[0], (BATCH, SEQ, d), jnp.float32),
        "c": nrm(ks[1], (BATCH, d), jnp.float32),
        "w_ada": nrm(ks[2], (DEPTH, d, 6 * d)) * (0.5 * d ** -0.5),
        "b_ada": 0.02 * nrm(ks[3], (DEPTH, 6 * d)),
        "norm1_w": 1.0 + 0.05 * nrm(ks[4], (DEPTH, d)),
        "w_in": nrm(ks[5], (DEPTH, d, N_IN)) * d ** -0.5,
        "gdn_conv_w": nrm(ks[6], (DEPTH, GDN_CONV, gdn_qkv_w)) * GDN_CONV ** -0.5,
        "gdn_a_log": jnp.log(jax.random.uniform(ks[7], (DEPTH, GDN_HEADS), minval=1.0, maxval=16.0)),
        "gdn_dt_bias": _log_uniform_dt_bias(ks[8], (DEPTH, GDN_HEADS)),
        "gdn_norm_w": 1.0 + 0.05 * nrm(ks[9], (DEPTH, GDN_DV)),
        "hgrn_lb_param": nrm(ks[10], (DEPTH, HGRN_KEY_W)),
        "hgrn_norm_w": 1.0 + 0.05 * nrm(ks[11], (DEPTH, HGRN_DV)),
        "ssd_conv_w": nrm(ks[12], (DEPTH, SSD_CONV, SSD_XBC_W)) * SSD_CONV ** -0.5,
        "ssd_conv_b": 0.02 * nrm(ks[13], (DEPTH, SSD_XBC_W)),
        "ssd_a_log": jnp.log(jax.random.uniform(ks[14], (DEPTH, SSD_HEADS), minval=1.0, maxval=16.0)),
        "ssd_dt_bias": _log_uniform_dt_bias(ks[15], (DEPTH, SSD_HEADS)),
        "ssd_d": 1.0 + 0.05 * nrm(ks[16], (DEPTH, SSD_HEADS)),
        "ssd_norm_w": 1.0 + 0.05 * nrm(ks[17], (DEPTH, SSD_INNER)),
        "w_br_a": nrm(ks[18], (DEPTH, GDN_VAL_W, d)) * GDN_VAL_W ** -0.5,
        "w_br_b": nrm(ks[19], (DEPTH, HGRN_VAL_W, d)) * HGRN_VAL_W ** -0.5,
        "w_br_c": nrm(ks[20], (DEPTH, SSD_INNER, d)) * SSD_INNER ** -0.5,
        "w_out": nrm(ks[21], (DEPTH, d, d)) * d ** -0.5,
        "norm2_w": 1.0 + 0.05 * nrm(ks[22], (DEPTH, d)),
        "ffn_w_up": nrm(ks[23], (DEPTH, d, f2)) * d ** -0.5,
        "ffn_conv_w": nrm(ks[24], (DEPTH, FFN_CONV, f2)) * FFN_CONV ** -0.5,
        "ffn_conv_b": 0.02 * nrm(ks[25], (DEPTH, f2)),
        "ffn_w_down": nrm(ks[26], (DEPTH, FFN_HIDDEN, d)) * FFN_HIDDEN ** -0.5,
        "final_norm_w": 1.0 + 0.05 * nrm(ks[27], (d,)),
    }


def reference(x, c, w_ada, b_ada, norm1_w, w_in, gdn_conv_w, gdn_a_log, gdn_dt_bias, gdn_norm_w,
              hgrn_lb_param, hgrn_norm_w, ssd_conv_w, ssd_conv_b, ssd_a_log, ssd_dt_bias, ssd_d,
              ssd_norm_w, w_br_a, w_br_b, w_br_c, w_out, norm2_w, ffn_w_up, ffn_conv_w, ffn_conv_b,
              ffn_w_down, final_norm_w):
    c_act = jax.nn.silu(c)
    lb_soft = jax.nn.softmax(hgrn_lb_param.astype(jnp.float32), axis=0)
    lower_bounds = jnp.cumsum(lb_soft, axis=0) - lb_soft[0]
    for l in range(DEPTH):
        mod = c_act @ w_ada[l] + b_ada[l]
        shift1, scale1, gate1, shift2, scale2, gate2 = jnp.split(mod, 6, axis=-1)
        h = modulate(rms_norm(x, norm1_w[l]), shift1, scale1)
        mix = token_mixing(h, lower_bounds[l], w_in[l], gdn_conv_w[l], gdn_a_log[l], gdn_dt_bias[l],
                           gdn_norm_w[l], hgrn_norm_w[l], ssd_conv_w[l], ssd_conv_b[l], ssd_a_log[l],
                           ssd_dt_bias[l], ssd_d[l], ssd_norm_w[l], w_br_a[l], w_br_b[l], w_br_c[l],
                           w_out[l])
        x = x + gate1[:, None, :] * mix
        h = modulate(rms_norm(x, norm2_w[l]), shift2, scale2)
        x = x + gate2[:, None, :] * conv_glu_ffn(h, ffn_w_up[l], ffn_conv_w[l], ffn_conv_b[l], ffn_w_down[l])
    return rms_norm(x, final_norm_w)
```

```python
import functools

import jax
import jax.numpy as jnp
from jax import lax
from jax.experimental import pallas as pl
from jax.experimental.pallas import tpu as pltpu

F32 = jnp.float32
BF16 = jnp.bfloat16
EPS = 1e-6

GDN_HEADS = 4
GDN_DK = 128
GDN_DV = 128
GDN_CONV = 4
GDN_CHUNK = 64
HGRN_HEADS = 4
HGRN_DK = 128
HGRN_CHUNK = 16
SSD_HEADS = 8
SSD_HEAD_DIM = 64
SSD_GROUPS = 2
SSD_STATE = 128
SSD_CONV = 4
FFN_CONV = 3

HALO = 8
VMEM_LIMIT = 56 * 1024 * 1024

_HI = lax.Precision.HIGHEST


def _mm(a, b):
    return jnp.dot(a.astype(BF16), b.astype(BF16), preferred_element_type=F32)


def _mm_nt(a, b):
    return lax.dot_general(a.astype(BF16), b.astype(BF16), (((1,), (1,)), ((), ())),
                           preferred_element_type=F32)


def _mm_tn(a, b):
    return lax.dot_general(a.astype(BF16), b.astype(BF16), (((0,), (0,)), ((), ())),
                           preferred_element_type=F32)


def _mm_hi(a, b):
    return jnp.dot(a, b, precision=_HI, preferred_element_type=F32)


def _sigmoid(x):
    return jax.nn.sigmoid(x)


def _silu(x):
    return x * jax.nn.sigmoid(x)


def _softplus(x):
    return jnp.maximum(x, 0.0) + jnp.log(1.0 + jnp.exp(-jnp.abs(x)))


def _mod_norm(x, nw, shift, scale):
    y = x * lax.rsqrt(jnp.mean(x * x, axis=-1, keepdims=True) + EPS) * nw
    return y * (1.0 + scale) + shift


def _tri(n, strict=False):
    r = lax.broadcasted_iota(jnp.int32, (n, n), 0)
    c = lax.broadcasted_iota(jnp.int32, (n, n), 1)
    return (r > c) if strict else (r >= c)


def _cparams(sem):
    return pltpu.CompilerParams(dimension_semantics=sem, vmem_limit_bytes=VMEM_LIMIT)


def _const_spec(shape):
    nd = len(shape)
    return pl.BlockSpec(shape, lambda *_: (0,) * nd)


def _ada_kernel(c_ref, w_ref, b_ref, o_ref):
    c = c_ref[...]
    o_ref[0] = _mm_hi(_silu(c), w_ref[0]) + b_ref[0]


def _ada_call(c, w_ada, b_ada):
    depth, d, n = w_ada.shape
    bsz = c.shape[0]
    tn = 1536
    return pl.pallas_call(
        _ada_kernel,
        grid=(depth, n // tn),
        in_specs=[
            pl.BlockSpec((bsz, d), lambda l, j: (0, 0)),
            pl.BlockSpec((1, d, tn), lambda l, j: (l, 0, j)),
            pl.BlockSpec((1, 1, tn), lambda l, j: (l, 0, j)),
        ],
        out_specs=pl.BlockSpec((1, bsz, tn), lambda l, j: (l, 0, j)),
        out_shape=jax.ShapeDtypeStruct((depth, bsz, n), F32),
        compiler_params=_cparams(("arbitrary", "arbitrary")),
    )(c, w_ada, b_ada.reshape(depth, 1, n))


def _gdn_chunk_head(q, k, v, g_col, g_row, beta, state, incl, strict, eye):
    c = q.shape[0]
    q = q * lax.rsqrt(jnp.sum(q * q, axis=-1, keepdims=True) + EPS) * (GDN_DK ** -0.5)
    k = k * lax.rsqrt(jnp.sum(k * k, axis=-1, keepdims=True) + EPS)
    decay = jnp.where(incl, jnp.exp(jnp.where(incl, g_col - g_row, 0.0)), 0.0)
    kb = k * beta
    m = jnp.where(strict, _mm_nt(kb, k) * decay, 0.0)
    exp_g = jnp.exp(g_col)
    inv = eye - m
    mp = m
    for _ in range(max(c.bit_length() - 2, 0)):
        mp = _mm_hi(mp, mp)
        inv = inv + _mm_hi(inv, mp)
    u = _mm_hi(inv, v * beta)
    w = _mm_hi(inv, kb * exp_g)
    attn = _mm_nt(q, k) * decay
    ws = _mm(jnp.concatenate([w, q * exp_g], axis=0), state)
    v_new = u - ws[:c]
    o = ws[c:] + _mm(attn, v_new)
    g_end = g_col[c - 1:c, :]
    k_end = k * jnp.exp(g_end - g_col)
    state = state * jnp.exp(g_end) + _mm_tn(k_end, v_new)
    return o, state


def _gdn_kernel(x_ref, mod_ref, nw_ref, w_ref, wsm_ref, wsmt_ref, cw_ref, pcol_ref, prow_ref,
                gnw_ref, o_ref, ext_ref, st_ref, *, ts):
    kw = GDN_HEADS * GDN_DK
    qkv_w = 2 * kw + GDN_HEADS * GDN_DV
    c = GDN_CHUNK

    @pl.when(pl.program_id(1) == 0)
    def _():
        st_ref[...] = jnp.zeros_like(st_ref)
        ext_ref[0:HALO, :] = jnp.zeros((HALO, qkv_w), F32)

    m = mod_ref[0]
    h = _mod_norm(x_ref[0], nw_ref[...], m[0:1], m[1:2]).astype(BF16)
    proj = jnp.dot(h, w_ref[...], preferred_element_type=F32)
    ext_ref[HALO:HALO + ts, :] = proj[:, :qkv_w]
    z = proj[:, qkv_w:]
    cw = cw_ref[...]
    acc = cw[GDN_CONV - 1:GDN_CONV] * proj[:, :qkv_w]
    for i in range(GDN_CONV - 1):
        off = HALO - (GDN_CONV - 1) + i
        acc = acc + cw[i:i + 1] * ext_ref[off:off + ts, :]
    ext_ref[0:HALO, :] = ext_ref[ts:ts + HALO, :]
    qkv = _silu(acc)

    sm = jnp.dot(h, wsm_ref[...], preferred_element_type=F32)
    smt = lax.dot_general(wsmt_ref[...], h, (((1,), (1,)), ((), ())), preferred_element_type=F32)
    pcol = pcol_ref[...]
    prow = prow_ref[...]
    g_cols = -jnp.exp(pcol[0:1]) * _softplus(sm + pcol[1:2])
    beta_cols = _sigmoid(sm)
    g_rows = -jnp.exp(prow[:, 0:1]) * _softplus(smt + prow[:, 1:2])

    incl = _tri(c)
    strict = _tri(c, strict=True)
    eye = jnp.where(incl & (~strict), 1.0, 0.0).astype(F32)
    low = jnp.where(incl, 1.0, 0.0).astype(F32)
    upp = jnp.where(lax.broadcasted_iota(jnp.int32, (c, c), 0)
                    <= lax.broadcasted_iota(jnp.int32, (c, c), 1), 1.0, 0.0).astype(F32)
    gnw = gnw_ref[...]

    for ci in range(ts // c):
        r0 = ci * c
        gc_all = _mm_hi(low, g_cols[r0:r0 + c, :])
        gr_all = _mm_hi(g_rows[:, r0:r0 + c], upp)
        for hd in range(GDN_HEADS):
            q = qkv[r0:r0 + c, hd * GDN_DK:(hd + 1) * GDN_DK]
            k = qkv[r0:r0 + c, kw + hd * GDN_DK:kw + (hd + 1) * GDN_DK]
            v = qkv[r0:r0 + c, 2 * kw + hd * GDN_DV:2 * kw + (hd + 1) * GDN_DV]
            o, st = _gdn_chunk_head(
                q, k, v, gc_all[:, hd:hd + 1], gr_all[hd:hd + 1, :],
                beta_cols[r0:r0 + c, GDN_HEADS + hd:GDN_HEADS + hd + 1],
                st_ref[hd], incl, strict, eye)
            st_ref[hd] = st
            zz = z[r0:r0 + c, hd * GDN_DV:(hd + 1) * GDN_DV]
            o = o * lax.rsqrt(jnp.mean(o * o, axis=-1, keepdims=True) + EPS) * gnw * _silu(zz)
            o_ref[0, r0:r0 + c, hd * GDN_DV:(hd + 1) * GDN_DV] = o.astype(o_ref.dtype)


def _gdn_call(x, mod, nw, w, wsm, wsmt, cw, pcol, prow, gnw, ts):
    bsz, s, d = x.shape
    kw = GDN_HEADS * GDN_DK
    vw = GDN_HEADS * GDN_DV
    qkv_w = 2 * kw + vw
    return pl.pallas_call(
        functools.partial(_gdn_kernel, ts=ts),
        grid=(bsz, s // ts),
        in_specs=[
            pl.BlockSpec((1, ts, d), lambda b, j: (b, j, 0)),
            pl.BlockSpec((1, 6, d), lambda b, j: (b, 0, 0)),
            _const_spec(nw.shape), _const_spec(w.shape), _const_spec(wsm.shape),
            _const_spec(wsmt.shape), _const_spec(cw.shape), _const_spec(pcol.shape),
            _const_spec(prow.shape), _const_spec(gnw.shape),
        ],
        out_specs=pl.BlockSpec((1, ts, vw), lambda b, j: (b, j, 0)),
        out_shape=jax.ShapeDtypeStruct((bsz, s, vw), BF16),
        scratch_shapes=[pltpu.VMEM((ts + HALO, qkv_w), F32),
                        pltpu.VMEM((GDN_HEADS, GDN_DK, GDN_DV), F32)],
        compiler_params=_cparams(("parallel", "arbitrary")),
    )(x, mod, nw, w, wsm, wsmt, cw, pcol, prow, gnw)


def _group_cumsum_rows(x, group):
    rows = lax.broadcasted_iota(jnp.int32, x.shape, 0) % group
    d = 1
    while d < group:
        x = x + jnp.where(rows >= d, pltpu.roll(x, d, axis=0), 0.0)
        d *= 2
    return x


def _hgrn_kernel(x_ref, mod_ref, nw_ref, w_ref, lbp_ref, gnw_ref, o_ref,
                 q_ref, k_ref, v_ref, g_ref, z_ref, st_ref, *, ts, layer):
    kw = HGRN_HEADS * HGRN_DK
    c = HGRN_CHUNK

    @pl.when(pl.program_id(1) == 0)
    def _():
        st_ref[...] = jnp.zeros_like(st_ref)

    m = mod_ref[0]
    h = _mod_norm(x_ref[0], nw_ref[...], m[0:1], m[1:2]).astype(BF16)
    proj = jnp.dot(h, w_ref[...], preferred_element_type=F32)

    lbp = lbp_ref[...]
    e = jnp.exp(lbp - jnp.max(lbp, axis=0, keepdims=True))
    soft = e / jnp.sum(e, axis=0, keepdims=True)
    lb = jnp.zeros((1, kw), F32)
    for i in range(1, layer + 1):
        lb = lb + soft[i:i + 1]

    f_raw = proj[:, kw:2 * kw]
    q_ref[...] = _silu(proj[:, :kw])
    k_ref[...] = (1.0 - lb) * _sigmoid(-f_raw)
    v_ref[...] = proj[:, 2 * kw:3 * kw]
    z_ref[...] = proj[:, 3 * kw:]
    g_ref[...] = _group_cumsum_rows(jnp.log(lb + (1.0 - lb) * _sigmoid(f_raw)), c)

    incl = _tri(c)
    gnw = gnw_ref[...]

    def step(n, carry):
        rows = pl.ds(pl.multiple_of(n * c, c), c)
        g = g_ref[rows, :]
        q = q_ref[rows, :]
        k = k_ref[rows, :]
        v = v_ref[rows, :]
        zz = z_ref[rows, :]
        g_mid = g[c // 2 - 1:c // 2, :]
        g_last = g[c - 1:c, :]
        qs = q * jnp.exp(g - g_mid)
        ks = k * jnp.exp(g_mid - g)
        qg = q * jnp.exp(g)
        k_end = k * jnp.exp(g_last - g)
        g_end = jnp.exp(g_last)
        for hd in range(HGRN_HEADS):
            sl = slice(hd * HGRN_DK, (hd + 1) * HGRN_DK)
            st = st_ref[hd]
            attn = jnp.where(incl, _mm_nt(qs[:, sl], ks[:, sl]), 0.0)
            o = _mm(attn, v[:, sl]) + _mm_nt(qg[:, sl], st)
            st_ref[hd] = st * g_end[:, sl] + _mm_tn(v[:, sl], k_end[:, sl])
            o = o * lax.rsqrt(jnp.mean(o * o, axis=-1, keepdims=True) + EPS) * gnw * _silu(zz[:, sl])
            o_ref[0, rows, sl] = o.astype(o_ref.dtype)
        return carry

    lax.fori_loop(0, ts // c, step, 0)


def _hgrn_call(x, mod, nw, w, lbp, gnw, ts, layer):
    bsz, s, d = x.shape
    kw = HGRN_HEADS * HGRN_DK
    return pl.pallas_call(
        functools.partial(_hgrn_kernel, ts=ts, layer=layer),
        grid=(bsz, s // ts),
        in_specs=[
            pl.BlockSpec((1, ts, d), lambda b, j: (b, j, 0)),
            pl.BlockSpec((1, 6, d), lambda b, j: (b, 0, 0)),
            _const_spec(nw.shape), _const_spec(w.shape), _const_spec(lbp.shape),
            _const_spec(gnw.shape),
        ],
        out_specs=pl.BlockSpec((1, ts, kw), lambda b, j: (b, j, 0)),
        out_shape=jax.ShapeDtypeStruct((bsz, s, kw), BF16),
        scratch_shapes=[pltpu.VMEM((ts, kw), F32) for _ in range(5)]
        + [pltpu.VMEM((HGRN_HEADS, HGRN_DK, HGRN_DK), F32)],
        compiler_params=_cparams(("parallel", "arbitrary")),
    )(x, mod, nw, w, lbp, gnw)


def _ssd_kernel(x_ref, mod_ref, nw_ref, w_ref, wdt_ref, wdtt_ref, cw_ref, cb_ref, pcol_ref,
                prow_ref, dsk_ref, gnw_ref, o_ref, ext_ref, y_ref, st_ref, *, ts):
    inner = SSD_HEADS * SSD_HEAD_DIM
    xbc_w = inner + 2 * SSD_GROUPS * SSD_STATE
    hpg = SSD_HEADS // SSD_GROUPS
    p = SSD_HEAD_DIM

    @pl.when(pl.program_id(1) == 0)
    def _():
        st_ref[...] = jnp.zeros_like(st_ref)
        ext_ref[0:HALO, :] = jnp.zeros((HALO, xbc_w), F32)

    m = mod_ref[0]
    h = _mod_norm(x_ref[0], nw_ref[...], m[0:1], m[1:2]).astype(BF16)
    proj = jnp.dot(h, w_ref[...], preferred_element_type=F32)
    z = proj[:, :inner]
    ext_ref[HALO:HALO + ts, :] = proj[:, inner:]
    cw = cw_ref[...]
    acc = cw[SSD_CONV - 1:SSD_CONV] * proj[:, inner:] + cb_ref[...]
    for i in range(SSD_CONV - 1):
        off = HALO - (SSD_CONV - 1) + i
        acc = acc + cw[i:i + 1] * ext_ref[off:off + ts, :]
    ext_ref[0:HALO, :] = ext_ref[ts:ts + HALO, :]
    xbc = _silu(acc)
    xs = xbc[:, :inner]

    pcol = pcol_ref[...]
    prow = prow_ref[...]
    dt_cols = _softplus(jnp.dot(h, wdt_ref[...], preferred_element_type=F32) + pcol[1:2])
    dt_rows = _softplus(lax.dot_general(wdtt_ref[...], h, (((1,), (1,)), ((), ())),
                                        preferred_element_type=F32) + prow[:, 1:2])
    incl = _tri(ts)
    low = jnp.where(incl, 1.0, 0.0).astype(F32)
    upp = jnp.where(lax.broadcasted_iota(jnp.int32, (ts, ts), 0)
                    <= lax.broadcasted_iota(jnp.int32, (ts, ts), 1), 1.0, 0.0).astype(F32)
    acs_cols = _mm_hi(low, dt_cols * (-jnp.exp(pcol[0:1])))
    acs_rows = _mm_hi(dt_rows * (-jnp.exp(prow[:, 0:1])), upp)

    for g in range(SSD_GROUPS):
        bm = xbc[:, inner + g * SSD_STATE:inner + (g + 1) * SSD_STATE]
        cm = xbc[:, inner + (SSD_GROUPS + g) * SSD_STATE:inner + (SSD_GROUPS + g + 1) * SSD_STATE]
        cb = _mm_nt(cm, bm)
        for hh in range(hpg):
            hd = g * hpg + hh
            a_col = acs_cols[:, hd:hd + 1]
            a_row = acs_rows[hd:hd + 1, :]
            seg = jnp.where(incl, jnp.exp(jnp.where(incl, a_col - a_row, 0.0)), 0.0)
            x_h = xs[:, hd * p:(hd + 1) * p]
            xdt = x_h * dt_cols[:, hd:hd + 1]
            st = st_ref[hd]
            y = _mm(cb * seg, xdt) + _mm_nt(cm, st) * jnp.exp(a_col)
            last = a_col[ts - 1:ts, :]
            st_ref[hd] = st * jnp.exp(last) + _mm_tn(xdt * jnp.exp(last - a_col), bm)
            y_ref[:, hd * p:(hd + 1) * p] = y

    y = (y_ref[...] + dsk_ref[...] * xs) * _silu(z)
    gnw = gnw_ref[...]
    gw = hpg * p
    for g in range(SSD_GROUPS):
        yg = y[:, g * gw:(g + 1) * gw]
        yg = yg * lax.rsqrt(jnp.mean(yg * yg, axis=-1, keepdims=True) + EPS) * gnw[:, g * gw:(g + 1) * gw]
        o_ref[0, :, g * gw:(g + 1) * gw] = yg.astype(o_ref.dtype)


def _ssd_call(x, mod, nw, w, wdt, wdtt, cw, cb, pcol, prow, dsk, gnw, ts):
    bsz, s, d = x.shape
    inner = SSD_HEADS * SSD_HEAD_DIM
    xbc_w = inner + 2 * SSD_GROUPS * SSD_STATE
    return pl.pallas_call(
        functools.partial(_ssd_kernel, ts=ts),
        grid=(bsz, s // ts),
        in_specs=[
            pl.BlockSpec((1, ts, d), lambda b, j: (b, j, 0)),
            pl.BlockSpec((1, 6, d), lambda b, j: (b, 0, 0)),
            _const_spec(nw.shape), _const_spec(w.shape), _const_spec(wdt.shape),
            _const_spec(wdtt.shape), _const_spec(cw.shape), _const_spec(cb.shape),
            _const_spec(pcol.shape), _const_spec(prow.shape), _const_spec(dsk.shape),
            _const_spec(gnw.shape),
        ],
        out_specs=pl.BlockSpec((1, ts, inner), lambda b, j: (b, j, 0)),
        out_shape=jax.ShapeDtypeStruct((bsz, s, inner), BF16),
        scratch_shapes=[pltpu.VMEM((ts + HALO, xbc_w), F32),
                        pltpu.VMEM((ts, inner), F32),
                        pltpu.VMEM((SSD_HEADS, SSD_HEAD_DIM, SSD_STATE), F32)],
        compiler_params=_cparams(("parallel", "arbitrary")),
    )(x, mod, nw, w, wdt, wdtt, cw, cb, pcol, prow, dsk, gnw)


def _merge_kernel(x_ref, mod_ref, nw_ref, oa_ref, ob_ref, oc_ref, wg_ref, wa_ref, wb_ref, wc_ref,
                  wo_ref, o_ref):
    d = x_ref.shape[-1]
    x = x_ref[0]
    m = mod_ref[0]
    h = _mod_norm(x, nw_ref[...], m[0:1], m[1:2]).astype(BF16)
    gates = _sigmoid(jnp.dot(h, wg_ref[...], preferred_element_type=F32))
    merged = (gates[:, :d] * jnp.dot(oa_ref[0], wa_ref[...], preferred_element_type=F32)
              + gates[:, d:2 * d] * jnp.dot(ob_ref[0], wb_ref[...], preferred_element_type=F32)
              + gates[:, 2 * d:] * jnp.dot(oc_ref[0], wc_ref[...], preferred_element_type=F32))
    mix = jnp.dot(merged.astype(BF16), wo_ref[...], preferred_element_type=F32)
    o_ref[0] = x + m[2:3] * mix


def _merge_call(x, mod, nw, oa, ob, oc, wg, wa, wb, wc, wo, tm):
    bsz, s, d = x.shape
    tok = lambda w: pl.BlockSpec((1, tm, w), lambda b, j: (b, j, 0))
    return pl.pallas_call(
        _merge_kernel,
        grid=(bsz, s // tm),
        in_specs=[
            tok(d), pl.BlockSpec((1, 6, d), lambda b, j: (b, 0, 0)), _const_spec(nw.shape),
            tok(oa.shape[-1]), tok(ob.shape[-1]), tok(oc.shape[-1]),
            _const_spec(wg.shape), _const_spec(wa.shape), _const_spec(wb.shape),
            _const_spec(wc.shape), _const_spec(wo.shape),
        ],
        out_specs=tok(d),
        out_shape=jax.ShapeDtypeStruct((bsz, s, d), F32),
        compiler_params=_cparams(("parallel", "parallel")),
    )(x, mod, nw, oa, ob, oc, wg, wa, wb, wc, wo)


def _ffn_kernel(x_ref, mod_ref, nw_ref, wup_ref, cw_ref, cb_ref, wdn_ref, fnw_ref, o_ref,
                ext_ref, tail_ref, *, tm, n_split, final):
    hidden = wdn_ref.shape[0]
    fc = hidden // n_split

    @pl.when(pl.program_id(1) == 0)
    def _():
        tail_ref[...] = jnp.zeros_like(tail_ref)

    x = x_ref[0]
    m = mod_ref[0]
    h = _mod_norm(x, nw_ref[...], m[3:4], m[4:5]).astype(BF16)

    def conv_part(c0):
        u = jnp.dot(h, wup_ref[:, c0:c0 + fc], preferred_element_type=F32)
        ext_ref[0:HALO, :] = tail_ref[:, c0:c0 + fc]
        ext_ref[HALO:HALO + tm, :] = u
        cw = cw_ref[:, c0:c0 + fc]
        acc = cw[FFN_CONV - 1:FFN_CONV] * u + cb_ref[:, c0:c0 + fc]
        for i in range(FFN_CONV - 1):
            off = HALO - (FFN_CONV - 1) + i
            acc = acc + cw[i:i + 1] * ext_ref[off:off + tm, :]
        tail_ref[:, c0:c0 + fc] = ext_ref[tm:tm + HALO, :]
        return acc

    y = jnp.zeros_like(x)
    for i in range(n_split):
        gate = conv_part(i * fc)
        val = conv_part(hidden + i * fc)
        act = (_silu(gate) * val).astype(BF16)
        y = y + jnp.dot(act, wdn_ref[i * fc:(i + 1) * fc, :], preferred_element_type=F32)
    out = x + m[5:6] * y
    if final:
        out = out * lax.rsqrt(jnp.mean(out * out, axis=-1, keepdims=True) + EPS) * fnw_ref[...]
    o_ref[0] = out


def _ffn_call(x, mod, nw, wup, cw, cb, wdn, fnw, tm, final):
    bsz, s, d = x.shape
    hidden = wdn.shape[0]
    n_split = 2
    tok = pl.BlockSpec((1, tm, d), lambda b, j: (b, j, 0))
    single = lambda a: pl.BlockSpec(a.shape, lambda *_: (0,) * a.ndim, pipeline_mode=pl.Buffered(1))
    return pl.pallas_call(
        functools.partial(_ffn_kernel, tm=tm, n_split=n_split, final=final),
        grid=(bsz, s // tm),
        in_specs=[
            tok, pl.BlockSpec((1, 6, d), lambda b, j: (b, 0, 0)), _const_spec(nw.shape),
            single(wup), _const_spec(cw.shape), _const_spec(cb.shape), single(wdn),
            _const_spec(fnw.shape),
        ],
        out_specs=tok,
        out_shape=jax.ShapeDtypeStruct((bsz, s, d), F32),
        scratch_shapes=[pltpu.VMEM((tm + HALO, hidden // n_split), F32),
                        pltpu.VMEM((HALO, 2 * hidden), F32)],
        compiler_params=_cparams(("parallel", "arbitrary")),
    )(x, mod, nw, wup, cw, cb, wdn, fnw)


def _pad_lanes(a, width=128):
    return jnp.pad(a, ((0, 0), (0, width - a.shape[-1])))


def kernel(x, c, w_ada, b_ada, norm1_w, w_in, gdn_conv_w, gdn_a_log, gdn_dt_bias, gdn_norm_w,
           hgrn_lb_param, hgrn_norm_w, ssd_conv_w, ssd_conv_b, ssd_a_log, ssd_dt_bias, ssd_d,
           ssd_norm_w, w_br_a, w_br_b, w_br_c, w_out, norm2_w, ffn_w_up, ffn_conv_w, ffn_conv_b,
           ffn_w_down, final_norm_w):
    bsz, s, d = x.shape
    depth = w_in.shape[0]
    gk = GDN_HEADS * GDN_DK
    gv = GDN_HEADS * GDN_DV
    hk = HGRN_HEADS * HGRN_DK
    inner = SSD_HEADS * SSD_HEAD_DIM
    xbc_w = inner + 2 * SSD_GROUPS * SSD_STATE

    sizes = (2 * gk + gv, GDN_HEADS, GDN_HEADS, gv, hk, hk, hk, hk, inner, xbc_w, SSD_HEADS, 3 * d)
    offs = [0]
    for sz in sizes:
        offs.append(offs[-1] + sz)
    (o_qkv, o_a, o_b, o_z, o_hq, _, _, _, o_sz, o_xbc, o_dt, o_gate, o_end) = offs

    ts_gdn = min(256, s)
    ts_hgrn = min(512, s)
    ts_ssd = min(256, s)
    tm_merge = min(512, s)
    tm_ffn = min(256, s)

    mod = _ada_call(c, w_ada, b_ada).reshape(depth, bsz, 6, d)
    fnw = final_norm_w.reshape(1, d)

    for l in range(depth):
        wl = w_in[l]
        mod_l = mod[l]
        nw1 = norm1_w[l].reshape(1, d)

        w_gdn = jnp.concatenate([wl[:, o_qkv:o_a], wl[:, o_z:o_hq]], axis=1).astype(BF16)
        w_ab = wl[:, o_a:o_z]
        o_ga = _gdn_call(
            x, mod_l, nw1, w_gdn, _pad_lanes(w_ab).astype(BF16), w_ab.T.astype(BF16),
            gdn_conv_w[l],
            _pad_lanes(jnp.stack([gdn_a_log[l], gdn_dt_bias[l]])),
            jnp.pad(jnp.stack([gdn_a_log[l], gdn_dt_bias[l]], axis=1), ((0, GDN_HEADS), (0, 0))),
            gdn_norm_w[l].reshape(1, GDN_DV), ts_gdn)

        o_hg = _hgrn_call(x, mod_l, nw1, wl[:, o_hq:o_sz].astype(BF16), hgrn_lb_param,
                          hgrn_norm_w[l].reshape(1, HGRN_DK), ts_hgrn, l)

        w_dt = wl[:, o_dt:o_gate]
        o_sd = _ssd_call(
            x, mod_l, nw1, wl[:, o_sz:o_dt].astype(BF16), _pad_lanes(w_dt).astype(BF16),
            w_dt.T.astype(BF16), ssd_conv_w[l], ssd_conv_b[l].reshape(1, xbc_w),
            _pad_lanes(jnp.stack([ssd_a_log[l], ssd_dt_bias[l]])),
            jnp.stack([ssd_a_log[l], ssd_dt_bias[l]], axis=1),
            jnp.repeat(ssd_d[l], SSD_HEAD_DIM).reshape(1, inner),
            ssd_norm_w[l].reshape(1, inner), ts_ssd)

        x = _merge_call(x, mod_l, nw1, o_ga, o_hg, o_sd, wl[:, o_gate:o_end].astype(BF16),
                        w_br_a[l].astype(BF16), w_br_b[l].astype(BF16), w_br_c[l].astype(BF16),
                        w_out[l].astype(BF16), tm_merge)

        x = _ffn_call(x, mod_l, norm2_w[l].reshape(1, d), ffn_w_up[l].astype(BF16), ffn_conv_w[l],
                      ffn_conv_b[l].reshape(1, -1), ffn_w_down[l].astype(BF16), fnw, tm_ffn,
                      final=(l == depth - 1))
    return x
```

```python
import functools

import jax
import jax.numpy as jnp
from jax import lax
from jax.experimental import pallas as pl
from jax.experimental.pallas import tpu as pltpu

F32 = jnp.float32
BF16 = jnp.bfloat16
EPS = 1e-6

GDN_HEADS = 4
GDN_DK = 128
GDN_DV = 128
GDN_CONV = 4
GDN_CHUNK = 64
HGRN_HEADS = 4
HGRN_DK = 128
HGRN_CHUNK = 16
SSD_HEADS = 8
SSD_HEAD_DIM = 64
SSD_GROUPS = 2
SSD_STATE = 128
SSD_CONV = 4
FFN_CONV = 3

HALO = 8
VMEM_LIMIT = 56 * 1024 * 1024

_HI = lax.Precision.HIGHEST


def _mm(a, b):
    return jnp.dot(a.astype(BF16), b.astype(BF16), preferred_element_type=F32)


def _mm_nt(a, b):
    return lax.dot_general(a.astype(BF16), b.astype(BF16), (((1,), (1,)), ((), ())),
                           preferred_element_type=F32)


def _mm_tn(a, b):
    return lax.dot_general(a.astype(BF16), b.astype(BF16), (((0,), (0,)), ((), ())),
                           preferred_element_type=F32)


def _mm_hi(a, b):
    return jnp.dot(a, b, precision=_HI, preferred_element_type=F32)


def _sigmoid(x):
    return jax.nn.sigmoid(x)


def _silu(x):
    return x * jax.nn.sigmoid(x)


def _softplus(x):
    return jnp.maximum(x, 0.0) + jnp.log(1.0 + jnp.exp(-jnp.abs(x)))


def _mod_norm(x, nw, shift, scale):
    y = x * lax.rsqrt(jnp.mean(x * x, axis=-1, keepdims=True) + EPS) * nw
    return y * (1.0 + scale) + shift


def _tri(n, strict=False):
    r = lax.broadcasted_iota(jnp.int32, (n, n), 0)
    c = lax.broadcasted_iota(jnp.int32, (n, n), 1)
    return (r > c) if strict else (r >= c)


def _cparams(sem):
    return pltpu.CompilerParams(dimension_semantics=sem, vmem_limit_bytes=VMEM_LIMIT)


def _const_spec(shape):
    nd = len(shape)
    return pl.BlockSpec(shape, lambda *_: (0,) * nd)


def _ada_kernel(c_ref, w_ref, b_ref, o_ref):
    c = c_ref[...]
    o_ref[0] = _mm_hi(_silu(c), w_ref[0]) + b_ref[0]


def _ada_call(c, w_ada, b_ada):
    depth, d, n = w_ada.shape
    bsz = c.shape[0]
    tn = 1536
    return pl.pallas_call(
        _ada_kernel,
        grid=(depth, n // tn),
        in_specs=[
            pl.BlockSpec((bsz, d), lambda l, j: (0, 0)),
            pl.BlockSpec((1, d, tn), lambda l, j: (l, 0, j)),
            pl.BlockSpec((1, 1, tn), lambda l, j: (l, 0, j)),
        ],
        out_specs=pl.BlockSpec((1, bsz, tn), lambda l, j: (l, 0, j)),
        out_shape=jax.ShapeDtypeStruct((depth, bsz, n), F32),
        compiler_params=_cparams(("arbitrary", "arbitrary")),
    )(c, w_ada, b_ada.reshape(depth, 1, n))


def _split3(a):
    hi = a.astype(BF16)
    lo = (a - hi.astype(F32)).astype(BF16)
    return hi, lo


def _dot3(a_hi, a_lo, b_hi, b_lo):
    f = lambda p, q: jnp.dot(p, q, preferred_element_type=F32)
    return f(a_hi, b_hi) + (f(a_lo, b_hi) + f(a_hi, b_lo))


def _unit_lower_inverse(ms, eye):
    c = eye.shape[0]
    n_steps = c.bit_length() - 1
    qs, ps = [], []
    for m in ms:
        hi, lo = _split3(m)
        qs.append(_dot3(hi, lo, hi, lo))
        ps.append(eye - m)
    for step in range(1, n_steps):
        last = step == n_steps - 1
        for i in range(len(ms)):
            if last:
                p_hi, p_lo = _split3(ps[i])
                q_hi, q_lo = _split3(qs[i])
                ps[i] = ps[i] + _dot3(p_hi, p_lo, q_hi, q_lo)
            else:
                hi, lo = _split3(jnp.concatenate([qs[i], ps[i]], axis=0))
                r = _dot3(hi, lo, hi[:c], lo[:c])
                qs[i] = r[:c]
                ps[i] = ps[i] + r[c:]
    return ps


def _gdn_kernel(x_ref, mod_ref, nw_ref, w_ref, wsm_ref, wsmt_ref, cw_ref, pcol_ref, prow_ref,
                gnw_ref, o_ref, ext_ref, st_ref, *, ts):
    kw = GDN_HEADS * GDN_DK
    qkv_w = 2 * kw + GDN_HEADS * GDN_DV
    c = GDN_CHUNK

    @pl.when(pl.program_id(1) == 0)
    def _():
        st_ref[...] = jnp.zeros_like(st_ref)
        ext_ref[0:HALO, :] = jnp.zeros((HALO, qkv_w), F32)

    m = mod_ref[0]
    h = _mod_norm(x_ref[0], nw_ref[...], m[0:1], m[1:2]).astype(BF16)
    proj = jnp.dot(h, w_ref[...], preferred_element_type=F32)
    ext_ref[HALO:HALO + ts, :] = proj[:, :qkv_w]
    z = proj[:, qkv_w:]
    cw = cw_ref[...]
    acc = cw[GDN_CONV - 1:GDN_CONV] * proj[:, :qkv_w]
    for i in range(GDN_CONV - 1):
        off = HALO - (GDN_CONV - 1) + i
        acc = acc + cw[i:i + 1] * ext_ref[off:off + ts, :]
    ext_ref[0:HALO, :] = ext_ref[ts:ts + HALO, :]
    qkv = _silu(acc)

    sm = jnp.dot(h, wsm_ref[...], preferred_element_type=F32)
    smt = lax.dot_general(wsmt_ref[...], h, (((1,), (1,)), ((), ())), preferred_element_type=F32)
    pcol = pcol_ref[...]
    prow = prow_ref[...]
    g_cols = -jnp.exp(pcol[0:1]) * _softplus(sm + pcol[1:2])
    beta_cols = _sigmoid(sm)
    g_rows = -jnp.exp(prow[:, 0:1]) * _softplus(smt + prow[:, 1:2])

    incl = _tri(c)
    strict = _tri(c, strict=True)
    eye = jnp.where(incl & (~strict), 1.0, 0.0).astype(F32)
    rr = lax.broadcasted_iota(jnp.int32, (ts, ts), 0)
    cc = lax.broadcasted_iota(jnp.int32, (ts, ts), 1)
    same = (rr // c) == (cc // c)
    gc_all = _mm_hi(jnp.where(same & (rr >= cc), 1.0, 0.0).astype(F32), g_cols)
    gr_all = _mm_hi(g_rows, jnp.where(same & (rr <= cc), 1.0, 0.0).astype(F32))
    gnw = gnw_ref[...]

    pairs = [(ci, hd) for ci in range(ts // c) for hd in range(GDN_HEADS)]
    qn, kn, dec, kbs, egs, ms = {}, {}, {}, {}, {}, []
    for ci, hd in pairs:
        r0 = ci * c
        q = qkv[r0:r0 + c, hd * GDN_DK:(hd + 1) * GDN_DK]
        k = qkv[r0:r0 + c, kw + hd * GDN_DK:kw + (hd + 1) * GDN_DK]
        q = q * lax.rsqrt(jnp.sum(q * q, axis=-1, keepdims=True) + EPS) * (GDN_DK ** -0.5)
        k = k * lax.rsqrt(jnp.sum(k * k, axis=-1, keepdims=True) + EPS)
        g_col = gc_all[r0:r0 + c, hd:hd + 1]
        g_row = gr_all[hd:hd + 1, r0:r0 + c]
        decay = jnp.where(incl, jnp.exp(jnp.where(incl, g_col - g_row, 0.0)), 0.0)
        kb = k * beta_cols[r0:r0 + c, GDN_HEADS + hd:GDN_HEADS + hd + 1]
        qn[ci, hd], kn[ci, hd], dec[ci, hd], kbs[ci, hd] = q, k, decay, kb
        egs[ci, hd] = jnp.exp(g_col)
        ms.append(jnp.where(strict, _mm_nt(kb, k) * decay, 0.0))
    invs = _unit_lower_inverse(ms, eye)
    uw, attn = {}, {}
    for (ci, hd), inv in zip(pairs, invs):
        r0 = ci * c
        v = qkv[r0:r0 + c, 2 * kw + hd * GDN_DV:2 * kw + (hd + 1) * GDN_DV]
        beta = beta_cols[r0:r0 + c, GDN_HEADS + hd:GDN_HEADS + hd + 1]
        rhs_hi, rhs_lo = _split3(jnp.concatenate([v * beta, kbs[ci, hd] * egs[ci, hd]], axis=1))
        inv_hi, inv_lo = _split3(inv)
        uw[ci, hd] = _dot3(inv_hi, inv_lo, rhs_hi, rhs_lo)
        attn[ci, hd] = _mm_nt(qn[ci, hd], kn[ci, hd]) * dec[ci, hd]

    for ci in range(ts // c):
        r0 = ci * c
        for hd in range(GDN_HEADS):
            st = st_ref[hd]
            u = uw[ci, hd][:, :GDN_DV]
            w = uw[ci, hd][:, GDN_DV:]
            ws = _mm(jnp.concatenate([w, qn[ci, hd] * egs[ci, hd]], axis=0), st)
            v_new = u - ws[:c]
            o = ws[c:] + _mm(attn[ci, hd], v_new)
            g_col = gc_all[r0:r0 + c, hd:hd + 1]
            g_end = g_col[c - 1:c, :]
            k_end = kn[ci, hd] * jnp.exp(g_end - g_col)
            st_ref[hd] = st * jnp.exp(g_end) + _mm_tn(k_end, v_new)
            zz = z[r0:r0 + c, hd * GDN_DV:(hd + 1) * GDN_DV]
            o = o * lax.rsqrt(jnp.mean(o * o, axis=-1, keepdims=True) + EPS) * gnw * _silu(zz)
            o_ref[0, r0:r0 + c, hd * GDN_DV:(hd + 1) * GDN_DV] = o.astype(o_ref.dtype)


def _gdn_call(x, mod, nw, w, wsm, wsmt, cw, pcol, prow, gnw, ts):
    bsz, s, d = x.shape
    kw = GDN_HEADS * GDN_DK
    vw = GDN_HEADS * GDN_DV
    qkv_w = 2 * kw + vw
    return pl.pallas_call(
        functools.partial(_gdn_kernel, ts=ts),
        grid=(bsz, s // ts),
        in_specs=[
            pl.BlockSpec((1, ts, d), lambda b, j: (b, j, 0)),
            pl.BlockSpec((1, 6, d), lambda b, j: (b, 0, 0)),
            _const_spec(nw.shape), _const_spec(w.shape), _const_spec(wsm.shape),
            _const_spec(wsmt.shape), _const_spec(cw.shape), _const_spec(pcol.shape),
            _const_spec(prow.shape), _const_spec(gnw.shape),
        ],
        out_specs=pl.BlockSpec((1, ts, vw), lambda b, j: (b, j, 0)),
        out_shape=jax.ShapeDtypeStruct((bsz, s, vw), BF16),
        scratch_shapes=[pltpu.VMEM((ts + HALO, qkv_w), F32),
                        pltpu.VMEM((GDN_HEADS, GDN_DK, GDN_DV), F32)],
        compiler_params=_cparams(("parallel", "arbitrary")),
    )(x, mod, nw, w, wsm, wsmt, cw, pcol, prow, gnw)


def _group_cumsum_rows(x, group):
    rows = lax.broadcasted_iota(jnp.int32, x.shape, 0) % group
    d = 1
    while d < group:
        x = x + jnp.where(rows >= d, pltpu.roll(x, d, axis=0), 0.0)
        d *= 2
    return x


def _hgrn_kernel(x_ref, mod_ref, nw_ref, w_ref, lbp_ref, gnw_ref, o_ref,
                 q_ref, k_ref, v_ref, g_ref, z_ref, st_ref, *, ts, layer):
    kw = HGRN_HEADS * HGRN_DK
    c = HGRN_CHUNK

    @pl.when(pl.program_id(1) == 0)
    def _():
        st_ref[...] = jnp.zeros_like(st_ref)

    m = mod_ref[0]
    h = _mod_norm(x_ref[0], nw_ref[...], m[0:1], m[1:2]).astype(BF16)
    proj = jnp.dot(h, w_ref[...], preferred_element_type=F32)

    lbp = lbp_ref[...]
    e = jnp.exp(lbp - jnp.max(lbp, axis=0, keepdims=True))
    soft = e / jnp.sum(e, axis=0, keepdims=True)
    lb = jnp.zeros((1, kw), F32)
    for i in range(1, layer + 1):
        lb = lb + soft[i:i + 1]

    f_raw = proj[:, kw:2 * kw]
    q_ref[...] = _silu(proj[:, :kw])
    k_ref[...] = (1.0 - lb) * _sigmoid(-f_raw)
    v_ref[...] = proj[:, 2 * kw:3 * kw]
    z_ref[...] = proj[:, 3 * kw:]
    g_ref[...] = _group_cumsum_rows(jnp.log(lb + (1.0 - lb) * _sigmoid(f_raw)), c)

    incl = _tri(c)
    gnw = gnw_ref[...]

    def step(n, carry):
        rows = pl.ds(pl.multiple_of(n * c, c), c)
        g = g_ref[rows, :]
        q = q_ref[rows, :]
        k = k_ref[rows, :]
        v = v_ref[rows, :]
        zz = z_ref[rows, :]
        g_mid = g[c // 2 - 1:c // 2, :]
        g_last = g[c - 1:c, :]
        qs = q * jnp.exp(g - g_mid)
        ks = k * jnp.exp(g_mid - g)
        qg = q * jnp.exp(g)
        k_end = k * jnp.exp(g_last - g)
        g_end = jnp.exp(g_last)
        for hd in range(HGRN_HEADS):
            sl = slice(hd * HGRN_DK, (hd + 1) * HGRN_DK)
            st = st_ref[hd]
            attn = jnp.where(incl, _mm_nt(qs[:, sl], ks[:, sl]), 0.0)
            o = _mm(attn, v[:, sl]) + _mm_nt(qg[:, sl], st)
            st_ref[hd] = st * g_end[:, sl] + _mm_tn(v[:, sl], k_end[:, sl])
            o = o * lax.rsqrt(jnp.mean(o * o, axis=-1, keepdims=True) + EPS) * gnw * _silu(zz[:, sl])
            o_ref[0, rows, sl] = o.astype(o_ref.dtype)
        return carry

    lax.fori_loop(0, ts // c, step, 0, unroll=4)


def _hgrn_call(x, mod, nw, w, lbp, gnw, ts, layer):
    bsz, s, d = x.shape
    kw = HGRN_HEADS * HGRN_DK
    return pl.pallas_call(
        functools.partial(_hgrn_kernel, ts=ts, layer=layer),
        grid=(bsz, s // ts),
        in_specs=[
            pl.BlockSpec((1, ts, d), lambda b, j: (b, j, 0)),
            pl.BlockSpec((1, 6, d), lambda b, j: (b, 0, 0)),
            _const_spec(nw.shape), _const_spec(w.shape), _const_spec(lbp.shape),
            _const_spec(gnw.shape),
        ],
        out_specs=pl.BlockSpec((1, ts, kw), lambda b, j: (b, j, 0)),
        out_shape=jax.ShapeDtypeStruct((bsz, s, kw), BF16),
        scratch_shapes=[pltpu.VMEM((ts, kw), F32) for _ in range(5)]
        + [pltpu.VMEM((HGRN_HEADS, HGRN_DK, HGRN_DK), F32)],
        compiler_params=_cparams(("parallel", "arbitrary")),
    )(x, mod, nw, w, lbp, gnw)


def _ssd_kernel(x_ref, mod_ref, nw_ref, w_ref, wdt_ref, wdtt_ref, cw_ref, cb_ref, pcol_ref,
                prow_ref, dsk_ref, gnw_ref, o_ref, ext_ref, y_ref, st_ref, *, ts):
    inner = SSD_HEADS * SSD_HEAD_DIM
    xbc_w = inner + 2 * SSD_GROUPS * SSD_STATE
    hpg = SSD_HEADS // SSD_GROUPS
    p = SSD_HEAD_DIM

    @pl.when(pl.program_id(1) == 0)
    def _():
        st_ref[...] = jnp.zeros_like(st_ref)
        ext_ref[0:HALO, :] = jnp.zeros((HALO, xbc_w), F32)

    m = mod_ref[0]
    h = _mod_norm(x_ref[0], nw_ref[...], m[0:1], m[1:2]).astype(BF16)
    proj = jnp.dot(h, w_ref[...], preferred_element_type=F32)
    z = proj[:, :inner]
    ext_ref[HALO:HALO + ts, :] = proj[:, inner:]
    cw = cw_ref[...]
    acc = cw[SSD_CONV - 1:SSD_CONV] * proj[:, inner:] + cb_ref[...]
    for i in range(SSD_CONV - 1):
        off = HALO - (SSD_CONV - 1) + i
        acc = acc + cw[i:i + 1] * ext_ref[off:off + ts, :]
    ext_ref[0:HALO, :] = ext_ref[ts:ts + HALO, :]
    xbc = _silu(acc)
    xs = xbc[:, :inner]

    pcol = pcol_ref[...]
    prow = prow_ref[...]
    dt_cols = _softplus(jnp.dot(h, wdt_ref[...], preferred_element_type=F32) + pcol[1:2])
    dt_rows = _softplus(lax.dot_general(wdtt_ref[...], h, (((1,), (1,)), ((), ())),
                                        preferred_element_type=F32) + prow[:, 1:2])
    incl = _tri(ts)
    low = jnp.where(incl, 1.0, 0.0).astype(F32)
    upp = jnp.where(lax.broadcasted_iota(jnp.int32, (ts, ts), 0)
                    <= lax.broadcasted_iota(jnp.int32, (ts, ts), 1), 1.0, 0.0).astype(F32)
    acs_cols = _mm_hi(low, dt_cols * (-jnp.exp(pcol[0:1])))
    acs_rows = _mm_hi(dt_rows * (-jnp.exp(prow[:, 0:1])), upp)

    for g in range(SSD_GROUPS):
        bm = xbc[:, inner + g * SSD_STATE:inner + (g + 1) * SSD_STATE]
        cm = xbc[:, inner + (SSD_GROUPS + g) * SSD_STATE:inner + (SSD_GROUPS + g + 1) * SSD_STATE]
        cb = _mm_nt(cm, bm)
        for hh in range(hpg):
            hd = g * hpg + hh
            a_col = acs_cols[:, hd:hd + 1]
            a_row = acs_rows[hd:hd + 1, :]
            seg = jnp.where(incl, jnp.exp(jnp.where(incl, a_col - a_row, 0.0)), 0.0)
            x_h = xs[:, hd * p:(hd + 1) * p]
            xdt = x_h * dt_cols[:, hd:hd + 1]
            st = st_ref[hd]
            y = _mm(cb * seg, xdt) + _mm_nt(cm, st) * jnp.exp(a_col)
            last = a_col[ts - 1:ts, :]
            st_ref[hd] = st * jnp.exp(last) + _mm_tn(xdt * jnp.exp(last - a_col), bm)
            y_ref[:, hd * p:(hd + 1) * p] = y

    y = (y_ref[...] + dsk_ref[...] * xs) * _silu(z)
    gnw = gnw_ref[...]
    gw = hpg * p
    for g in range(SSD_GROUPS):
        yg = y[:, g * gw:(g + 1) * gw]
        yg = yg * lax.rsqrt(jnp.mean(yg * yg, axis=-1, keepdims=True) + EPS) * gnw[:, g * gw:(g + 1) * gw]
        o_ref[0, :, g * gw:(g + 1) * gw] = yg.astype(o_ref.dtype)


def _ssd_call(x, mod, nw, w, wdt, wdtt, cw, cb, pcol, prow, dsk, gnw, ts):
    bsz, s, d = x.shape
    inner = SSD_HEADS * SSD_HEAD_DIM
    xbc_w = inner + 2 * SSD_GROUPS * SSD_STATE
    return pl.pallas_call(
        functools.partial(_ssd_kernel, ts=ts),
        grid=(bsz, s // ts),
        in_specs=[
            pl.BlockSpec((1, ts, d), lambda b, j: (b, j, 0)),
            pl.BlockSpec((1, 6, d), lambda b, j: (b, 0, 0)),
            _const_spec(nw.shape), _const_spec(w.shape), _const_spec(wdt.shape),
            _const_spec(wdtt.shape), _const_spec(cw.shape), _const_spec(cb.shape),
            _const_spec(pcol.shape), _const_spec(prow.shape), _const_spec(dsk.shape),
            _const_spec(gnw.shape),
        ],
        out_specs=pl.BlockSpec((1, ts, inner), lambda b, j: (b, j, 0)),
        out_shape=jax.ShapeDtypeStruct((bsz, s, inner), BF16),
        scratch_shapes=[pltpu.VMEM((ts + HALO, xbc_w), F32),
                        pltpu.VMEM((ts, inner), F32),
                        pltpu.VMEM((SSD_HEADS, SSD_HEAD_DIM, SSD_STATE), F32)],
        compiler_params=_cparams(("parallel", "arbitrary")),
    )(x, mod, nw, w, wdt, wdtt, cw, cb, pcol, prow, dsk, gnw)


def _merge_kernel(x_ref, mod_ref, nw_ref, oa_ref, ob_ref, oc_ref, wg_ref, wa_ref, wb_ref, wc_ref,
                  wo_ref, o_ref):
    d = x_ref.shape[-1]
    x = x_ref[0]
    m = mod_ref[0]
    h = _mod_norm(x, nw_ref[...], m[0:1], m[1:2]).astype(BF16)
    gates = _sigmoid(jnp.dot(h, wg_ref[...], preferred_element_type=F32))
    merged = (gates[:, :d] * jnp.dot(oa_ref[0], wa_ref[...], preferred_element_type=F32)
              + gates[:, d:2 * d] * jnp.dot(ob_ref[0], wb_ref[...], preferred_element_type=F32)
              + gates[:, 2 * d:] * jnp.dot(oc_ref[0], wc_ref[...], preferred_element_type=F32))
    mix = jnp.dot(merged.astype(BF16), wo_ref[...], preferred_element_type=F32)
    o_ref[0] = x + m[2:3] * mix


def _merge_call(x, mod, nw, oa, ob, oc, wg, wa, wb, wc, wo, tm):
    bsz, s, d = x.shape
    tok = lambda w: pl.BlockSpec((1, tm, w), lambda b, j: (b, j, 0))
    return pl.pallas_call(
        _merge_kernel,
        grid=(bsz, s // tm),
        in_specs=[
            tok(d), pl.BlockSpec((1, 6, d), lambda b, j: (b, 0, 0)), _const_spec(nw.shape),
            tok(oa.shape[-1]), tok(ob.shape[-1]), tok(oc.shape[-1]),
            _const_spec(wg.shape), _const_spec(wa.shape), _const_spec(wb.shape),
            _const_spec(wc.shape), _const_spec(wo.shape),
        ],
        out_specs=tok(d),
        out_shape=jax.ShapeDtypeStruct((bsz, s, d), F32),
        compiler_params=_cparams(("parallel", "parallel")),
    )(x, mod, nw, oa, ob, oc, wg, wa, wb, wc, wo)


def _ffn_kernel(x_ref, mod_ref, nw_ref, wup_ref, cw_ref, cb_ref, wdn_ref, fnw_ref, o_ref,
                ext_ref, tail_ref, *, tm, n_split, final):
    hidden = wdn_ref.shape[0]
    fc = hidden // n_split

    @pl.when(pl.program_id(1) == 0)
    def _():
        tail_ref[...] = jnp.zeros_like(tail_ref)

    x = x_ref[0]
    m = mod_ref[0]
    h = _mod_norm(x, nw_ref[...], m[3:4], m[4:5]).astype(BF16)

    def conv_part(c0):
        u = jnp.dot(h, wup_ref[:, c0:c0 + fc], preferred_element_type=F32)
        ext_ref[0:HALO, :] = tail_ref[:, c0:c0 + fc]
        ext_ref[HALO:HALO + tm, :] = u
        cw = cw_ref[:, c0:c0 + fc]
        acc = cw[FFN_CONV - 1:FFN_CONV] * u + cb_ref[:, c0:c0 + fc]
        for i in range(FFN_CONV - 1):
            off = HALO - (FFN_CONV - 1) + i
            acc = acc + cw[i:i + 1] * ext_ref[off:off + tm, :]
        tail_ref[:, c0:c0 + fc] = ext_ref[tm:tm + HALO, :]
        return acc

    y = jnp.zeros_like(x)
    for i in range(n_split):
        gate = conv_part(i * fc)
        val = conv_part(hidden + i * fc)
        act = (_silu(gate) * val).astype(BF16)
        y = y + jnp.dot(act, wdn_ref[i * fc:(i + 1) * fc, :], preferred_element_type=F32)
    out = x + m[5:6] * y
    if final:
        out = out * lax.rsqrt(jnp.mean(out * out, axis=-1, keepdims=True) + EPS) * fnw_ref[...]
    o_ref[0] = out


def _ffn_call(x, mod, nw, wup, cw, cb, wdn, fnw, tm, final):
    bsz, s, d = x.shape
    hidden = wdn.shape[0]
    n_split = 2
    tok = pl.BlockSpec((1, tm, d), lambda b, j: (b, j, 0))
    single = lambda a: pl.BlockSpec(a.shape, lambda *_: (0,) * a.ndim, pipeline_mode=pl.Buffered(1))
    return pl.pallas_call(
        functools.partial(_ffn_kernel, tm=tm, n_split=n_split, final=final),
        grid=(bsz, s // tm),
        in_specs=[
            tok, pl.BlockSpec((1, 6, d), lambda b, j: (b, 0, 0)), _const_spec(nw.shape),
            single(wup), _const_spec(cw.shape), _const_spec(cb.shape), single(wdn),
            _const_spec(fnw.shape),
        ],
        out_specs=tok,
        out_shape=jax.ShapeDtypeStruct((bsz, s, d), F32),
        scratch_shapes=[pltpu.VMEM((tm + HALO, hidden // n_split), F32),
                        pltpu.VMEM((HALO, 2 * hidden), F32)],
        compiler_params=_cparams(("parallel", "arbitrary")),
    )(x, mod, nw, wup, cw, cb, wdn, fnw)


def _pad_lanes(a, width=128):
    return jnp.pad(a, ((0, 0), (0, width - a.shape[-1])))


def kernel(x, c, w_ada, b_ada, norm1_w, w_in, gdn_conv_w, gdn_a_log, gdn_dt_bias, gdn_norm_w,
           hgrn_lb_param, hgrn_norm_w, ssd_conv_w, ssd_conv_b, ssd_a_log, ssd_dt_bias, ssd_d,
           ssd_norm_w, w_br_a, w_br_b, w_br_c, w_out, norm2_w, ffn_w_up, ffn_conv_w, ffn_conv_b,
           ffn_w_down, final_norm_w):
    bsz, s, d = x.shape
    depth = w_in.shape[0]
    gk = GDN_HEADS * GDN_DK
    gv = GDN_HEADS * GDN_DV
    hk = HGRN_HEADS * HGRN_DK
    inner = SSD_HEADS * SSD_HEAD_DIM
    xbc_w = inner + 2 * SSD_GROUPS * SSD_STATE

    sizes = (2 * gk + gv, GDN_HEADS, GDN_HEADS, gv, hk, hk, hk, hk, inner, xbc_w, SSD_HEADS, 3 * d)
    offs = [0]
    for sz in sizes:
        offs.append(offs[-1] + sz)
    (o_qkv, o_a, o_b, o_z, o_hq, _, _, _, o_sz, o_xbc, o_dt, o_gate, o_end) = offs

    ts_gdn = min(256, s)
    ts_hgrn = min(512, s)
    ts_ssd = min(256, s)
    tm_merge = min(512, s)
    tm_ffn = min(256, s)

    mod = _ada_call(c, w_ada, b_ada).reshape(depth, bsz, 6, d)
    fnw = final_norm_w.reshape(1, d)

    for l in range(depth):
        wl = w_in[l]
        mod_l = mod[l]
        nw1 = norm1_w[l].reshape(1, d)

        w_gdn = jnp.concatenate([wl[:, o_qkv:o_a], wl[:, o_z:o_hq]], axis=1).astype(BF16)
        w_ab = wl[:, o_a:o_z]
        o_ga = _gdn_call(
            x, mod_l, nw1, w_gdn, _pad_lanes(w_ab).astype(BF16), w_ab.T.astype(BF16),
            gdn_conv_w[l],
            _pad_lanes(jnp.stack([gdn_a_log[l], gdn_dt_bias[l]])),
            jnp.pad(jnp.stack([gdn_a_log[l], gdn_dt_bias[l]], axis=1), ((0, GDN_HEADS), (0, 0))),
            gdn_norm_w[l].reshape(1, GDN_DV), ts_gdn)

        o_hg = _hgrn_call(x, mod_l, nw1, wl[:, o_hq:o_sz].astype(BF16), hgrn_lb_param,
                          hgrn_norm_w[l].reshape(1, HGRN_DK), ts_hgrn, l)

        w_dt = wl[:, o_dt:o_gate]
        o_sd = _ssd_call(
            x, mod_l, nw1, wl[:, o_sz:o_dt].astype(BF16), _pad_lanes(w_dt).astype(BF16),
            w_dt.T.astype(BF16), ssd_conv_w[l], ssd_conv_b[l].reshape(1, xbc_w),
            _pad_lanes(jnp.stack([ssd_a_log[l], ssd_dt_bias[l]])),
            jnp.stack([ssd_a_log[l], ssd_dt_bias[l]], axis=1),
            jnp.repeat(ssd_d[l], SSD_HEAD_DIM).reshape(1, inner),
            ssd_norm_w[l].reshape(1, inner), ts_ssd)

        x = _merge_call(x, mod_l, nw1, o_ga, o_hg, o_sd, wl[:, o_gate:o_end].astype(BF16),
                        w_br_a[l].astype(BF16), w_br_b[l].astype(BF16), w_br_c[l].astype(BF16),
                        w_out[l].astype(BF16), tm_merge)

        x = _ffn_call(x, mod_l, norm2_w[l].reshape(1, d), ffn_w_up[l].astype(BF16), ffn_conv_w[l],
                      ffn_conv_b[l].reshape(1, -1), ffn_w_down[l].astype(BF16), fnw, tm_ffn,
                      final=(l == depth - 1))
    return x
```

```python
import functools

import jax
import jax.numpy as jnp
from jax import lax
from jax.experimental import pallas as pl
from jax.experimental.pallas import tpu as pltpu

F32 = jnp.float32
BF16 = jnp.bfloat16
EPS = 1e-6

GDN_HEADS = 4
GDN_DK = 128
GDN_DV = 128
GDN_CONV = 4
GDN_CHUNK = 64
HGRN_HEADS = 4
HGRN_DK = 128
HGRN_CHUNK = 16
SSD_HEADS = 8
SSD_HEAD_DIM = 64
SSD_GROUPS = 2
SSD_STATE = 128
SSD_CONV = 4
FFN_CONV = 3

HALO = 8
VMEM_LIMIT = 56 * 1024 * 1024

_HI = lax.Precision.HIGHEST


def _mm(a, b):
    return jnp.dot(a.astype(BF16), b.astype(BF16), preferred_element_type=F32)


def _mm_nt(a, b):
    return lax.dot_general(a.astype(BF16), b.astype(BF16), (((1,), (1,)), ((), ())),
                           preferred_element_type=F32)


def _mm_tn(a, b):
    return lax.dot_general(a.astype(BF16), b.astype(BF16), (((0,), (0,)), ((), ())),
                           preferred_element_type=F32)


def _mm_hi(a, b):
    return jnp.dot(a, b, precision=_HI, preferred_element_type=F32)


def _sigmoid(x):
    return jax.nn.sigmoid(x)


def _silu(x):
    return x * jax.nn.sigmoid(x)


def _softplus(x):
    return jnp.maximum(x, 0.0) + jnp.log(1.0 + jnp.exp(-jnp.abs(x)))


def _mod_norm(x, nw, shift, scale):
    y = x * lax.rsqrt(jnp.mean(x * x, axis=-1, keepdims=True) + EPS) * nw
    return y * (1.0 + scale) + shift


def _tri(n, strict=False):
    r = lax.broadcasted_iota(jnp.int32, (n, n), 0)
    c = lax.broadcasted_iota(jnp.int32, (n, n), 1)
    return (r > c) if strict else (r >= c)


def _cparams(sem):
    return pltpu.CompilerParams(dimension_semantics=sem, vmem_limit_bytes=VMEM_LIMIT)


def _const_spec(shape):
    nd = len(shape)
    return pl.BlockSpec(shape, lambda *_: (0,) * nd)


def _ada_kernel(c_ref, w_ref, b_ref, o_ref):
    c = c_ref[...]
    o_ref[0] = _mm_hi(_silu(c), w_ref[0]) + b_ref[0]


def _ada_call(c, w_ada, b_ada):
    depth, d, n = w_ada.shape
    bsz = c.shape[0]
    tn = 1536
    return pl.pallas_call(
        _ada_kernel,
        grid=(depth, n // tn),
        in_specs=[
            pl.BlockSpec((bsz, d), lambda l, j: (0, 0)),
            pl.BlockSpec((1, d, tn), lambda l, j: (l, 0, j)),
            pl.BlockSpec((1, 1, tn), lambda l, j: (l, 0, j)),
        ],
        out_specs=pl.BlockSpec((1, bsz, tn), lambda l, j: (l, 0, j)),
        out_shape=jax.ShapeDtypeStruct((depth, bsz, n), F32),
        compiler_params=_cparams(("arbitrary", "arbitrary")),
    )(c, w_ada, b_ada.reshape(depth, 1, n))


def _split3(a):
    hi = a.astype(BF16)
    lo = (a - hi.astype(F32)).astype(BF16)
    return hi, lo


def _dot3(a_hi, a_lo, b_hi, b_lo):
    f = lambda p, q: jnp.dot(p, q, preferred_element_type=F32)
    return f(a_hi, b_hi) + (f(a_lo, b_hi) + f(a_hi, b_lo))


def _unit_lower_inverse(ms, eye, bd_mask):
    c = eye.shape[0]
    nb = eye.shape[1] // c
    n_steps = c.bit_length() - 1
    f = lambda p, q: jnp.dot(p, q, preferred_element_type=F32)

    def blockdiag(x):
        return jnp.concatenate([x] * nb, axis=0) * bd_mask

    def dot3(a_hi, a_lo, b_hi, b_lo):
        r = a_hi.shape[0]
        t = f(jnp.concatenate([a_hi, a_lo], axis=0), blockdiag(b_hi))
        return t[:r] + (t[r:] + f(a_hi, blockdiag(b_lo)))

    qs, ps = [], []
    for m in ms:
        hi, lo = _split3(m)
        qs.append(dot3(hi, lo, hi, lo))
        ps.append(eye - m)
    for step in range(1, n_steps):
        last = step == n_steps - 1
        for i in range(len(ms)):
            if last:
                p_hi, p_lo = _split3(ps[i])
                q_hi, q_lo = _split3(qs[i])
                ps[i] = ps[i] + dot3(p_hi, p_lo, q_hi, q_lo)
            else:
                hi, lo = _split3(jnp.concatenate([qs[i], ps[i]], axis=0))
                r = dot3(hi, lo, hi[:c], lo[:c])
                qs[i] = r[:c]
                ps[i] = ps[i] + r[c:]
    return ps


def _gdn_kernel(x_ref, mod_ref, nw_ref, w_ref, wsm_ref, wsmt_ref, cw_ref, pcol_ref, prow_ref,
                gnw_ref, o_ref, ext_ref, st_ref, *, nb, rows):
    kw = GDN_HEADS * GDN_DK
    qkv_w = 2 * kw + GDN_HEADS * GDN_DV
    c = GDN_CHUNK
    ts = nb * rows
    cps = rows // c

    @pl.when(pl.program_id(1) == 0)
    def _():
        st_ref[...] = jnp.zeros_like(st_ref)
        ext_ref[:, 0:HALO, :] = jnp.zeros((nb, HALO, qkv_w), F32)

    m = mod_ref[...]
    h = _mod_norm(x_ref[...], nw_ref[...], m[:, 0:1, :], m[:, 1:2, :])
    h = h.reshape(ts, h.shape[-1]).astype(BF16)
    proj = jnp.dot(h, w_ref[...], preferred_element_type=F32)
    z = proj[:, qkv_w:]
    cw = cw_ref[...]
    accs = []
    for i in range(nb):
        p_i = proj[i * rows:(i + 1) * rows, :qkv_w]
        ext_ref[i, HALO:HALO + rows, :] = p_i
        acc = cw[GDN_CONV - 1:GDN_CONV] * p_i
        for t in range(GDN_CONV - 1):
            off = HALO - (GDN_CONV - 1) + t
            acc = acc + cw[t:t + 1] * ext_ref[i, off:off + rows, :]
        ext_ref[i, 0:HALO, :] = ext_ref[i, rows:rows + HALO, :]
        accs.append(acc)
    qkv = _silu(jnp.concatenate(accs, axis=0))

    sm = jnp.dot(h, wsm_ref[...], preferred_element_type=F32)
    smt = lax.dot_general(wsmt_ref[...], h, (((1,), (1,)), ((), ())), preferred_element_type=F32)
    pcol = pcol_ref[...]
    prow = prow_ref[...]
    g_cols = -jnp.exp(pcol[0:1]) * _softplus(sm + pcol[1:2])
    beta_cols = _sigmoid(sm)
    g_rows = -jnp.exp(prow[:, 0:1]) * _softplus(smt + prow[:, 1:2])

    rr = lax.broadcasted_iota(jnp.int32, (ts, ts), 0)
    cc = lax.broadcasted_iota(jnp.int32, (ts, ts), 1)
    same = (rr // c) == (cc // c)
    gc_all = _mm_hi(jnp.where(same & (rr >= cc), 1.0, 0.0).astype(F32), g_cols)
    gr_all = _mm_hi(g_rows, jnp.where(same & (rr <= cc), 1.0, 0.0).astype(F32))
    gnw = gnw_ref[...]

    heads = range(GDN_HEADS)
    hc = GDN_HEADS * c
    row_i = lax.broadcasted_iota(jnp.int32, (c, hc), 0)
    lane_i = lax.broadcasted_iota(jnp.int32, (c, hc), 1)
    lane_head = lane_i // c
    col_i = lane_i - lane_head * c
    incl_bd = row_i >= col_i
    strict_bd = row_i > col_i
    eye_bd = jnp.where(row_i == col_i, 1.0, 0.0).astype(F32)
    bd_mask = jnp.where(lax.broadcasted_iota(jnp.int32, (hc, hc), 0) // c
                        == lax.broadcasted_iota(jnp.int32, (hc, hc), 1) // c, 1.0, 0.0).astype(BF16)
    qn, kn, kbs, egs, ms, qks = {}, {}, {}, {}, [], []
    for ci in range(ts // c):
        r0 = ci * c
        g_col_bd = jnp.zeros((c, hc), F32)
        for hd in heads:
            q = qkv[r0:r0 + c, hd * GDN_DK:(hd + 1) * GDN_DK]
            k = qkv[r0:r0 + c, kw + hd * GDN_DK:kw + (hd + 1) * GDN_DK]
            q = q * lax.rsqrt(jnp.sum(q * q, axis=-1, keepdims=True) + EPS) * (GDN_DK ** -0.5)
            k = k * lax.rsqrt(jnp.sum(k * k, axis=-1, keepdims=True) + EPS)
            g_col = gc_all[r0:r0 + c, hd:hd + 1]
            qn[ci, hd], kn[ci, hd] = q, k
            kbs[ci, hd] = k * beta_cols[r0:r0 + c, GDN_HEADS + hd:GDN_HEADS + hd + 1]
            egs[ci, hd] = jnp.exp(g_col)
            g_col_bd = jnp.where(lane_head == hd, g_col, g_col_bd)
        g_row_bd = jnp.concatenate([gr_all[hd:hd + 1, r0:r0 + c] for hd in heads], axis=1)
        decay = jnp.where(incl_bd, jnp.exp(jnp.where(incl_bd, g_col_bd - g_row_bd, 0.0)), 0.0)
        kk = jnp.concatenate([_mm_nt(kbs[ci, hd], kn[ci, hd]) for hd in heads], axis=1)
        qk = jnp.concatenate([_mm_nt(qn[ci, hd], kn[ci, hd]) for hd in heads], axis=1)
        ms.append(jnp.where(strict_bd, kk * decay, 0.0))
        qks.append(qk * decay)
    invs = _unit_lower_inverse(ms, eye_bd, bd_mask)
    uw, attn = {}, {}
    for ci in range(ts // c):
        r0 = ci * c
        for hd in heads:
            v = qkv[r0:r0 + c, 2 * kw + hd * GDN_DV:2 * kw + (hd + 1) * GDN_DV]
            beta = beta_cols[r0:r0 + c, GDN_HEADS + hd:GDN_HEADS + hd + 1]
            rhs_hi, rhs_lo = _split3(jnp.concatenate([v * beta, kbs[ci, hd] * egs[ci, hd]], axis=1))
            inv_hi, inv_lo = _split3(invs[ci][:, hd * c:(hd + 1) * c])
            uw[ci, hd] = _dot3(inv_hi, inv_lo, rhs_hi, rhs_lo)
            attn[ci, hd] = qks[ci][:, hd * c:(hd + 1) * c]

    items = [(i, hd) for i in range(nb) for hd in heads]
    for j in range(cps):
        ci = {i: i * cps + j for i in range(nb)}
        st = {(i, hd): st_ref[i, hd] for i, hd in items}
        ws = {(i, hd): _mm(jnp.concatenate([uw[ci[i], hd][:, GDN_DV:],
                                            qn[ci[i], hd] * egs[ci[i], hd]], axis=0), st[i, hd])
              for i, hd in items}
        v_new = {(i, hd): uw[ci[i], hd][:, :GDN_DV] - ws[i, hd][:c] for i, hd in items}
        for i, hd in items:
            r0 = ci[i] * c
            g_col = gc_all[r0:r0 + c, hd:hd + 1]
            g_end = g_col[c - 1:c, :]
            k_end = kn[ci[i], hd] * jnp.exp(g_end - g_col)
            st_ref[i, hd] = st[i, hd] * jnp.exp(g_end) + _mm_tn(k_end, v_new[i, hd])
        for i, hd in items:
            r0 = ci[i] * c
            o = ws[i, hd][c:] + _mm(attn[ci[i], hd], v_new[i, hd])
            zz = z[r0:r0 + c, hd * GDN_DV:(hd + 1) * GDN_DV]
            o = o * lax.rsqrt(jnp.mean(o * o, axis=-1, keepdims=True) + EPS) * gnw * _silu(zz)
            o_ref[i, j * c:(j + 1) * c, hd * GDN_DV:(hd + 1) * GDN_DV] = o.astype(o_ref.dtype)


def _gdn_call(x, mod, nw, w, wsm, wsmt, cw, pcol, prow, gnw, nb, rows):
    bsz, s, d = x.shape
    kw = GDN_HEADS * GDN_DK
    vw = GDN_HEADS * GDN_DV
    qkv_w = 2 * kw + vw
    return pl.pallas_call(
        functools.partial(_gdn_kernel, nb=nb, rows=rows),
        grid=(bsz // nb, s // rows),
        in_specs=[
            pl.BlockSpec((nb, rows, d), lambda b, j: (b, j, 0)),
            pl.BlockSpec((nb, 6, d), lambda b, j: (b, 0, 0)),
            _const_spec(nw.shape), _const_spec(w.shape), _const_spec(wsm.shape),
            _const_spec(wsmt.shape), _const_spec(cw.shape), _const_spec(pcol.shape),
            _const_spec(prow.shape), _const_spec(gnw.shape),
        ],
        out_specs=pl.BlockSpec((nb, rows, vw), lambda b, j: (b, j, 0)),
        out_shape=jax.ShapeDtypeStruct((bsz, s, vw), BF16),
        scratch_shapes=[pltpu.VMEM((nb, rows + HALO, qkv_w), F32),
                        pltpu.VMEM((nb, GDN_HEADS, GDN_DK, GDN_DV), F32)],
        compiler_params=_cparams(("parallel", "arbitrary")),
    )(x, mod, nw, w, wsm, wsmt, cw, pcol, prow, gnw)


def _group_cumsum_rows(x, group):
    rows = lax.broadcasted_iota(jnp.int32, x.shape, 0) % group
    d = 1
    while d < group:
        x = x + jnp.where(rows >= d, pltpu.roll(x, d, axis=0), 0.0)
        d *= 2
    return x


def _hgrn_kernel(x_ref, mod_ref, nw_ref, w_ref, lbp_ref, gnw_ref, o_ref,
                 qs_ref, ks_ref, qg_ref, ke_ref, v_ref, ge_ref, z_ref, st_ref, *, ts, layer, unroll):
    kw = HGRN_HEADS * HGRN_DK
    c = HGRN_CHUNK

    @pl.when(pl.program_id(1) == 0)
    def _():
        st_ref[...] = jnp.zeros_like(st_ref)

    m = mod_ref[0]
    h = _mod_norm(x_ref[0], nw_ref[...], m[0:1], m[1:2]).astype(BF16)
    proj = jnp.dot(h, w_ref[...], preferred_element_type=F32)

    lbp = lbp_ref[...]
    e = jnp.exp(lbp - jnp.max(lbp, axis=0, keepdims=True))
    soft = e / jnp.sum(e, axis=0, keepdims=True)
    lb = jnp.zeros((1, kw), F32)
    for i in range(1, layer + 1):
        lb = lb + soft[i:i + 1]

    f_raw = proj[:, kw:2 * kw]
    n_sub = ts // c
    g = _group_cumsum_rows(jnp.log(lb + (1.0 - lb) * _sigmoid(f_raw)), c).reshape(n_sub, c, kw)
    q = _silu(proj[:, :kw]).reshape(n_sub, c, kw)
    k = ((1.0 - lb) * _sigmoid(-f_raw)).reshape(n_sub, c, kw)
    g_mid = g[:, c // 2 - 1:c // 2, :]
    g_last = g[:, c - 1:c, :]
    qs_ref[...] = (q * jnp.exp(g - g_mid)).reshape(ts, kw).astype(BF16)
    ks_ref[...] = (k * jnp.exp(g_mid - g)).reshape(ts, kw).astype(BF16)
    qg_ref[...] = (q * jnp.exp(g)).reshape(ts, kw).astype(BF16)
    ke_ref[...] = (k * jnp.exp(g_last - g)).reshape(ts, kw).astype(BF16)
    ge_ref[...] = jnp.broadcast_to(jnp.exp(g_last), (n_sub, c, kw)).reshape(ts, kw)
    v_ref[...] = proj[:, 2 * kw:3 * kw].astype(BF16)
    z_ref[...] = _silu(proj[:, 3 * kw:])

    incl = _tri(c)
    gnw = gnw_ref[...]
    heads = range(HGRN_HEADS)
    sl = [slice(hd * HGRN_DK, (hd + 1) * HGRN_DK) for hd in heads]

    def step(i, carry):
        rows = [pl.ds(pl.multiple_of((i * unroll + u) * c, c), c) for u in range(unroll)]
        pairs = [(u, hd) for u in range(unroll) for hd in heads]
        vv = [v_ref[r, :] for r in rows]
        sc = {(u, hd): _mm_nt(qs_ref[rows[u], sl[hd]], ks_ref[rows[u], sl[hd]]) for u, hd in pairs}
        kv = {(u, hd): _mm_tn(vv[u][:, sl[hd]], ke_ref[rows[u], sl[hd]]) for u, hd in pairs}
        oi = {(u, hd): _mm(jnp.where(incl, sc[u, hd], 0.0), vv[u][:, sl[hd]]) for u, hd in pairs}
        st = [st_ref[hd] for hd in heads]
        oo = {}
        for u in range(unroll):
            g_end = ge_ref[rows[u], :][0:1, :]
            for hd in heads:
                oo[u, hd] = oi[u, hd] + _mm_nt(qg_ref[rows[u], sl[hd]], st[hd])
                st[hd] = st[hd] * g_end[:, sl[hd]] + kv[u, hd]
        for hd in heads:
            st_ref[hd] = st[hd]
        for u, hd in pairs:
            o = oo[u, hd]
            o = o * lax.rsqrt(jnp.mean(o * o, axis=-1, keepdims=True) + EPS) * gnw * z_ref[rows[u], sl[hd]]
            o_ref[0, rows[u], sl[hd]] = o.astype(o_ref.dtype)
        return carry

    lax.fori_loop(0, n_sub // unroll, step, 0)


def _hgrn_call(x, mod, nw, w, lbp, gnw, ts, layer):
    bsz, s, d = x.shape
    kw = HGRN_HEADS * HGRN_DK
    return pl.pallas_call(
        functools.partial(_hgrn_kernel, ts=ts, layer=layer, unroll=8),
        grid=(bsz, s // ts),
        in_specs=[
            pl.BlockSpec((1, ts, d), lambda b, j: (b, j, 0)),
            pl.BlockSpec((1, 6, d), lambda b, j: (b, 0, 0)),
            _const_spec(nw.shape), _const_spec(w.shape), _const_spec(lbp.shape),
            _const_spec(gnw.shape),
        ],
        out_specs=pl.BlockSpec((1, ts, kw), lambda b, j: (b, j, 0)),
        out_shape=jax.ShapeDtypeStruct((bsz, s, kw), BF16),
        scratch_shapes=[pltpu.VMEM((ts, kw), BF16) for _ in range(5)]
        + [pltpu.VMEM((ts, kw), F32) for _ in range(2)]
        + [pltpu.VMEM((HGRN_HEADS, HGRN_DK, HGRN_DK), F32)],
        compiler_params=_cparams(("parallel", "arbitrary")),
    )(x, mod, nw, w, lbp, gnw)


def _ssd_kernel(x_ref, mod_ref, nw_ref, w_ref, wdt_ref, wdtt_ref, cw_ref, cb_ref, pcol_ref,
                prow_ref, dsk_ref, gnw_ref, o_ref, ext_ref, y_ref, st_ref, *, ts):
    inner = SSD_HEADS * SSD_HEAD_DIM
    xbc_w = inner + 2 * SSD_GROUPS * SSD_STATE
    hpg = SSD_HEADS // SSD_GROUPS
    p = SSD_HEAD_DIM

    @pl.when(pl.program_id(1) == 0)
    def _():
        st_ref[...] = jnp.zeros_like(st_ref)
        ext_ref[0:HALO, :] = jnp.zeros((HALO, xbc_w), F32)

    m = mod_ref[0]
    h = _mod_norm(x_ref[0], nw_ref[...], m[0:1], m[1:2]).astype(BF16)
    proj = jnp.dot(h, w_ref[...], preferred_element_type=F32)
    z = proj[:, :inner]
    ext_ref[HALO:HALO + ts, :] = proj[:, inner:]
    cw = cw_ref[...]
    acc = cw[SSD_CONV - 1:SSD_CONV] * proj[:, inner:] + cb_ref[...]
    for i in range(SSD_CONV - 1):
        off = HALO - (SSD_CONV - 1) + i
        acc = acc + cw[i:i + 1] * ext_ref[off:off + ts, :]
    ext_ref[0:HALO, :] = ext_ref[ts:ts + HALO, :]
    xbc = _silu(acc)
    xs = xbc[:, :inner]

    pcol = pcol_ref[...]
    prow = prow_ref[...]
    dt_cols = _softplus(jnp.dot(h, wdt_ref[...], preferred_element_type=F32) + pcol[1:2])
    dt_rows = _softplus(lax.dot_general(wdtt_ref[...], h, (((1,), (1,)), ((), ())),
                                        preferred_element_type=F32) + prow[:, 1:2])
    incl = _tri(ts)
    low = jnp.where(incl, 1.0, 0.0).astype(F32)
    upp = jnp.where(lax.broadcasted_iota(jnp.int32, (ts, ts), 0)
                    <= lax.broadcasted_iota(jnp.int32, (ts, ts), 1), 1.0, 0.0).astype(F32)
    acs_cols = _mm_hi(low, dt_cols * (-jnp.exp(pcol[0:1])))
    acs_rows = _mm_hi(dt_rows * (-jnp.exp(prow[:, 0:1])), upp)

    for g in range(SSD_GROUPS):
        bm = xbc[:, inner + g * SSD_STATE:inner + (g + 1) * SSD_STATE]
        cm = xbc[:, inner + (SSD_GROUPS + g) * SSD_STATE:inner + (SSD_GROUPS + g + 1) * SSD_STATE]
        cb = _mm_nt(cm, bm)
        for hh in range(hpg):
            hd = g * hpg + hh
            a_col = acs_cols[:, hd:hd + 1]
            a_row = acs_rows[hd:hd + 1, :]
            seg = jnp.where(incl, jnp.exp(jnp.where(incl, a_col - a_row, 0.0)), 0.0)
            x_h = xs[:, hd * p:(hd + 1) * p]
            xdt = x_h * dt_cols[:, hd:hd + 1]
            st = st_ref[hd]
            y = _mm(cb * seg, xdt) + _mm_nt(cm, st) * jnp.exp(a_col)
            last = a_col[ts - 1:ts, :]
            st_ref[hd] = st * jnp.exp(last) + _mm_tn(xdt * jnp.exp(last - a_col), bm)
            y_ref[:, hd * p:(hd + 1) * p] = y

    y = (y_ref[...] + dsk_ref[...] * xs) * _silu(z)
    gnw = gnw_ref[...]
    gw = hpg * p
    for g in range(SSD_GROUPS):
        yg = y[:, g * gw:(g + 1) * gw]
        yg = yg * lax.rsqrt(jnp.mean(yg * yg, axis=-1, keepdims=True) + EPS) * gnw[:, g * gw:(g + 1) * gw]
        o_ref[0, :, g * gw:(g + 1) * gw] = yg.astype(o_ref.dtype)


def _ssd_call(x, mod, nw, w, wdt, wdtt, cw, cb, pcol, prow, dsk, gnw, ts):
    bsz, s, d = x.shape
    inner = SSD_HEADS * SSD_HEAD_DIM
    xbc_w = inner + 2 * SSD_GROUPS * SSD_STATE
    return pl.pallas_call(
        functools.partial(_ssd_kernel, ts=ts),
        grid=(bsz, s // ts),
        in_specs=[
            pl.BlockSpec((1, ts, d), lambda b, j: (b, j, 0)),
            pl.BlockSpec((1, 6, d), lambda b, j: (b, 0, 0)),
            _const_spec(nw.shape), _const_spec(w.shape), _const_spec(wdt.shape),
            _const_spec(wdtt.shape), _const_spec(cw.shape), _const_spec(cb.shape),
            _const_spec(pcol.shape), _const_spec(prow.shape), _const_spec(dsk.shape),
            _const_spec(gnw.shape),
        ],
        out_specs=pl.BlockSpec((1, ts, inner), lambda b, j: (b, j, 0)),
        out_shape=jax.ShapeDtypeStruct((bsz, s, inner), BF16),
        scratch_shapes=[pltpu.VMEM((ts + HALO, xbc_w), F32),
                        pltpu.VMEM((ts, inner), F32),
                        pltpu.VMEM((SSD_HEADS, SSD_HEAD_DIM, SSD_STATE), F32)],
        compiler_params=_cparams(("parallel", "arbitrary")),
    )(x, mod, nw, w, wdt, wdtt, cw, cb, pcol, prow, dsk, gnw)


def _merge_kernel(x_ref, mod_ref, nw_ref, oa_ref, ob_ref, oc_ref, wg_ref, wa_ref, wb_ref, wc_ref,
                  wo_ref, o_ref):
    d = x_ref.shape[-1]
    x = x_ref[0]
    m = mod_ref[0]
    h = _mod_norm(x, nw_ref[...], m[0:1], m[1:2]).astype(BF16)
    gates = _sigmoid(jnp.dot(h, wg_ref[...], preferred_element_type=F32))
    merged = (gates[:, :d] * jnp.dot(oa_ref[0], wa_ref[...], preferred_element_type=F32)
              + gates[:, d:2 * d] * jnp.dot(ob_ref[0], wb_ref[...], preferred_element_type=F32)
              + gates[:, 2 * d:] * jnp.dot(oc_ref[0], wc_ref[...], preferred_element_type=F32))
    mix = jnp.dot(merged.astype(BF16), wo_ref[...], preferred_element_type=F32)
    o_ref[0] = x + m[2:3] * mix


def _merge_call(x, mod, nw, oa, ob, oc, wg, wa, wb, wc, wo, tm):
    bsz, s, d = x.shape
    tok = lambda w: pl.BlockSpec((1, tm, w), lambda b, j: (b, j, 0))
    return pl.pallas_call(
        _merge_kernel,
        grid=(bsz, s // tm),
        in_specs=[
            tok(d), pl.BlockSpec((1, 6, d), lambda b, j: (b, 0, 0)), _const_spec(nw.shape),
            tok(oa.shape[-1]), tok(ob.shape[-1]), tok(oc.shape[-1]),
            _const_spec(wg.shape), _const_spec(wa.shape), _const_spec(wb.shape),
            _const_spec(wc.shape), _const_spec(wo.shape),
        ],
        out_specs=tok(d),
        out_shape=jax.ShapeDtypeStruct((bsz, s, d), F32),
        compiler_params=_cparams(("parallel", "parallel")),
    )(x, mod, nw, oa, ob, oc, wg, wa, wb, wc, wo)


def _ffn_kernel(x_ref, mod_ref, nw_ref, wup_ref, cw_ref, cb_ref, wdn_ref, fnw_ref, o_ref,
                ext_ref, tail_ref, *, tm, n_split, final):
    hidden = wdn_ref.shape[0]
    fc = hidden // n_split

    @pl.when(pl.program_id(1) == 0)
    def _():
        tail_ref[...] = jnp.zeros_like(tail_ref)

    x = x_ref[0]
    m = mod_ref[0]
    h = _mod_norm(x, nw_ref[...], m[3:4], m[4:5]).astype(BF16)

    def conv_part(c0):
        u = jnp.dot(h, wup_ref[:, c0:c0 + fc], preferred_element_type=F32)
        ext_ref[0:HALO, :] = tail_ref[:, c0:c0 + fc]
        ext_ref[HALO:HALO + tm, :] = u
        cw = cw_ref[:, c0:c0 + fc]
        acc = cw[FFN_CONV - 1:FFN_CONV] * u + cb_ref[:, c0:c0 + fc]
        for i in range(FFN_CONV - 1):
            off = HALO - (FFN_CONV - 1) + i
            acc = acc + cw[i:i + 1] * ext_ref[off:off + tm, :]
        tail_ref[:, c0:c0 + fc] = ext_ref[tm:tm + HALO, :]
        return acc

    y = jnp.zeros_like(x)
    for i in range(n_split):
        gate = conv_part(i * fc)
        val = conv_part(hidden + i * fc)
        act = (_silu(gate) * val).astype(BF16)
        y = y + jnp.dot(act, wdn_ref[i * fc:(i + 1) * fc, :], preferred_element_type=F32)
    out = x + m[5:6] * y
    if final:
        out = out * lax.rsqrt(jnp.mean(out * out, axis=-1, keepdims=True) + EPS) * fnw_ref[...]
    o_ref[0] = out


def _ffn_call(x, mod, nw, wup, cw, cb, wdn, fnw, tm, final):
    bsz, s, d = x.shape
    hidden = wdn.shape[0]
    n_split = 2
    tok = pl.BlockSpec((1, tm, d), lambda b, j: (b, j, 0))
    single = lambda a: pl.BlockSpec(a.shape, lambda *_: (0,) * a.ndim, pipeline_mode=pl.Buffered(1))
    return pl.pallas_call(
        functools.partial(_ffn_kernel, tm=tm, n_split=n_split, final=final),
        grid=(bsz, s // tm),
        in_specs=[
            tok, pl.BlockSpec((1, 6, d), lambda b, j: (b, 0, 0)), _const_spec(nw.shape),
            single(wup), _const_spec(cw.shape), _const_spec(cb.shape), single(wdn),
            _const_spec(fnw.shape),
        ],
        out_specs=tok,
        out_shape=jax.ShapeDtypeStruct((bsz, s, d), F32),
        scratch_shapes=[pltpu.VMEM((tm + HALO, hidden // n_split), F32),
                        pltpu.VMEM((HALO, 2 * hidden), F32)],
        compiler_params=_cparams(("parallel", "arbitrary")),
    )(x, mod, nw, wup, cw, cb, wdn, fnw)


def _pad_lanes(a, width=128):
    return jnp.pad(a, ((0, 0), (0, width - a.shape[-1])))


def kernel(x, c, w_ada, b_ada, norm1_w, w_in, gdn_conv_w, gdn_a_log, gdn_dt_bias, gdn_norm_w,
           hgrn_lb_param, hgrn_norm_w, ssd_conv_w, ssd_conv_b, ssd_a_log, ssd_dt_bias, ssd_d,
           ssd_norm_w, w_br_a, w_br_b, w_br_c, w_out, norm2_w, ffn_w_up, ffn_conv_w, ffn_conv_b,
           ffn_w_down, final_norm_w):
    bsz, s, d = x.shape
    depth = w_in.shape[0]
    gk = GDN_HEADS * GDN_DK
    gv = GDN_HEADS * GDN_DV
    hk = HGRN_HEADS * HGRN_DK
    inner = SSD_HEADS * SSD_HEAD_DIM
    xbc_w = inner + 2 * SSD_GROUPS * SSD_STATE

    sizes = (2 * gk + gv, GDN_HEADS, GDN_HEADS, gv, hk, hk, hk, hk, inner, xbc_w, SSD_HEADS, 3 * d)
    offs = [0]
    for sz in sizes:
        offs.append(offs[-1] + sz)
    (o_qkv, o_a, o_b, o_z, o_hq, _, _, _, o_sz, o_xbc, o_dt, o_gate, o_end) = offs

    nb_gdn = 4 if bsz % 4 == 0 else (2 if bsz % 2 == 0 else 1)
    rows_gdn = min(max(GDN_CHUNK, 256 // nb_gdn), s)
    ts_hgrn = min(512, s)
    ts_ssd = min(256, s)
    tm_merge = min(512, s)
    tm_ffn = min(256, s)

    mod = _ada_call(c, w_ada, b_ada).reshape(depth, bsz, 6, d)
    fnw = final_norm_w.reshape(1, d)

    for l in range(depth):
        wl = w_in[l]
        mod_l = mod[l]
        nw1 = norm1_w[l].reshape(1, d)

        w_gdn = jnp.concatenate([wl[:, o_qkv:o_a], wl[:, o_z:o_hq]], axis=1).astype(BF16)
        w_ab = wl[:, o_a:o_z]
        o_ga = _gdn_call(
            x, mod_l, nw1, w_gdn, _pad_lanes(w_ab).astype(BF16), w_ab.T.astype(BF16),
            gdn_conv_w[l],
            _pad_lanes(jnp.stack([gdn_a_log[l], gdn_dt_bias[l]])),
            jnp.pad(jnp.stack([gdn_a_log[l], gdn_dt_bias[l]], axis=1), ((0, GDN_HEADS), (0, 0))),
            gdn_norm_w[l].reshape(1, GDN_DV), nb_gdn, rows_gdn)

        o_hg = _hgrn_call(x, mod_l, nw1, wl[:, o_hq:o_sz].astype(BF16), hgrn_lb_param,
                          hgrn_norm_w[l].reshape(1, HGRN_DK), ts_hgrn, l)

        w_dt = wl[:, o_dt:o_gate]
        o_sd = _ssd_call(
            x, mod_l, nw1, wl[:, o_sz:o_dt].astype(BF16), _pad_lanes(w_dt).astype(BF16),
            w_dt.T.astype(BF16), ssd_conv_w[l], ssd_conv_b[l].reshape(1, xbc_w),
            _pad_lanes(jnp.stack([ssd_a_log[l], ssd_dt_bias[l]])),
            jnp.stack([ssd_a_log[l], ssd_dt_bias[l]], axis=1),
            jnp.repeat(ssd_d[l], SSD_HEAD_DIM).reshape(1, inner),
            ssd_norm_w[l].reshape(1, inner), ts_ssd)

        x = _merge_call(x, mod_l, nw1, o_ga, o_hg, o_sd, wl[:, o_gate:o_end].astype(BF16),
                        w_br_a[l].astype(BF16), w_br_b[l].astype(BF16), w_br_c[l].astype(BF16),
                        w_out[l].astype(BF16), tm_merge)

        x = _ffn_call(x, mod_l, norm2_w[l].reshape(1, d), ffn_w_up[l].astype(BF16), ffn_conv_w[l],
                      ffn_conv_b[l].reshape(1, -1), ffn_w_down[l].astype(BF16), fnw, tm_ffn,
                      final=(l == depth - 1))
    return x
```

```python
import functools

import jax
import jax.numpy as jnp
from jax import lax
from jax.experimental import pallas as pl
from jax.experimental.pallas import tpu as pltpu

F32 = jnp.float32
BF16 = jnp.bfloat16
EPS = 1e-6

GDN_HEADS = 4
GDN_DK = 128
GDN_DV = 128
GDN_CONV = 4
GDN_CHUNK = 64
GDN_SOLVE_BASE = 8
HGRN_HEADS = 4
HGRN_DK = 128
HGRN_CHUNK = 16
SSD_HEADS = 8
SSD_HEAD_DIM = 64
SSD_GROUPS = 2
SSD_STATE = 128
SSD_CONV = 4
FFN_CONV = 3

HALO = 8
MXU_TILE = 256
VMEM_LIMIT = 56 * 1024 * 1024

_HI = lax.Precision.HIGHEST


def _mm(a, b):
    return jnp.dot(a.astype(BF16), b.astype(BF16), preferred_element_type=F32)


def _mm_nt(a, b):
    return lax.dot_general(a.astype(BF16), b.astype(BF16), (((1,), (1,)), ((), ())),
                           preferred_element_type=F32)


def _mm_tn(a, b):
    return lax.dot_general(a.astype(BF16), b.astype(BF16), (((0,), (0,)), ((), ())),
                           preferred_element_type=F32)


def _mm_hi(a, b):
    return jnp.dot(a, b, precision=_HI, preferred_element_type=F32)


def _sigmoid(x):
    return jax.nn.sigmoid(x)


def _silu(x):
    return x * jax.nn.sigmoid(x)


def _softplus(x):
    return jnp.maximum(x, 0.0) + jnp.log(1.0 + jnp.exp(-jnp.abs(x)))


def _mod_norm(x, nw, shift, scale):
    y = x * lax.rsqrt(jnp.mean(x * x, axis=-1, keepdims=True) + EPS) * nw
    return y * (1.0 + scale) + shift


def _tri(n, strict=False):
    r = lax.broadcasted_iota(jnp.int32, (n, n), 0)
    c = lax.broadcasted_iota(jnp.int32, (n, n), 1)
    return (r > c) if strict else (r >= c)


def _cparams(sem):
    return pltpu.CompilerParams(dimension_semantics=sem, vmem_limit_bytes=VMEM_LIMIT)


def _const_spec(shape):
    nd = len(shape)
    return pl.BlockSpec(shape, lambda *_: (0,) * nd)


def _ada_kernel(c_ref, w_ref, b_ref, o_ref):
    c = c_ref[...]
    o_ref[0] = _mm_hi(_silu(c), w_ref[0]) + b_ref[0]


def _ada_call(c, w_ada, b_ada):
    depth, d, n = w_ada.shape
    bsz = c.shape[0]
    tn = 1536
    return pl.pallas_call(
        _ada_kernel,
        grid=(depth, n // tn),
        in_specs=[
            pl.BlockSpec((bsz, d), lambda l, j: (0, 0)),
            pl.BlockSpec((1, d, tn), lambda l, j: (l, 0, j)),
            pl.BlockSpec((1, 1, tn), lambda l, j: (l, 0, j)),
        ],
        out_specs=pl.BlockSpec((1, bsz, tn), lambda l, j: (l, 0, j)),
        out_shape=jax.ShapeDtypeStruct((depth, bsz, n), F32),
        compiler_params=_cparams(("arbitrary", "arbitrary")),
    )(c, w_ada, b_ada.reshape(depth, 1, n))


def _split3(a):
    hi = a.astype(BF16)
    lo = (a - hi.astype(F32)).astype(BF16)
    return hi, lo


def _dot3(a_hi, a_lo, b_hi, b_lo):
    f = lambda p, q: jnp.dot(p, q, preferred_element_type=F32)
    return f(a_hi, b_hi) + (f(a_lo, b_hi) + f(a_hi, b_lo))


def _unit_lower_inverse(ms, bd_mask, base):
    c = ms[0].shape[0]
    nb = ms[0].shape[1] // c
    f = lambda p, q: jnp.dot(p, q, preferred_element_type=F32)
    row = lax.broadcasted_iota(jnp.int32, (c, nb * c), 0)
    col = lax.broadcasted_iota(jnp.int32, (c, nb * c), 1) % c
    eye = jnp.where(row == col, 1.0, 0.0).astype(F32)

    def blockdiag(x):
        return jnp.concatenate([x] * nb, axis=0) * bd_mask

    def dot3(a_hi, a_lo, b_hi, b_lo):
        r = a_hi.shape[0]
        t = f(jnp.concatenate([a_hi, a_lo], axis=0), blockdiag(b_hi))
        return t[:r] + (t[r:] + f(a_hi, blockdiag(b_lo)))

    n_items = range(len(ms))
    n_steps = base.bit_length() - 1
    in_base = (row // base) == (col // base)
    ds = [jnp.where(in_base, m, 0.0) for m in ms]
    xs = [eye - d for d in ds]
    if n_steps > 1:
        qs = []
        for d in ds:
            hi, lo = _split3(d)
            qs.append(dot3(hi, lo, hi, lo))
        for step in range(1, n_steps):
            for i in n_items:
                if step == n_steps - 1:
                    p_hi, p_lo = _split3(xs[i])
                    q_hi, q_lo = _split3(qs[i])
                    xs[i] = xs[i] + dot3(p_hi, p_lo, q_hi, q_lo)
                else:
                    hi, lo = _split3(jnp.concatenate([qs[i], xs[i]], axis=0))
                    r = dot3(hi, lo, hi[:c], lo[:c])
                    qs[i] = r[:c]
                    xs[i] = xs[i] + r[c:]
    b = base
    while b < c:
        level = ((row // (2 * b)) == (col // (2 * b))) & ((row // b) != (col // b))
        splits = [_split3(x) for x in xs]
        ts = []
        for i in n_items:
            l_hi, l_lo = _split3(jnp.where(level, ms[i], 0.0))
            ts.append(dot3(splits[i][0], splits[i][1], l_hi, l_lo))
        for i in n_items:
            t_hi, t_lo = _split3(ts[i])
            xs[i] = xs[i] - dot3(t_hi, t_lo, splits[i][0], splits[i][1])
        b *= 2
    return xs


def _gdn_kernel(x_ref, mod_ref, nw_ref, w_ref, wsm_ref, wsmt_ref, cw_ref, pcol_ref, prow_ref,
                gnw_ref, o_ref, ext_ref, st_ref, *, nb, rows):
    kw = GDN_HEADS * GDN_DK
    qkv_w = 2 * kw + GDN_HEADS * GDN_DV
    c = GDN_CHUNK
    ts = nb * rows
    cps = rows // c

    @pl.when(pl.program_id(1) == 0)
    def _():
        st_ref[...] = jnp.zeros_like(st_ref)
        ext_ref[:, 0:HALO, :] = jnp.zeros((nb, HALO, qkv_w), F32)

    m = mod_ref[...]
    h = _mod_norm(x_ref[...], nw_ref[...], m[:, 0:1, :], m[:, 1:2, :])
    h = h.reshape(ts, h.shape[-1]).astype(BF16)
    proj = jnp.dot(h, w_ref[...], preferred_element_type=F32)
    z = proj[:, qkv_w:]
    cw = cw_ref[...]
    accs = []
    for i in range(nb):
        p_i = proj[i * rows:(i + 1) * rows, :qkv_w]
        ext_ref[i, HALO:HALO + rows, :] = p_i
        acc = cw[GDN_CONV - 1:GDN_CONV] * p_i
        for t in range(GDN_CONV - 1):
            off = HALO - (GDN_CONV - 1) + t
            acc = acc + cw[t:t + 1] * ext_ref[i, off:off + rows, :]
        ext_ref[i, 0:HALO, :] = ext_ref[i, rows:rows + HALO, :]
        accs.append(acc)
    qkv = _silu(jnp.concatenate(accs, axis=0))

    sm = jnp.dot(h, wsm_ref[...], preferred_element_type=F32)
    smt = lax.dot_general(wsmt_ref[...], h, (((1,), (1,)), ((), ())), preferred_element_type=F32)
    pcol = pcol_ref[...]
    prow = prow_ref[...]
    g_cols = -jnp.exp(pcol[0:1]) * _softplus(sm + pcol[1:2])
    beta_cols = _sigmoid(sm)
    g_rows = -jnp.exp(prow[:, 0:1]) * _softplus(smt + prow[:, 1:2])

    rr = lax.broadcasted_iota(jnp.int32, (ts, ts), 0)
    cc = lax.broadcasted_iota(jnp.int32, (ts, ts), 1)
    same = (rr // c) == (cc // c)
    gc_all = _mm_hi(jnp.where(same & (rr >= cc), 1.0, 0.0).astype(F32), g_cols)
    gr_all = _mm_hi(g_rows, jnp.where(same & (rr <= cc), 1.0, 0.0).astype(F32))
    gnw = gnw_ref[...]

    heads = range(GDN_HEADS)
    hc = GDN_HEADS * c
    row_i = lax.broadcasted_iota(jnp.int32, (c, hc), 0)
    lane_i = lax.broadcasted_iota(jnp.int32, (c, hc), 1)
    lane_head = lane_i // c
    col_i = lane_i - lane_head * c
    incl_bd = row_i >= col_i
    strict_bd = row_i > col_i
    bd_mask = jnp.where(lax.broadcasted_iota(jnp.int32, (hc, hc), 0) // c
                        == lax.broadcasted_iota(jnp.int32, (hc, hc), 1) // c, 1.0, 0.0).astype(BF16)
    qn, kn, kbs, egs, ms, qks = {}, {}, {}, {}, [], []
    for ci in range(ts // c):
        r0 = ci * c
        g_col_bd = jnp.zeros((c, hc), F32)
        for hd in heads:
            q = qkv[r0:r0 + c, hd * GDN_DK:(hd + 1) * GDN_DK]
            k = qkv[r0:r0 + c, kw + hd * GDN_DK:kw + (hd + 1) * GDN_DK]
            q = q * lax.rsqrt(jnp.sum(q * q, axis=-1, keepdims=True) + EPS) * (GDN_DK ** -0.5)
            k = k * lax.rsqrt(jnp.sum(k * k, axis=-1, keepdims=True) + EPS)
            g_col = gc_all[r0:r0 + c, hd:hd + 1]
            qn[ci, hd], kn[ci, hd] = q, k
            kbs[ci, hd] = k * beta_cols[r0:r0 + c, GDN_HEADS + hd:GDN_HEADS + hd + 1]
            egs[ci, hd] = jnp.exp(g_col)
            g_col_bd = jnp.where(lane_head == hd, g_col, g_col_bd)
        g_row_bd = jnp.concatenate([gr_all[hd:hd + 1, r0:r0 + c] for hd in heads], axis=1)
        decay = jnp.where(incl_bd, jnp.exp(jnp.where(incl_bd, g_col_bd - g_row_bd, 0.0)), 0.0)
        kk = jnp.concatenate([_mm_nt(kbs[ci, hd], kn[ci, hd]) for hd in heads], axis=1)
        qk = jnp.concatenate([_mm_nt(qn[ci, hd], kn[ci, hd]) for hd in heads], axis=1)
        ms.append(jnp.where(strict_bd, kk * decay, 0.0))
        qks.append(qk * decay)
    invs = _unit_lower_inverse(ms, bd_mask, GDN_SOLVE_BASE)
    uw, attn = {}, {}
    for ci in range(ts // c):
        r0 = ci * c
        for hd in heads:
            v = qkv[r0:r0 + c, 2 * kw + hd * GDN_DV:2 * kw + (hd + 1) * GDN_DV]
            beta = beta_cols[r0:r0 + c, GDN_HEADS + hd:GDN_HEADS + hd + 1]
            rhs_hi, rhs_lo = _split3(jnp.concatenate([v * beta, kbs[ci, hd] * egs[ci, hd]], axis=1))
            inv_hi, inv_lo = _split3(invs[ci][:, hd * c:(hd + 1) * c])
            uw[ci, hd] = _dot3(inv_hi, inv_lo, rhs_hi, rhs_lo)
            attn[ci, hd] = qks[ci][:, hd * c:(hd + 1) * c]

    items = [(i, hd) for i in range(nb) for hd in heads]
    for j in range(cps):
        ci = {i: i * cps + j for i in range(nb)}
        st = {(i, hd): st_ref[i, hd] for i, hd in items}
        ws = {(i, hd): _mm(jnp.concatenate([uw[ci[i], hd][:, GDN_DV:],
                                            qn[ci[i], hd] * egs[ci[i], hd]], axis=0), st[i, hd])
              for i, hd in items}
        v_new = {(i, hd): uw[ci[i], hd][:, :GDN_DV] - ws[i, hd][:c] for i, hd in items}
        for i, hd in items:
            r0 = ci[i] * c
            g_col = gc_all[r0:r0 + c, hd:hd + 1]
            g_end = g_col[c - 1:c, :]
            k_end = kn[ci[i], hd] * jnp.exp(g_end - g_col)
            st_ref[i, hd] = st[i, hd] * jnp.exp(g_end) + _mm_tn(k_end, v_new[i, hd])
        for i, hd in items:
            r0 = ci[i] * c
            o = ws[i, hd][c:] + _mm(attn[ci[i], hd], v_new[i, hd])
            zz = z[r0:r0 + c, hd * GDN_DV:(hd + 1) * GDN_DV]
            o = o * lax.rsqrt(jnp.mean(o * o, axis=-1, keepdims=True) + EPS) * gnw * _silu(zz)
            o_ref[i, j * c:(j + 1) * c, hd * GDN_DV:(hd + 1) * GDN_DV] = o.astype(o_ref.dtype)


def _gdn_call(x, mod, nw, w, wsm, wsmt, cw, pcol, prow, gnw, nb, rows):
    bsz, s, d = x.shape
    kw = GDN_HEADS * GDN_DK
    vw = GDN_HEADS * GDN_DV
    qkv_w = 2 * kw + vw
    return pl.pallas_call(
        functools.partial(_gdn_kernel, nb=nb, rows=rows),
        grid=(bsz // nb, s // rows),
        in_specs=[
            pl.BlockSpec((nb, rows, d), lambda b, j: (b, j, 0)),
            pl.BlockSpec((nb, 6, d), lambda b, j: (b, 0, 0)),
            _const_spec(nw.shape), _const_spec(w.shape), _const_spec(wsm.shape),
            _const_spec(wsmt.shape), _const_spec(cw.shape), _const_spec(pcol.shape),
            _const_spec(prow.shape), _const_spec(gnw.shape),
        ],
        out_specs=pl.BlockSpec((nb, rows, vw), lambda b, j: (b, j, 0)),
        out_shape=jax.ShapeDtypeStruct((bsz, s, vw), BF16),
        scratch_shapes=[pltpu.VMEM((nb, rows + HALO, qkv_w), F32),
                        pltpu.VMEM((nb, GDN_HEADS, GDN_DK, GDN_DV), F32)],
        compiler_params=_cparams(("parallel", "arbitrary")),
    )(x, mod, nw, w, wsm, wsmt, cw, pcol, prow, gnw)


def _group_cumsum_rows(x, group):
    rows = lax.broadcasted_iota(jnp.int32, x.shape, 0) % group
    d = 1
    while d < group:
        x = x + jnp.where(rows >= d, pltpu.roll(x, d, axis=0), 0.0)
        d *= 2
    return x


def _hgrn_kernel(x_ref, mod_ref, nw_ref, w_ref, lbp_ref, gnw_ref, o_ref,
                 qs_ref, ks_ref, qg_ref, ke_ref, v_ref, ge_ref, z_ref, st_ref, *, ts, layer, unroll):
    kw = HGRN_HEADS * HGRN_DK
    c = HGRN_CHUNK

    @pl.when(pl.program_id(1) == 0)
    def _():
        st_ref[...] = jnp.zeros_like(st_ref)

    m = mod_ref[0]
    h = _mod_norm(x_ref[0], nw_ref[...], m[0:1], m[1:2]).astype(BF16)
    proj = jnp.dot(h, w_ref[...], preferred_element_type=F32)

    lbp = lbp_ref[...]
    e = jnp.exp(lbp - jnp.max(lbp, axis=0, keepdims=True))
    soft = e / jnp.sum(e, axis=0, keepdims=True)
    lb = jnp.zeros((1, kw), F32)
    for i in range(1, layer + 1):
        lb = lb + soft[i:i + 1]

    f_raw = proj[:, kw:2 * kw]
    n_sub = ts // c
    g = _group_cumsum_rows(jnp.log(lb + (1.0 - lb) * _sigmoid(f_raw)), c).reshape(n_sub, c, kw)
    q = _silu(proj[:, :kw]).reshape(n_sub, c, kw)
    k = ((1.0 - lb) * _sigmoid(-f_raw)).reshape(n_sub, c, kw)
    g_mid = g[:, c // 2 - 1:c // 2, :]
    g_last = g[:, c - 1:c, :]
    qs_ref[...] = (q * jnp.exp(g - g_mid)).reshape(ts, kw).astype(BF16)
    ks_ref[...] = (k * jnp.exp(g_mid - g)).reshape(ts, kw).astype(BF16)
    qg_ref[...] = (q * jnp.exp(g)).reshape(ts, kw).astype(BF16)
    ke_ref[...] = (k * jnp.exp(g_last - g)).reshape(ts, kw).astype(BF16)
    ge_ref[...] = jnp.broadcast_to(jnp.exp(g_last), (n_sub, c, kw)).reshape(ts, kw)
    v_ref[...] = proj[:, 2 * kw:3 * kw].astype(BF16)
    z_ref[...] = _silu(proj[:, 3 * kw:])

    incl = _tri(c)
    gnw = gnw_ref[...]
    heads = range(HGRN_HEADS)
    sl = [slice(hd * HGRN_DK, (hd + 1) * HGRN_DK) for hd in heads]

    def step(i, carry):
        rows = [pl.ds(pl.multiple_of((i * unroll + u) * c, c), c) for u in range(unroll)]
        pairs = [(u, hd) for u in range(unroll) for hd in heads]
        vv = [v_ref[r, :] for r in rows]
        sc = {(u, hd): _mm_nt(qs_ref[rows[u], sl[hd]], ks_ref[rows[u], sl[hd]]) for u, hd in pairs}
        kv = {(u, hd): _mm_tn(vv[u][:, sl[hd]], ke_ref[rows[u], sl[hd]]) for u, hd in pairs}
        oi = {(u, hd): _mm(jnp.where(incl, sc[u, hd], 0.0), vv[u][:, sl[hd]]) for u, hd in pairs}
        st = [st_ref[hd] for hd in heads]
        oo = {}
        for u in range(unroll):
            g_end = ge_ref[rows[u], :][0:1, :]
            for hd in heads:
                oo[u, hd] = oi[u, hd] + _mm_nt(qg_ref[rows[u], sl[hd]], st[hd])
                st[hd] = st[hd] * g_end[:, sl[hd]] + kv[u, hd]
        for hd in heads:
            st_ref[hd] = st[hd]
        for u, hd in pairs:
            o = oo[u, hd]
            o = o * lax.rsqrt(jnp.mean(o * o, axis=-1, keepdims=True) + EPS) * gnw * z_ref[rows[u], sl[hd]]
            o_ref[0, rows[u], sl[hd]] = o.astype(o_ref.dtype)
        return carry

    lax.fori_loop(0, n_sub // unroll, step, 0)


def _hgrn_call(x, mod, nw, w, lbp, gnw, ts, layer):
    bsz, s, d = x.shape
    kw = HGRN_HEADS * HGRN_DK
    return pl.pallas_call(
        functools.partial(_hgrn_kernel, ts=ts, layer=layer, unroll=8),
        grid=(bsz, s // ts),
        in_specs=[
            pl.BlockSpec((1, ts, d), lambda b, j: (b, j, 0)),
            pl.BlockSpec((1, 6, d), lambda b, j: (b, 0, 0)),
            _const_spec(nw.shape), _const_spec(w.shape), _const_spec(lbp.shape),
            _const_spec(gnw.shape),
        ],
        out_specs=pl.BlockSpec((1, ts, kw), lambda b, j: (b, j, 0)),
        out_shape=jax.ShapeDtypeStruct((bsz, s, kw), BF16),
        scratch_shapes=[pltpu.VMEM((ts, kw), BF16) for _ in range(5)]
        + [pltpu.VMEM((ts, kw), F32) for _ in range(2)]
        + [pltpu.VMEM((HGRN_HEADS, HGRN_DK, HGRN_DK), F32)],
        compiler_params=_cparams(("parallel", "arbitrary")),
    )(x, mod, nw, w, lbp, gnw)


def _ssd_kernel(x_ref, mod_ref, nw_ref, w_ref, wdt_ref, wdtt_ref, cw_ref, cb_ref, pcol_ref,
                prow_ref, dsk_ref, gnw_ref, o_ref, ext_ref, y_ref, st_ref, *, ts):
    inner = SSD_HEADS * SSD_HEAD_DIM
    xbc_w = inner + 2 * SSD_GROUPS * SSD_STATE
    hpg = SSD_HEADS // SSD_GROUPS
    p = SSD_HEAD_DIM

    @pl.when(pl.program_id(1) == 0)
    def _():
        st_ref[...] = jnp.zeros_like(st_ref)
        ext_ref[0:HALO, :] = jnp.zeros((HALO, xbc_w), F32)

    m = mod_ref[0]
    h = _mod_norm(x_ref[0], nw_ref[...], m[0:1], m[1:2]).astype(BF16)
    proj = jnp.dot(h, w_ref[...], preferred_element_type=F32)
    z = proj[:, :inner]
    ext_ref[HALO:HALO + ts, :] = proj[:, inner:]
    cw = cw_ref[...]
    acc = cw[SSD_CONV - 1:SSD_CONV] * proj[:, inner:] + cb_ref[...]
    for i in range(SSD_CONV - 1):
        off = HALO - (SSD_CONV - 1) + i
        acc = acc + cw[i:i + 1] * ext_ref[off:off + ts, :]
    ext_ref[0:HALO, :] = ext_ref[ts:ts + HALO, :]
    xbc = _silu(acc)
    xs = xbc[:, :inner]

    pcol = pcol_ref[...]
    prow = prow_ref[...]
    dt_cols = _softplus(jnp.dot(h, wdt_ref[...], preferred_element_type=F32) + pcol[1:2])
    dt_rows = _softplus(lax.dot_general(wdtt_ref[...], h, (((1,), (1,)), ((), ())),
                                        preferred_element_type=F32) + prow[:, 1:2])
    incl = _tri(ts)
    low = jnp.where(incl, 1.0, 0.0).astype(F32)
    upp = jnp.where(lax.broadcasted_iota(jnp.int32, (ts, ts), 0)
                    <= lax.broadcasted_iota(jnp.int32, (ts, ts), 1), 1.0, 0.0).astype(F32)
    acs_cols = _mm_hi(low, dt_cols * (-jnp.exp(pcol[0:1])))
    acs_rows = _mm_hi(dt_rows * (-jnp.exp(prow[:, 0:1])), upp)

    for g in range(SSD_GROUPS):
        bm = xbc[:, inner + g * SSD_STATE:inner + (g + 1) * SSD_STATE]
        cm = xbc[:, inner + (SSD_GROUPS + g) * SSD_STATE:inner + (SSD_GROUPS + g + 1) * SSD_STATE]
        cb = _mm_nt(cm, bm)
        for hh in range(hpg):
            hd = g * hpg + hh
            a_col = acs_cols[:, hd:hd + 1]
            a_row = acs_rows[hd:hd + 1, :]
            seg = jnp.where(incl, jnp.exp(jnp.where(incl, a_col - a_row, 0.0)), 0.0)
            x_h = xs[:, hd * p:(hd + 1) * p]
            xdt = x_h * dt_cols[:, hd:hd + 1]
            st = st_ref[hd]
            y = _mm(cb * seg, xdt) + _mm_nt(cm, st) * jnp.exp(a_col)
            last = a_col[ts - 1:ts, :]
            st_ref[hd] = st * jnp.exp(last) + _mm_tn(xdt * jnp.exp(last - a_col), bm)
            y_ref[:, hd * p:(hd + 1) * p] = y

    y = (y_ref[...] + dsk_ref[...] * xs) * _silu(z)
    gnw = gnw_ref[...]
    gw = hpg * p
    for g in range(SSD_GROUPS):
        yg = y[:, g * gw:(g + 1) * gw]
        yg = yg * lax.rsqrt(jnp.mean(yg * yg, axis=-1, keepdims=True) + EPS) * gnw[:, g * gw:(g + 1) * gw]
        o_ref[0, :, g * gw:(g + 1) * gw] = yg.astype(o_ref.dtype)


def _ssd_call(x, mod, nw, w, wdt, wdtt, cw, cb, pcol, prow, dsk, gnw, ts):
    bsz, s, d = x.shape
    inner = SSD_HEADS * SSD_HEAD_DIM
    xbc_w = inner + 2 * SSD_GROUPS * SSD_STATE
    return pl.pallas_call(
        functools.partial(_ssd_kernel, ts=ts),
        grid=(bsz, s // ts),
        in_specs=[
            pl.BlockSpec((1, ts, d), lambda b, j: (b, j, 0)),
            pl.BlockSpec((1, 6, d), lambda b, j: (b, 0, 0)),
            _const_spec(nw.shape), _const_spec(w.shape), _const_spec(wdt.shape),
            _const_spec(wdtt.shape), _const_spec(cw.shape), _const_spec(cb.shape),
            _const_spec(pcol.shape), _const_spec(prow.shape), _const_spec(dsk.shape),
            _const_spec(gnw.shape),
        ],
        out_specs=pl.BlockSpec((1, ts, inner), lambda b, j: (b, j, 0)),
        out_shape=jax.ShapeDtypeStruct((bsz, s, inner), BF16),
        scratch_shapes=[pltpu.VMEM((ts + HALO, xbc_w), F32),
                        pltpu.VMEM((ts, inner), F32),
                        pltpu.VMEM((SSD_HEADS, SSD_HEAD_DIM, SSD_STATE), F32)],
        compiler_params=_cparams(("parallel", "arbitrary")),
    )(x, mod, nw, w, wdt, wdtt, cw, cb, pcol, prow, dsk, gnw)


def _merge_kernel(x_ref, mod_ref, nw_ref, oa_ref, ob_ref, oc_ref, wg_ref, wa_ref, wb_ref, wc_ref,
                  wo_ref, o_ref):
    d = x_ref.shape[-1]
    x = x_ref[0]
    m = mod_ref[0]
    h = _mod_norm(x, nw_ref[...], m[0:1], m[1:2]).astype(BF16)
    gates = _sigmoid(jnp.dot(h, wg_ref[...], preferred_element_type=F32))
    merged = (gates[:, :d] * jnp.dot(oa_ref[0], wa_ref[...], preferred_element_type=F32)
              + gates[:, d:2 * d] * jnp.dot(ob_ref[0], wb_ref[...], preferred_element_type=F32)
              + gates[:, 2 * d:] * jnp.dot(oc_ref[0], wc_ref[...], preferred_element_type=F32))
    mix = jnp.dot(merged.astype(BF16), wo_ref[...], preferred_element_type=F32)
    o_ref[0] = x + m[2:3] * mix


def _merge_call(x, mod, nw, oa, ob, oc, wg, wa, wb, wc, wo, tm):
    bsz, s, d = x.shape
    tok = lambda w: pl.BlockSpec((1, tm, w), lambda b, j: (b, j, 0))
    return pl.pallas_call(
        _merge_kernel,
        grid=(bsz, s // tm),
        in_specs=[
            tok(d), pl.BlockSpec((1, 6, d), lambda b, j: (b, 0, 0)), _const_spec(nw.shape),
            tok(oa.shape[-1]), tok(ob.shape[-1]), tok(oc.shape[-1]),
            _const_spec(wg.shape), _const_spec(wa.shape), _const_spec(wb.shape),
            _const_spec(wc.shape), _const_spec(wo.shape),
        ],
        out_specs=tok(d),
        out_shape=jax.ShapeDtypeStruct((bsz, s, d), F32),
        compiler_params=_cparams(("parallel", "parallel")),
    )(x, mod, nw, oa, ob, oc, wg, wa, wb, wc, wo)


def _ffn_kernel(x_ref, mod_ref, nw_ref, wup_ref, cw_ref, cb_ref, wdn_ref, fnw_ref, o_ref,
                ext_ref, tail_ref, *, tm, splits, final):
    hidden = wdn_ref.shape[0]

    @pl.when(pl.program_id(1) == 0)
    def _():
        tail_ref[...] = jnp.zeros_like(tail_ref)

    x = x_ref[0]
    m = mod_ref[0]
    h = _mod_norm(x, nw_ref[...], m[3:4], m[4:5]).astype(BF16)

    def conv_part(c0, fc):
        u = jnp.dot(h, wup_ref[:, c0:c0 + fc], preferred_element_type=F32)
        ext_ref[0:HALO, 0:fc] = tail_ref[:, c0:c0 + fc]
        ext_ref[HALO:HALO + tm, 0:fc] = u
        cw = cw_ref[:, c0:c0 + fc]
        acc = cw[FFN_CONV - 1:FFN_CONV] * u + cb_ref[:, c0:c0 + fc]
        for i in range(FFN_CONV - 1):
            off = HALO - (FFN_CONV - 1) + i
            acc = acc + cw[i:i + 1] * ext_ref[off:off + tm, 0:fc]
        tail_ref[:, c0:c0 + fc] = ext_ref[tm:tm + HALO, 0:fc]
        return acc

    y = jnp.zeros_like(x)
    for c0, fc in splits:
        gate = conv_part(c0, fc)
        val = conv_part(hidden + c0, fc)
        act = (_silu(gate) * val).astype(BF16)
        y = y + jnp.dot(act, wdn_ref[c0:c0 + fc, :], preferred_element_type=F32)
    out = x + m[5:6] * y
    if final:
        out = out * lax.rsqrt(jnp.mean(out * out, axis=-1, keepdims=True) + EPS) * fnw_ref[...]
    o_ref[0] = out


def _ffn_call(x, mod, nw, wup, cw, cb, wdn, fnw, tm, final):
    bsz, s, d = x.shape
    hidden = wdn.shape[0]
    first = -(-(hidden // MXU_TILE) // 2) * MXU_TILE
    splits = ((0, first), (first, hidden - first))
    tok = pl.BlockSpec((1, tm, d), lambda b, j: (b, j, 0))
    single = lambda a: pl.BlockSpec(a.shape, lambda *_: (0,) * a.ndim, pipeline_mode=pl.Buffered(1))
    return pl.pallas_call(
        functools.partial(_ffn_kernel, tm=tm, splits=splits, final=final),
        grid=(bsz, s // tm),
        in_specs=[
            tok, pl.BlockSpec((1, 6, d), lambda b, j: (b, 0, 0)), _const_spec(nw.shape),
            single(wup), _const_spec(cw.shape), _const_spec(cb.shape), single(wdn),
            _const_spec(fnw.shape),
        ],
        out_specs=tok,
        out_shape=jax.ShapeDtypeStruct((bsz, s, d), F32),
        scratch_shapes=[pltpu.VMEM((tm + HALO, first), F32),
                        pltpu.VMEM((HALO, 2 * hidden), F32)],
        compiler_params=_cparams(("parallel", "arbitrary")),
    )(x, mod, nw, wup, cw, cb, wdn, fnw)


def _pad_lanes(a, width=128):
    return jnp.pad(a, ((0, 0), (0, width - a.shape[-1])))


def kernel(x, c, w_ada, b_ada, norm1_w, w_in, gdn_conv_w, gdn_a_log, gdn_dt_bias, gdn_norm_w,
           hgrn_lb_param, hgrn_norm_w, ssd_conv_w, ssd_conv_b, ssd_a_log, ssd_dt_bias, ssd_d,
           ssd_norm_w, w_br_a, w_br_b, w_br_c, w_out, norm2_w, ffn_w_up, ffn_conv_w, ffn_conv_b,
           ffn_w_down, final_norm_w):
    bsz, s, d = x.shape
    depth = w_in.shape[0]
    gk = GDN_HEADS * GDN_DK
    gv = GDN_HEADS * GDN_DV
    hk = HGRN_HEADS * HGRN_DK
    inner = SSD_HEADS * SSD_HEAD_DIM
    xbc_w = inner + 2 * SSD_GROUPS * SSD_STATE

    sizes = (2 * gk + gv, GDN_HEADS, GDN_HEADS, gv, hk, hk, hk, hk, inner, xbc_w, SSD_HEADS, 3 * d)
    offs = [0]
    for sz in sizes:
        offs.append(offs[-1] + sz)
    (o_qkv, o_a, o_b, o_z, o_hq, _, _, _, o_sz, o_xbc, o_dt, o_gate, o_end) = offs

    nb_gdn = 4 if bsz % 4 == 0 else (2 if bsz % 2 == 0 else 1)
    rows_gdn = min(max(GDN_CHUNK, 256 // nb_gdn), s)
    ts_hgrn = min(512, s)
    ts_ssd = min(256, s)
    tm_merge = min(512, s)
    tm_ffn = min(512, s)

    mod = _ada_call(c, w_ada, b_ada).reshape(depth, bsz, 6, d)
    fnw = final_norm_w.reshape(1, d)

    for l in range(depth):
        wl = w_in[l]
        mod_l = mod[l]
        nw1 = norm1_w[l].reshape(1, d)

        w_gdn = jnp.concatenate([wl[:, o_qkv:o_a], wl[:, o_z:o_hq]], axis=1).astype(BF16)
        w_ab = wl[:, o_a:o_z]
        o_ga = _gdn_call(
            x, mod_l, nw1, w_gdn, _pad_lanes(w_ab).astype(BF16), w_ab.T.astype(BF16),
            gdn_conv_w[l],
            _pad_lanes(jnp.stack([gdn_a_log[l], gdn_dt_bias[l]])),
            jnp.pad(jnp.stack([gdn_a_log[l], gdn_dt_bias[l]], axis=1), ((0, GDN_HEADS), (0, 0))),
            gdn_norm_w[l].reshape(1, GDN_DV), nb_gdn, rows_gdn)

        o_hg = _hgrn_call(x, mod_l, nw1, wl[:, o_hq:o_sz].astype(BF16), hgrn_lb_param,
                          hgrn_norm_w[l].reshape(1, HGRN_DK), ts_hgrn, l)

        w_dt = wl[:, o_dt:o_gate]
        o_sd = _ssd_call(
            x, mod_l, nw1, wl[:, o_sz:o_dt].astype(BF16), _pad_lanes(w_dt).astype(BF16),
            w_dt.T.astype(BF16), ssd_conv_w[l], ssd_conv_b[l].reshape(1, xbc_w),
            _pad_lanes(jnp.stack([ssd_a_log[l], ssd_dt_bias[l]])),
            jnp.stack([ssd_a_log[l], ssd_dt_bias[l]], axis=1),
            jnp.repeat(ssd_d[l], SSD_HEAD_DIM).reshape(1, inner),
            ssd_norm_w[l].reshape(1, inner), ts_ssd)

        x = _merge_call(x, mod_l, nw1, o_ga, o_hg, o_sd, wl[:, o_gate:o_end].astype(BF16),
                        w_br_a[l].astype(BF16), w_br_b[l].astype(BF16), w_br_c[l].astype(BF16),
                        w_out[l].astype(BF16), tm_merge)

        x = _ffn_call(x, mod_l, norm2_w[l].reshape(1, d), ffn_w_up[l].astype(BF16), ffn_conv_w[l],
                      ffn_conv_b[l].reshape(1, -1), ffn_w_down[l].astype(BF16), fnw, tm_ffn,
                      final=(l == depth - 1))
    return x
```

```python
import functools

import jax
import jax.numpy as jnp
from jax import lax
from jax.experimental import pallas as pl
from jax.experimental.pallas import tpu as pltpu

F32 = jnp.float32
BF16 = jnp.bfloat16
EPS = 1e-6

GDN_HEADS = 4
GDN_DK = 128
GDN_DV = 128
GDN_CONV = 4
GDN_CHUNK = 64
GDN_SOLVE_BASE = 8
HGRN_HEADS = 4
HGRN_DK = 128
HGRN_CHUNK = 16
SSD_HEADS = 8
SSD_HEAD_DIM = 64
SSD_GROUPS = 2
SSD_STATE = 128
SSD_CONV = 4
FFN_CONV = 3

HALO = 8
MXU_TILE = 256
VMEM_LIMIT = 56 * 1024 * 1024

_HI = lax.Precision.HIGHEST


def _mm(a, b):
    return jnp.dot(a.astype(BF16), b.astype(BF16), preferred_element_type=F32)


def _mm_nt(a, b):
    return lax.dot_general(a.astype(BF16), b.astype(BF16), (((1,), (1,)), ((), ())),
                           preferred_element_type=F32)


def _mm_tn(a, b):
    return lax.dot_general(a.astype(BF16), b.astype(BF16), (((0,), (0,)), ((), ())),
                           preferred_element_type=F32)


def _mm_hi(a, b):
    return jnp.dot(a, b, precision=_HI, preferred_element_type=F32)


def _sigmoid(x):
    return jax.nn.sigmoid(x)


def _silu(x):
    return x * jax.nn.sigmoid(x)


def _softplus(x):
    return jnp.maximum(x, 0.0) + jnp.log(1.0 + jnp.exp(-jnp.abs(x)))


def _mod_norm(x, nw, shift, scale):
    y = x * lax.rsqrt(jnp.mean(x * x, axis=-1, keepdims=True) + EPS) * nw
    return y * (1.0 + scale) + shift


def _tri(n, strict=False):
    r = lax.broadcasted_iota(jnp.int32, (n, n), 0)
    c = lax.broadcasted_iota(jnp.int32, (n, n), 1)
    return (r > c) if strict else (r >= c)


def _cparams(sem):
    return pltpu.CompilerParams(dimension_semantics=sem, vmem_limit_bytes=VMEM_LIMIT)


def _const_spec(shape):
    nd = len(shape)
    return pl.BlockSpec(shape, lambda *_: (0,) * nd)


def _ada_kernel(c_ref, w_ref, b_ref, o_ref):
    c = c_ref[...]
    o_ref[0] = _mm_hi(_silu(c), w_ref[0]) + b_ref[0]


def _ada_call(c, w_ada, b_ada):
    depth, d, n = w_ada.shape
    bsz = c.shape[0]
    tn = 1536
    return pl.pallas_call(
        _ada_kernel,
        grid=(depth, n // tn),
        in_specs=[
            pl.BlockSpec((bsz, d), lambda l, j: (0, 0)),
            pl.BlockSpec((1, d, tn), lambda l, j: (l, 0, j)),
            pl.BlockSpec((1, 1, tn), lambda l, j: (l, 0, j)),
        ],
        out_specs=pl.BlockSpec((1, bsz, tn), lambda l, j: (l, 0, j)),
        out_shape=jax.ShapeDtypeStruct((depth, bsz, n), F32),
        compiler_params=_cparams(("arbitrary", "arbitrary")),
    )(c, w_ada, b_ada.reshape(depth, 1, n))


def _split3(a):
    hi = a.astype(BF16)
    lo = (a - hi.astype(F32)).astype(BF16)
    return hi, lo


def _dot3(a_hi, a_lo, b_hi, b_lo):
    f = lambda p, q: jnp.dot(p, q, preferred_element_type=F32)
    return f(a_hi, b_hi) + (f(a_lo, b_hi) + f(a_hi, b_lo))


def _unit_lower_inverse(ms, bd_mask, base):
    c = ms[0].shape[0]
    nb = ms[0].shape[1] // c
    f = lambda p, q: jnp.dot(p, q, preferred_element_type=F32)
    row = lax.broadcasted_iota(jnp.int32, (c, nb * c), 0)
    col = lax.broadcasted_iota(jnp.int32, (c, nb * c), 1) % c
    eye = jnp.where(row == col, 1.0, 0.0).astype(F32)

    def blockdiag(x):
        return jnp.concatenate([x] * nb, axis=0) * bd_mask

    def dot3(a_hi, a_lo, b_hi, b_lo):
        r = a_hi.shape[0]
        t = f(jnp.concatenate([a_hi, a_lo], axis=0), blockdiag(b_hi))
        return t[:r] + (t[r:] + f(a_hi, blockdiag(b_lo)))

    n_items = range(len(ms))
    n_steps = base.bit_length() - 1
    in_base = (row // base) == (col // base)
    ds = [jnp.where(in_base, m, 0.0) for m in ms]
    xs = [eye - d for d in ds]
    if n_steps > 1:
        qs = []
        for d in ds:
            hi, lo = _split3(d)
            qs.append(dot3(hi, lo, hi, lo))
        for step in range(1, n_steps):
            for i in n_items:
                if step == n_steps - 1:
                    p_hi, p_lo = _split3(xs[i])
                    q_hi, q_lo = _split3(qs[i])
                    xs[i] = xs[i] + dot3(p_hi, p_lo, q_hi, q_lo)
                else:
                    hi, lo = _split3(jnp.concatenate([qs[i], xs[i]], axis=0))
                    r = dot3(hi, lo, hi[:c], lo[:c])
                    qs[i] = r[:c]
                    xs[i] = xs[i] + r[c:]
    b = base
    while b < c:
        level = ((row // (2 * b)) == (col // (2 * b))) & ((row // b) != (col // b))
        splits = [_split3(x) for x in xs]
        ts = []
        for i in n_items:
            l_hi, l_lo = _split3(jnp.where(level, ms[i], 0.0))
            ts.append(dot3(splits[i][0], splits[i][1], l_hi, l_lo))
        for i in n_items:
            t_hi, t_lo = _split3(ts[i])
            xs[i] = xs[i] - dot3(t_hi, t_lo, splits[i][0], splits[i][1])
        b *= 2
    return xs


def _gdn_kernel(x_ref, mod_ref, nw_ref, w_ref, wsm_ref, wsmt_ref, cw_ref, pcol_ref, prow_ref,
                gnw_ref, o_ref, ext_ref, st_ref, *, nb, rows):
    kw = GDN_HEADS * GDN_DK
    qkv_w = 2 * kw + GDN_HEADS * GDN_DV
    c = GDN_CHUNK
    ts = nb * rows
    cps = rows // c

    @pl.when(pl.program_id(1) == 0)
    def _():
        st_ref[...] = jnp.zeros_like(st_ref)
        ext_ref[:, 0:HALO, :] = jnp.zeros((nb, HALO, qkv_w), F32)

    m = mod_ref[...]
    h = _mod_norm(x_ref[...], nw_ref[...], m[:, 0:1, :], m[:, 1:2, :])
    h = h.reshape(ts, h.shape[-1]).astype(BF16)
    proj = jnp.dot(h, w_ref[...], preferred_element_type=F32)
    z = proj[:, qkv_w:]
    cw = cw_ref[...]
    accs = []
    for i in range(nb):
        p_i = proj[i * rows:(i + 1) * rows, :qkv_w]
        ext_ref[i, HALO:HALO + rows, :] = p_i
        acc = cw[GDN_CONV - 1:GDN_CONV] * p_i
        for t in range(GDN_CONV - 1):
            off = HALO - (GDN_CONV - 1) + t
            acc = acc + cw[t:t + 1] * ext_ref[i, off:off + rows, :]
        ext_ref[i, 0:HALO, :] = ext_ref[i, rows:rows + HALO, :]
        accs.append(acc)
    qkv = _silu(jnp.concatenate(accs, axis=0))

    sm = jnp.dot(h, wsm_ref[...], preferred_element_type=F32)
    smt = lax.dot_general(wsmt_ref[...], h, (((1,), (1,)), ((), ())), preferred_element_type=F32)
    pcol = pcol_ref[...]
    prow = prow_ref[...]
    g_cols = -jnp.exp(pcol[0:1]) * _softplus(sm + pcol[1:2])
    beta_cols = _sigmoid(sm)
    g_rows = -jnp.exp(prow[:, 0:1]) * _softplus(smt + prow[:, 1:2])

    rr = lax.broadcasted_iota(jnp.int32, (ts, ts), 0)
    cc = lax.broadcasted_iota(jnp.int32, (ts, ts), 1)
    same = (rr // c) == (cc // c)
    gc_all = _mm_hi(jnp.where(same & (rr >= cc), 1.0, 0.0).astype(F32), g_cols)
    gr_all = _mm_hi(g_rows, jnp.where(same & (rr <= cc), 1.0, 0.0).astype(F32))
    gnw = gnw_ref[...]

    heads = range(GDN_HEADS)
    hc = GDN_HEADS * c
    row_i = lax.broadcasted_iota(jnp.int32, (c, hc), 0)
    lane_i = lax.broadcasted_iota(jnp.int32, (c, hc), 1)
    lane_head = lane_i // c
    col_i = lane_i - lane_head * c
    incl_bd = row_i >= col_i
    strict_bd = row_i > col_i
    bd_mask = jnp.where(lax.broadcasted_iota(jnp.int32, (hc, hc), 0) // c
                        == lax.broadcasted_iota(jnp.int32, (hc, hc), 1) // c, 1.0, 0.0).astype(BF16)
    qn, kn, kbs, egs, ms, qks = {}, {}, {}, {}, [], []
    for ci in range(ts // c):
        r0 = ci * c
        g_col_bd = jnp.zeros((c, hc), F32)
        for hd in heads:
            q = qkv[r0:r0 + c, hd * GDN_DK:(hd + 1) * GDN_DK]
            k = qkv[r0:r0 + c, kw + hd * GDN_DK:kw + (hd + 1) * GDN_DK]
            q = q * lax.rsqrt(jnp.sum(q * q, axis=-1, keepdims=True) + EPS) * (GDN_DK ** -0.5)
            k = k * lax.rsqrt(jnp.sum(k * k, axis=-1, keepdims=True) + EPS)
            g_col = gc_all[r0:r0 + c, hd:hd + 1]
            qn[ci, hd], kn[ci, hd] = q, k
            kbs[ci, hd] = k * beta_cols[r0:r0 + c, GDN_HEADS + hd:GDN_HEADS + hd + 1]
            egs[ci, hd] = jnp.exp(g_col)
            g_col_bd = jnp.where(lane_head == hd, g_col, g_col_bd)
        g_row_bd = jnp.concatenate([gr_all[hd:hd + 1, r0:r0 + c] for hd in heads], axis=1)
        decay = jnp.where(incl_bd, jnp.exp(jnp.where(incl_bd, g_col_bd - g_row_bd, 0.0)), 0.0)
        kk = jnp.concatenate([_mm_nt(kbs[ci, hd], kn[ci, hd]) for hd in heads], axis=1)
        qk = jnp.concatenate([_mm_nt(qn[ci, hd], kn[ci, hd]) for hd in heads], axis=1)
        ms.append(jnp.where(strict_bd, kk * decay, 0.0))
        qks.append(qk * decay)
    invs = _unit_lower_inverse(ms, bd_mask, GDN_SOLVE_BASE)
    uw, attn = {}, {}
    for ci in range(ts // c):
        r0 = ci * c
        for hd in heads:
            v = qkv[r0:r0 + c, 2 * kw + hd * GDN_DV:2 * kw + (hd + 1) * GDN_DV]
            beta = beta_cols[r0:r0 + c, GDN_HEADS + hd:GDN_HEADS + hd + 1]
            rhs_hi, rhs_lo = _split3(jnp.concatenate([v * beta, kbs[ci, hd] * egs[ci, hd]], axis=1))
            inv_hi, inv_lo = _split3(invs[ci][:, hd * c:(hd + 1) * c])
            uw[ci, hd] = _dot3(inv_hi, inv_lo, rhs_hi, rhs_lo)
            attn[ci, hd] = qks[ci][:, hd * c:(hd + 1) * c]

    items = [(i, hd) for i in range(nb) for hd in heads]
    for j in range(cps):
        ci = {i: i * cps + j for i in range(nb)}
        st = {(i, hd): st_ref[i, hd] for i, hd in items}
        ws = {(i, hd): _mm(jnp.concatenate([uw[ci[i], hd][:, GDN_DV:],
                                            qn[ci[i], hd] * egs[ci[i], hd]], axis=0), st[i, hd])
              for i, hd in items}
        v_new = {(i, hd): uw[ci[i], hd][:, :GDN_DV] - ws[i, hd][:c] for i, hd in items}
        for i, hd in items:
            r0 = ci[i] * c
            g_col = gc_all[r0:r0 + c, hd:hd + 1]
            g_end = g_col[c - 1:c, :]
            k_end = kn[ci[i], hd] * jnp.exp(g_end - g_col)
            st_ref[i, hd] = st[i, hd] * jnp.exp(g_end) + _mm_tn(k_end, v_new[i, hd])
        for i, hd in items:
            r0 = ci[i] * c
            o = ws[i, hd][c:] + _mm(attn[ci[i], hd], v_new[i, hd])
            zz = z[r0:r0 + c, hd * GDN_DV:(hd + 1) * GDN_DV]
            o = o * lax.rsqrt(jnp.mean(o * o, axis=-1, keepdims=True) + EPS) * gnw * _silu(zz)
            o_ref[i, j * c:(j + 1) * c, hd * GDN_DV:(hd + 1) * GDN_DV] = o.astype(o_ref.dtype)


def _gdn_call(x, mod, nw, w, wsm, wsmt, cw, pcol, prow, gnw, nb, rows):
    bsz, s, d = x.shape
    kw = GDN_HEADS * GDN_DK
    vw = GDN_HEADS * GDN_DV
    qkv_w = 2 * kw + vw
    return pl.pallas_call(
        functools.partial(_gdn_kernel, nb=nb, rows=rows),
        grid=(bsz // nb, s // rows),
        in_specs=[
            pl.BlockSpec((nb, rows, d), lambda b, j: (b, j, 0)),
            pl.BlockSpec((nb, 6, d), lambda b, j: (b, 0, 0)),
            _const_spec(nw.shape), _const_spec(w.shape), _const_spec(wsm.shape),
            _const_spec(wsmt.shape), _const_spec(cw.shape), _const_spec(pcol.shape),
            _const_spec(prow.shape), _const_spec(gnw.shape),
        ],
        out_specs=pl.BlockSpec((nb, rows, vw), lambda b, j: (b, j, 0)),
        out_shape=jax.ShapeDtypeStruct((bsz, s, vw), BF16),
        scratch_shapes=[pltpu.VMEM((nb, rows + HALO, qkv_w), F32),
                        pltpu.VMEM((nb, GDN_HEADS, GDN_DK, GDN_DV), F32)],
        compiler_params=_cparams(("parallel", "arbitrary")),
    )(x, mod, nw, w, wsm, wsmt, cw, pcol, prow, gnw)


def _group_cumsum_rows(x, group):
    rows = lax.broadcasted_iota(jnp.int32, x.shape, 0) % group
    d = 1
    while d < group:
        x = x + jnp.where(rows >= d, pltpu.roll(x, d, axis=0), 0.0)
        d *= 2
    return x


def _hgrn_kernel(x_ref, mod_ref, nw_ref, w_ref, lbp_ref, gnw_ref, o_ref, *scratch, ts, layer, unit, group):
    kw = HGRN_HEADS * HGRN_DK
    c = HGRN_CHUNK
    n_sub = unit // c
    n_units = ts // unit
    sets = (scratch[0:7], scratch[7:14])
    st_ref = scratch[14]

    @pl.when(pl.program_id(1) == 0)
    def _():
        st_ref[...] = jnp.zeros_like(st_ref)

    m = mod_ref[0]
    nw = nw_ref[...]
    gnw = gnw_ref[...]
    incl = _tri(c)
    heads = range(HGRN_HEADS)
    sl = [slice(hd * HGRN_DK, (hd + 1) * HGRN_DK) for hd in heads]

    lbp = lbp_ref[...]
    e = jnp.exp(lbp - jnp.max(lbp, axis=0, keepdims=True))
    soft = e / jnp.sum(e, axis=0, keepdims=True)
    lb = jnp.zeros((1, kw), F32)
    for i in range(1, layer + 1):
        lb = lb + soft[i:i + 1]

    def prepare(u_idx, dst):
        qs_ref, ks_ref, qg_ref, ke_ref, v_ref, ge_ref, z_ref = dst
        rows = pl.ds(pl.multiple_of(u_idx * unit, unit), unit)
        h = _mod_norm(x_ref[0, rows, :], nw, m[0:1], m[1:2]).astype(BF16)
        yield
        f_raw = jnp.dot(h, w_ref[:, kw:2 * kw], preferred_element_type=F32)
        sig = _sigmoid(f_raw)
        g = _group_cumsum_rows(jnp.log(lb + (1.0 - lb) * sig), c).reshape(n_sub, c, kw)
        g_mid = g[:, c // 2 - 1:c // 2, :]
        g_last = g[:, c - 1:c, :]
        ge_ref[...] = jnp.broadcast_to(jnp.exp(g_last), (n_sub, c, kw)).reshape(unit, kw)
        yield
        k = ((1.0 - lb) * _sigmoid(-f_raw)).reshape(n_sub, c, kw)
        ks_ref[...] = (k * jnp.exp(g_mid - g)).reshape(unit, kw).astype(BF16)
        ke_ref[...] = (k * jnp.exp(g_last - g)).reshape(unit, kw).astype(BF16)
        yield
        q = _silu(jnp.dot(h, w_ref[:, :kw], preferred_element_type=F32)).reshape(n_sub, c, kw)
        qs_ref[...] = (q * jnp.exp(g - g_mid)).reshape(unit, kw).astype(BF16)
        qg_ref[...] = (q * jnp.exp(g)).reshape(unit, kw).astype(BF16)
        yield
        v_ref[...] = jnp.dot(h, w_ref[:, 2 * kw:3 * kw], preferred_element_type=F32).astype(BF16)
        yield
        z_ref[...] = _silu(jnp.dot(h, w_ref[:, 3 * kw:], preferred_element_type=F32))

    def recur(u_idx, src):
        qs_ref, ks_ref, qg_ref, ke_ref, v_ref, ge_ref, z_ref = src
        base = pl.multiple_of(u_idx * unit, unit)
        for g0 in range(0, n_sub, group):
            subs = range(g0, g0 + group)
            rows = {u: slice(u * c, (u + 1) * c) for u in subs}
            pairs = [(u, hd) for u in subs for hd in heads]
            vv = {u: v_ref[rows[u], :] for u in subs}
            sc = {(u, hd): _mm_nt(qs_ref[rows[u], sl[hd]], ks_ref[rows[u], sl[hd]]) for u, hd in pairs}
            yield
            kv = {(u, hd): _mm_tn(vv[u][:, sl[hd]], ke_ref[rows[u], sl[hd]]) for u, hd in pairs}
            yield
            oi = {(u, hd): _mm(jnp.where(incl, sc[u, hd], 0.0), vv[u][:, sl[hd]]) for u, hd in pairs}
            yield
            st = [st_ref[hd] for hd in heads]
            oo = {}
            for u in subs:
                g_end = ge_ref[u * c:u * c + 1, :]
                for hd in heads:
                    oo[u, hd] = oi[u, hd] + _mm_nt(qg_ref[rows[u], sl[hd]], st[hd])
                    st[hd] = st[hd] * g_end[:, sl[hd]] + kv[u, hd]
                if u == g0 + group // 2 - 1:
                    yield
            for hd in heads:
                st_ref[hd] = st[hd]
            yield
            for u, hd in pairs:
                o = oo[u, hd]
                o = o * lax.rsqrt(jnp.mean(o * o, axis=-1, keepdims=True) + EPS) * gnw * z_ref[rows[u], sl[hd]]
                o_ref[0, pl.ds(base + u * c, c), sl[hd]] = o.astype(o_ref.dtype)
            yield

    def alternate(*gens):
        live = list(gens)
        while live:
            for gen in list(live):
                if next(gen, live) is live:
                    live.remove(gen)

    alternate(prepare(0, sets[0]))

    def body(i, carry):
        alternate(recur(2 * i, sets[0]), prepare(2 * i + 1, sets[1]))
        alternate(recur(2 * i + 1, sets[1]), prepare(jnp.minimum(2 * i + 2, n_units - 1), sets[0]))
        return carry

    lax.fori_loop(0, n_units // 2, body, 0)


def _hgrn_call(x, mod, nw, w, lbp, gnw, ts, layer):
    bsz, s, d = x.shape
    kw = HGRN_HEADS * HGRN_DK
    group = 8
    unit = min(256, ts // 2)
    one_set = ([pltpu.VMEM((unit, kw), BF16) for _ in range(5)]
               + [pltpu.VMEM((unit, kw), F32) for _ in range(2)])
    return pl.pallas_call(
        functools.partial(_hgrn_kernel, ts=ts, layer=layer, unit=unit, group=group),
        grid=(bsz, s // ts),
        in_specs=[
            pl.BlockSpec((1, ts, d), lambda b, j: (b, j, 0)),
            pl.BlockSpec((1, 6, d), lambda b, j: (b, 0, 0)),
            _const_spec(nw.shape), _const_spec(w.shape), _const_spec(lbp.shape),
            _const_spec(gnw.shape),
        ],
        out_specs=pl.BlockSpec((1, ts, kw), lambda b, j: (b, j, 0)),
        out_shape=jax.ShapeDtypeStruct((bsz, s, kw), BF16),
        scratch_shapes=one_set + one_set + [pltpu.VMEM((HGRN_HEADS, HGRN_DK, HGRN_DK), F32)],
        compiler_params=_cparams(("parallel", "arbitrary")),
    )(x, mod, nw, w, lbp, gnw)


def _ssd_kernel(x_ref, mod_ref, nw_ref, w_ref, wdt_ref, wdtt_ref, cw_ref, cb_ref, pcol_ref,
                prow_ref, dsk_ref, gnw_ref, o_ref, ext_ref, y_ref, st_ref, *, ts):
    inner = SSD_HEADS * SSD_HEAD_DIM
    xbc_w = inner + 2 * SSD_GROUPS * SSD_STATE
    hpg = SSD_HEADS // SSD_GROUPS
    p = SSD_HEAD_DIM

    @pl.when(pl.program_id(1) == 0)
    def _():
        st_ref[...] = jnp.zeros_like(st_ref)
        ext_ref[0:HALO, :] = jnp.zeros((HALO, xbc_w), F32)

    m = mod_ref[0]
    h = _mod_norm(x_ref[0], nw_ref[...], m[0:1], m[1:2]).astype(BF16)
    proj = jnp.dot(h, w_ref[...], preferred_element_type=F32)
    z = proj[:, :inner]
    ext_ref[HALO:HALO + ts, :] = proj[:, inner:]
    cw = cw_ref[...]
    acc = cw[SSD_CONV - 1:SSD_CONV] * proj[:, inner:] + cb_ref[...]
    for i in range(SSD_CONV - 1):
        off = HALO - (SSD_CONV - 1) + i
        acc = acc + cw[i:i + 1] * ext_ref[off:off + ts, :]
    ext_ref[0:HALO, :] = ext_ref[ts:ts + HALO, :]
    xbc = _silu(acc)
    xs = xbc[:, :inner]

    pcol = pcol_ref[...]
    prow = prow_ref[...]
    dt_cols = _softplus(jnp.dot(h, wdt_ref[...], preferred_element_type=F32) + pcol[1:2])
    dt_rows = _softplus(lax.dot_general(wdtt_ref[...], h, (((1,), (1,)), ((), ())),
                                        preferred_element_type=F32) + prow[:, 1:2])
    incl = _tri(ts)
    low = jnp.where(incl, 1.0, 0.0).astype(F32)
    upp = jnp.where(lax.broadcasted_iota(jnp.int32, (ts, ts), 0)
                    <= lax.broadcasted_iota(jnp.int32, (ts, ts), 1), 1.0, 0.0).astype(F32)
    acs_cols = _mm_hi(low, dt_cols * (-jnp.exp(pcol[0:1])))
    acs_rows = _mm_hi(dt_rows * (-jnp.exp(prow[:, 0:1])), upp)

    first = lax.broadcasted_iota(jnp.int32, (ts, 2 * p), 1) < p
    first_rows = lax.broadcasted_iota(jnp.int32, (2 * p, SSD_STATE), 0) < p
    for g in range(SSD_GROUPS):
        bm = xbc[:, inner + g * SSD_STATE:inner + (g + 1) * SSD_STATE]
        cm = xbc[:, inner + (SSD_GROUPS + g) * SSD_STATE:inner + (SSD_GROUPS + g + 1) * SSD_STATE]
        cb = _mm_nt(cm, bm)
        for pr in range(hpg // 2):
            h0 = g * hpg + 2 * pr
            h1 = h0 + 1
            a0, a1 = acs_cols[:, h0:h0 + 1], acs_cols[:, h1:h1 + 1]
            seg0 = jnp.where(incl, jnp.exp(a0 - acs_rows[h0:h0 + 1, :]), 0.0)
            seg1 = jnp.where(incl, jnp.exp(a1 - acs_rows[h1:h1 + 1, :]), 0.0)
            xdt = xs[:, h0 * p:(h1 + 1) * p] * jnp.where(first, dt_cols[:, h0:h0 + 1], dt_cols[:, h1:h1 + 1])
            st = st_ref[g * (hpg // 2) + pr]
            y = (_mm(cb * seg0, jnp.where(first, xdt, 0.0)) + _mm(cb * seg1, jnp.where(first, 0.0, xdt))
                 + _mm_nt(cm, st) * jnp.where(first, jnp.exp(a0), jnp.exp(a1)))
            last0, last1 = a0[ts - 1:ts, :], a1[ts - 1:ts, :]
            w_in = jnp.where(first, jnp.exp(last0 - a0), jnp.exp(last1 - a1))
            st_ref[g * (hpg // 2) + pr] = (st * jnp.where(first_rows, jnp.exp(last0), jnp.exp(last1))
                                           + _mm_tn(xdt * w_in, bm))
            y_ref[:, h0 * p:(h1 + 1) * p] = y

    y = (y_ref[...] + dsk_ref[...] * xs) * _silu(z)
    gnw = gnw_ref[...]
    gw = hpg * p
    for g in range(SSD_GROUPS):
        yg = y[:, g * gw:(g + 1) * gw]
        yg = yg * lax.rsqrt(jnp.mean(yg * yg, axis=-1, keepdims=True) + EPS) * gnw[:, g * gw:(g + 1) * gw]
        o_ref[0, :, g * gw:(g + 1) * gw] = yg.astype(o_ref.dtype)


def _ssd_call(x, mod, nw, w, wdt, wdtt, cw, cb, pcol, prow, dsk, gnw, ts):
    bsz, s, d = x.shape
    inner = SSD_HEADS * SSD_HEAD_DIM
    xbc_w = inner + 2 * SSD_GROUPS * SSD_STATE
    return pl.pallas_call(
        functools.partial(_ssd_kernel, ts=ts),
        grid=(bsz, s // ts),
        in_specs=[
            pl.BlockSpec((1, ts, d), lambda b, j: (b, j, 0)),
            pl.BlockSpec((1, 6, d), lambda b, j: (b, 0, 0)),
            _const_spec(nw.shape), _const_spec(w.shape), _const_spec(wdt.shape),
            _const_spec(wdtt.shape), _const_spec(cw.shape), _const_spec(cb.shape),
            _const_spec(pcol.shape), _const_spec(prow.shape), _const_spec(dsk.shape),
            _const_spec(gnw.shape),
        ],
        out_specs=pl.BlockSpec((1, ts, inner), lambda b, j: (b, j, 0)),
        out_shape=jax.ShapeDtypeStruct((bsz, s, inner), BF16),
        scratch_shapes=[pltpu.VMEM((ts + HALO, xbc_w), F32),
                        pltpu.VMEM((ts, inner), F32),
                        pltpu.VMEM((SSD_HEADS // 2, 2 * SSD_HEAD_DIM, SSD_STATE), F32)],
        compiler_params=_cparams(("parallel", "arbitrary")),
    )(x, mod, nw, w, wdt, wdtt, cw, cb, pcol, prow, dsk, gnw)


def _merge_kernel(x_ref, mod_ref, nw_ref, oa_ref, ob_ref, oc_ref, wg_ref, wa_ref, wb_ref, wc_ref,
                  wo_ref, o_ref):
    d = x_ref.shape[-1]
    x = x_ref[0]
    m = mod_ref[0]
    h = _mod_norm(x, nw_ref[...], m[0:1], m[1:2]).astype(BF16)
    gates = _sigmoid(jnp.dot(h, wg_ref[...], preferred_element_type=F32))
    merged = (gates[:, :d] * jnp.dot(oa_ref[0], wa_ref[...], preferred_element_type=F32)
              + gates[:, d:2 * d] * jnp.dot(ob_ref[0], wb_ref[...], preferred_element_type=F32)
              + gates[:, 2 * d:] * jnp.dot(oc_ref[0], wc_ref[...], preferred_element_type=F32))
    mix = jnp.dot(merged.astype(BF16), wo_ref[...], preferred_element_type=F32)
    o_ref[0] = x + m[2:3] * mix


def _merge_call(x, mod, nw, oa, ob, oc, wg, wa, wb, wc, wo, tm):
    bsz, s, d = x.shape
    tok = lambda w: pl.BlockSpec((1, tm, w), lambda b, j: (b, j, 0))
    return pl.pallas_call(
        _merge_kernel,
        grid=(bsz, s // tm),
        in_specs=[
            tok(d), pl.BlockSpec((1, 6, d), lambda b, j: (b, 0, 0)), _const_spec(nw.shape),
            tok(oa.shape[-1]), tok(ob.shape[-1]), tok(oc.shape[-1]),
            _const_spec(wg.shape), _const_spec(wa.shape), _const_spec(wb.shape),
            _const_spec(wc.shape), _const_spec(wo.shape),
        ],
        out_specs=tok(d),
        out_shape=jax.ShapeDtypeStruct((bsz, s, d), F32),
        compiler_params=_cparams(("parallel", "parallel")),
    )(x, mod, nw, oa, ob, oc, wg, wa, wb, wc, wo)


def _ffn_kernel(x_ref, mod_ref, nw_ref, wup_ref, cw_ref, cb_ref, wdn_ref, fnw_ref, o_ref,
                ext_ref, tail_ref, *, tm, splits, final):
    hidden = wdn_ref.shape[0]

    @pl.when(pl.program_id(1) == 0)
    def _():
        tail_ref[...] = jnp.zeros_like(tail_ref)

    x = x_ref[0]
    m = mod_ref[0]
    h = _mod_norm(x, nw_ref[...], m[3:4], m[4:5]).astype(BF16)

    def conv_part(c0, fc):
        u = jnp.dot(h, wup_ref[:, c0:c0 + fc], preferred_element_type=F32)
        ext_ref[0:HALO, 0:fc] = tail_ref[:, c0:c0 + fc]
        ext_ref[HALO:HALO + tm, 0:fc] = u
        cw = cw_ref[:, c0:c0 + fc]
        acc = cw[FFN_CONV - 1:FFN_CONV] * u + cb_ref[:, c0:c0 + fc]
        for i in range(FFN_CONV - 1):
            off = HALO - (FFN_CONV - 1) + i
            acc = acc + cw[i:i + 1] * ext_ref[off:off + tm, 0:fc]
        tail_ref[:, c0:c0 + fc] = ext_ref[tm:tm + HALO, 0:fc]
        return acc

    y = jnp.zeros_like(x)
    for c0, fc in splits:
        gate = conv_part(c0, fc)
        val = conv_part(hidden + c0, fc)
        act = (_silu(gate) * val).astype(BF16)
        y = y + jnp.dot(act, wdn_ref[c0:c0 + fc, :], preferred_element_type=F32)
    out = x + m[5:6] * y
    if final:
        out = out * lax.rsqrt(jnp.mean(out * out, axis=-1, keepdims=True) + EPS) * fnw_ref[...]
    o_ref[0] = out


def _ffn_call(x, mod, nw, wup, cw, cb, wdn, fnw, tm, final):
    bsz, s, d = x.shape
    hidden = wdn.shape[0]
    first = -(-(hidden // MXU_TILE) // 2) * MXU_TILE
    splits = ((0, first), (first, hidden - first))
    tok = pl.BlockSpec((1, tm, d), lambda b, j: (b, j, 0))
    single = lambda a: pl.BlockSpec(a.shape, lambda *_: (0,) * a.ndim, pipeline_mode=pl.Buffered(1))
    return pl.pallas_call(
        functools.partial(_ffn_kernel, tm=tm, splits=splits, final=final),
        grid=(bsz, s // tm),
        in_specs=[
            tok, pl.BlockSpec((1, 6, d), lambda b, j: (b, 0, 0)), _const_spec(nw.shape),
            single(wup), _const_spec(cw.shape), _const_spec(cb.shape), single(wdn),
            _const_spec(fnw.shape),
        ],
        out_specs=tok,
        out_shape=jax.ShapeDtypeStruct((bsz, s, d), F32),
        scratch_shapes=[pltpu.VMEM((tm + HALO, first), F32),
                        pltpu.VMEM((HALO, 2 * hidden), F32)],
        compiler_params=_cparams(("parallel", "arbitrary")),
    )(x, mod, nw, wup, cw, cb, wdn, fnw)


def _pad_lanes(a, width=128):
    return jnp.pad(a, ((0, 0), (0, width - a.shape[-1])))


def kernel(x, c, w_ada, b_ada, norm1_w, w_in, gdn_conv_w, gdn_a_log, gdn_dt_bias, gdn_norm_w,
           hgrn_lb_param, hgrn_norm_w, ssd_conv_w, ssd_conv_b, ssd_a_log, ssd_dt_bias, ssd_d,
           ssd_norm_w, w_br_a, w_br_b, w_br_c, w_out, norm2_w, ffn_w_up, ffn_conv_w, ffn_conv_b,
           ffn_w_down, final_norm_w):
    bsz, s, d = x.shape
    depth = w_in.shape[0]
    gk = GDN_HEADS * GDN_DK
    gv = GDN_HEADS * GDN_DV
    hk = HGRN_HEADS * HGRN_DK
    inner = SSD_HEADS * SSD_HEAD_DIM
    xbc_w = inner + 2 * SSD_GROUPS * SSD_STATE

    sizes = (2 * gk + gv, GDN_HEADS, GDN_HEADS, gv, hk, hk, hk, hk, inner, xbc_w, SSD_HEADS, 3 * d)
    offs = [0]
    for sz in sizes:
        offs.append(offs[-1] + sz)
    (o_qkv, o_a, o_b, o_z, o_hq, _, _, _, o_sz, o_xbc, o_dt, o_gate, o_end) = offs

    nb_gdn = 4 if bsz % 4 == 0 else (2 if bsz % 2 == 0 else 1)
    rows_gdn = min(max(GDN_CHUNK, 256 // nb_gdn), s)
    ts_hgrn = min(1024, s)
    ts_ssd = min(256, s)
    tm_merge = min(512, s)
    tm_ffn = min(512, s)

    mod = _ada_call(c, w_ada, b_ada).reshape(depth, bsz, 6, d)
    fnw = final_norm_w.reshape(1, d)

    for l in range(depth):
        wl = w_in[l]
        mod_l = mod[l]
        nw1 = norm1_w[l].reshape(1, d)

        w_gdn = jnp.concatenate([wl[:, o_qkv:o_a], wl[:, o_z:o_hq]], axis=1).astype(BF16)
        w_ab = wl[:, o_a:o_z]
        o_ga = _gdn_call(
            x, mod_l, nw1, w_gdn, _pad_lanes(w_ab).astype(BF16), w_ab.T.astype(BF16),
            gdn_conv_w[l],
            _pad_lanes(jnp.stack([gdn_a_log[l], gdn_dt_bias[l]])),
            jnp.pad(jnp.stack([gdn_a_log[l], gdn_dt_bias[l]], axis=1), ((0, GDN_HEADS), (0, 0))),
            gdn_norm_w[l].reshape(1, GDN_DV), nb_gdn, rows_gdn)

        o_hg = _hgrn_call(x, mod_l, nw1, wl[:, o_hq:o_sz].astype(BF16), hgrn_lb_param,
                          hgrn_norm_w[l].reshape(1, HGRN_DK), ts_hgrn, l)

        w_dt = wl[:, o_dt:o_gate]
        o_sd = _ssd_call(
            x, mod_l, nw1, wl[:, o_sz:o_dt].astype(BF16), _pad_lanes(w_dt).astype(BF16),
            w_dt.T.astype(BF16), ssd_conv_w[l], ssd_conv_b[l].reshape(1, xbc_w),
            _pad_lanes(jnp.stack([ssd_a_log[l], ssd_dt_bias[l]])),
            jnp.stack([ssd_a_log[l], ssd_dt_bias[l]], axis=1),
            jnp.repeat(ssd_d[l], SSD_HEAD_DIM).reshape(1, inner),
            ssd_norm_w[l].reshape(1, inner), ts_ssd)

        x = _merge_call(x, mod_l, nw1, o_ga, o_hg, o_sd, wl[:, o_gate:o_end].astype(BF16),
                        w_br_a[l].astype(BF16), w_br_b[l].astype(BF16), w_br_c[l].astype(BF16),
                        w_out[l].astype(BF16), tm_merge)

        x = _ffn_call(x, mod_l, norm2_w[l].reshape(1, d), ffn_w_up[l].astype(BF16), ffn_conv_w[l],
                      ffn_conv_b[l].reshape(1, -1), ffn_w_down[l].astype(BF16), fnw, tm_ffn,
                      final=(l == depth - 1))
    return x
```

```python
import functools

import jax
import jax.numpy as jnp
from jax import lax
from jax.experimental import pallas as pl
from jax.experimental.pallas import tpu as pltpu

F32 = jnp.float32
BF16 = jnp.bfloat16
EPS = 1e-6

GDN_HEADS = 4
GDN_DK = 128
GDN_DV = 128
GDN_CONV = 4
GDN_CHUNK = 64
GDN_SOLVE_BASE = 8
HGRN_HEADS = 4
HGRN_DK = 128
HGRN_CHUNK = 16
SSD_HEADS = 8
SSD_HEAD_DIM = 64
SSD_GROUPS = 2
SSD_STATE = 128
SSD_CONV = 4
FFN_CONV = 3

HALO = 8
MXU_TILE = 256
VMEM_LIMIT = 56 * 1024 * 1024

_HI = lax.Precision.HIGHEST


def _mm(a, b):
    return jnp.dot(a.astype(BF16), b.astype(BF16), preferred_element_type=F32)


def _mm_nt(a, b):
    return lax.dot_general(a.astype(BF16), b.astype(BF16), (((1,), (1,)), ((), ())),
                           preferred_element_type=F32)


def _mm_tn(a, b):
    return lax.dot_general(a.astype(BF16), b.astype(BF16), (((0,), (0,)), ((), ())),
                           preferred_element_type=F32)


def _mm_hi(a, b):
    return jnp.dot(a, b, precision=_HI, preferred_element_type=F32)


def _sigmoid(x):
    return jax.nn.sigmoid(x)


def _silu(x):
    return x * jax.nn.sigmoid(x)


def _softplus(x):
    return jnp.maximum(x, 0.0) + jnp.log(1.0 + jnp.exp(-jnp.abs(x)))


def _mod_norm(x, nw, shift, scale):
    y = x * lax.rsqrt(jnp.mean(x * x, axis=-1, keepdims=True) + EPS) * nw
    return y * (1.0 + scale) + shift


def _tri(n, strict=False):
    r = lax.broadcasted_iota(jnp.int32, (n, n), 0)
    c = lax.broadcasted_iota(jnp.int32, (n, n), 1)
    return (r > c) if strict else (r >= c)


def _cparams(sem):
    return pltpu.CompilerParams(dimension_semantics=sem, vmem_limit_bytes=VMEM_LIMIT)


def _const_spec(shape):
    nd = len(shape)
    return pl.BlockSpec(shape, lambda *_: (0,) * nd)


def _alternate(*gens):
    live = list(gens)
    while live:
        for gen in list(live):
            if next(gen, live) is live:
                live.remove(gen)


def _ada_kernel(c_ref, w_ref, b_ref, o_ref):
    c = c_ref[...]
    o_ref[0] = _mm_hi(_silu(c), w_ref[0]) + b_ref[0]


def _ada_call(c, w_ada, b_ada):
    depth, d, n = w_ada.shape
    bsz = c.shape[0]
    tn = 1536
    return pl.pallas_call(
        _ada_kernel,
        grid=(depth, n // tn),
        in_specs=[
            pl.BlockSpec((bsz, d), lambda l, j: (0, 0)),
            pl.BlockSpec((1, d, tn), lambda l, j: (l, 0, j)),
            pl.BlockSpec((1, 1, tn), lambda l, j: (l, 0, j)),
        ],
        out_specs=pl.BlockSpec((1, bsz, tn), lambda l, j: (l, 0, j)),
        out_shape=jax.ShapeDtypeStruct((depth, bsz, n), F32),
        compiler_params=_cparams(("arbitrary", "arbitrary")),
    )(c, w_ada, b_ada.reshape(depth, 1, n))


def _split3(a):
    hi = a.astype(BF16)
    lo = (a - hi.astype(F32)).astype(BF16)
    return hi, lo


def _dot3(a_hi, a_lo, b_hi, b_lo):
    f = lambda p, q: jnp.dot(p, q, preferred_element_type=F32)
    return f(a_hi, b_hi) + (f(a_lo, b_hi) + f(a_hi, b_lo))


def _unit_lower_inverse(ms, bd_mask, base):
    c = ms[0].shape[0]
    nb = ms[0].shape[1] // c
    f = lambda p, q: jnp.dot(p, q, preferred_element_type=F32)
    row = lax.broadcasted_iota(jnp.int32, (c, nb * c), 0)
    col = lax.broadcasted_iota(jnp.int32, (c, nb * c), 1) % c
    eye = jnp.where(row == col, 1.0, 0.0).astype(F32)

    def blockdiag(x):
        return jnp.concatenate([x] * nb, axis=0) * bd_mask

    def dot3(a_hi, a_lo, b_hi, b_lo):
        r = a_hi.shape[0]
        t = f(jnp.concatenate([a_hi, a_lo], axis=0), blockdiag(b_hi))
        return t[:r] + (t[r:] + f(a_hi, blockdiag(b_lo)))

    n_items = range(len(ms))
    n_steps = base.bit_length() - 1
    in_base = (row // base) == (col // base)
    ds = [jnp.where(in_base, m, 0.0) for m in ms]
    xs = [eye - d for d in ds]
    if n_steps > 1:
        qs = []
        for d in ds:
            hi, lo = _split3(d)
            qs.append(dot3(hi, lo, hi, lo))
        for step in range(1, n_steps):
            for i in n_items:
                if step == n_steps - 1:
                    p_hi, p_lo = _split3(xs[i])
                    q_hi, q_lo = _split3(qs[i])
                    xs[i] = xs[i] + dot3(p_hi, p_lo, q_hi, q_lo)
                else:
                    hi, lo = _split3(jnp.concatenate([qs[i], xs[i]], axis=0))
                    r = dot3(hi, lo, hi[:c], lo[:c])
                    qs[i] = r[:c]
                    xs[i] = xs[i] + r[c:]
            yield
    b = base
    while b < c:
        level = ((row // (2 * b)) == (col // (2 * b))) & ((row // b) != (col // b))
        splits = [_split3(x) for x in xs]
        ts = []
        for i in n_items:
            l_hi, l_lo = _split3(jnp.where(level, ms[i], 0.0))
            ts.append(dot3(splits[i][0], splits[i][1], l_hi, l_lo))
        yield
        for i in n_items:
            t_hi, t_lo = _split3(ts[i])
            xs[i] = xs[i] - dot3(t_hi, t_lo, splits[i][0], splits[i][1])
        yield
        b *= 2
    return xs


def _gdn_kernel(x0_ref, xa_ref, xb_ref, mod_ref, nw_ref, w_ref, wsm_ref, wsmt_ref, cw_ref,
                pcol_ref, prow_ref, gnw_ref, o_ref, ext_ref, st_ref, *sets, nb, rows):
    kw = GDN_HEADS * GDN_DK
    vw = GDN_HEADS * GDN_DV
    qkv_w = 2 * kw + vw
    c = GDN_CHUNK
    ts = nb * rows
    n_chunks = ts // c
    heads = range(GDN_HEADS)
    hc = GDN_HEADS * c
    set_a, set_b = sets[:7], sets[7:]
    wd = GDN_DV + GDN_DK

    m = mod_ref[...]
    nw = nw_ref[...]
    gnw = gnw_ref[...]
    row_i = lax.broadcasted_iota(jnp.int32, (c, hc), 0)
    lane_i = lax.broadcasted_iota(jnp.int32, (c, hc), 1)
    lane_head = lane_i // c
    col_i = lane_i - lane_head * c
    incl_bd = row_i >= col_i
    strict_bd = row_i > col_i

    def prepare(x_ref, dst):
        qg_s, kn_s, rhs_s, z_s, gcol_s, m_s, qk_s = dst
        h = _mod_norm(x_ref[...], nw, m[:, 0:1, :], m[:, 1:2, :])
        h = h.reshape(ts, h.shape[-1]).astype(BF16)
        yield
        proj = jnp.dot(h, w_ref[...], preferred_element_type=F32)
        z_s[...] = _silu(proj[:, qkv_w:])
        yield
        cw = cw_ref[...]
        accs = []
        for i in range(nb):
            p_i = proj[i * rows:(i + 1) * rows, :qkv_w]
            ext_ref[i, HALO:HALO + rows, :] = p_i
            acc = cw[GDN_CONV - 1:GDN_CONV] * p_i
            for t in range(GDN_CONV - 1):
                off = HALO - (GDN_CONV - 1) + t
                acc = acc + cw[t:t + 1] * ext_ref[i, off:off + rows, :]
            ext_ref[i, 0:HALO, :] = ext_ref[i, rows:rows + HALO, :]
            accs.append(acc)
            yield
        qkv = _silu(jnp.concatenate(accs, axis=0))

        sm = jnp.dot(h, wsm_ref[...], preferred_element_type=F32)
        smt = lax.dot_general(wsmt_ref[...], h, (((1,), (1,)), ((), ())), preferred_element_type=F32)
        pcol = pcol_ref[...]
        prow = prow_ref[...]
        g_cols = -jnp.exp(pcol[0:1]) * _softplus(sm + pcol[1:2])
        beta_cols = _sigmoid(sm)
        g_rows = -jnp.exp(prow[:, 0:1]) * _softplus(smt + prow[:, 1:2])
        rr = lax.broadcasted_iota(jnp.int32, (ts, ts), 0)
        cc = lax.broadcasted_iota(jnp.int32, (ts, ts), 1)
        same = (rr // c) == (cc // c)
        gc_all = _mm_hi(jnp.where(same & (rr >= cc), 1.0, 0.0).astype(F32), g_cols)
        gr_all = _mm_hi(g_rows, jnp.where(same & (rr <= cc), 1.0, 0.0).astype(F32))
        gcol_s[...] = gc_all
        yield
        for ci in range(n_chunks):
            r0 = ci * c
            g_col_bd = jnp.zeros((c, hc), F32)
            kks, qks = [], []
            for hd in heads:
                q = qkv[r0:r0 + c, hd * GDN_DK:(hd + 1) * GDN_DK]
                k = qkv[r0:r0 + c, kw + hd * GDN_DK:kw + (hd + 1) * GDN_DK]
                v = qkv[r0:r0 + c, 2 * kw + hd * GDN_DV:2 * kw + (hd + 1) * GDN_DV]
                q = q * lax.rsqrt(jnp.sum(q * q, axis=-1, keepdims=True) + EPS) * (GDN_DK ** -0.5)
                k = k * lax.rsqrt(jnp.sum(k * k, axis=-1, keepdims=True) + EPS)
                g_col = gc_all[r0:r0 + c, hd:hd + 1]
                beta = beta_cols[r0:r0 + c, GDN_HEADS + hd:GDN_HEADS + hd + 1]
                eg = jnp.exp(g_col)
                kb = k * beta
                qg_s[r0:r0 + c, hd * GDN_DK:(hd + 1) * GDN_DK] = q * eg
                kn_s[r0:r0 + c, hd * GDN_DK:(hd + 1) * GDN_DK] = k
                rhs_s[r0:r0 + c, hd * wd:hd * wd + GDN_DV] = v * beta
                rhs_s[r0:r0 + c, hd * wd + GDN_DV:(hd + 1) * wd] = kb * eg
                kks.append(_mm_nt(kb, k))
                qks.append(_mm_nt(q, k))
                g_col_bd = jnp.where(lane_head == hd, g_col, g_col_bd)
            g_row_bd = jnp.concatenate([gr_all[hd:hd + 1, r0:r0 + c] for hd in heads], axis=1)
            decay = jnp.where(incl_bd, jnp.exp(g_col_bd - g_row_bd), 0.0)
            m_s[r0:r0 + c, :] = jnp.where(strict_bd, jnp.concatenate(kks, axis=1) * decay, 0.0)
            qk_s[r0:r0 + c, :] = jnp.concatenate(qks, axis=1) * decay
            yield

    def finish(src, out_rows):
        qg_s, kn_s, rhs_s, z_s, gcol_s, m_s, qk_s = src
        bd_mask = jnp.where(lax.broadcasted_iota(jnp.int32, (hc, hc), 0) // c
                            == lax.broadcasted_iota(jnp.int32, (hc, hc), 1) // c, 1.0, 0.0).astype(BF16)
        ms = [m_s[ci * c:(ci + 1) * c, :] for ci in range(n_chunks)]
        invs = yield from _unit_lower_inverse(ms, bd_mask, GDN_SOLVE_BASE)
        pairs = [(ci, hd) for ci in range(n_chunks) for hd in heads]
        uw = {}
        for ci, hd in pairs:
            r0 = ci * c
            rhs_hi, rhs_lo = _split3(rhs_s[r0:r0 + c, hd * wd:(hd + 1) * wd])
            inv_hi, inv_lo = _split3(invs[ci][:, hd * c:(hd + 1) * c])
            uw[ci, hd] = _dot3(inv_hi, inv_lo, rhs_hi, rhs_lo)
            if hd == GDN_HEADS - 1:
                yield
        cps = rows // c
        items = [(i, hd) for i in range(nb) for hd in heads]
        for j in range(cps):
            ci = {i: i * cps + j for i in range(nb)}
            hsl = {hd: slice(hd * GDN_DK, (hd + 1) * GDN_DK) for hd in heads}
            st = {(i, hd): st_ref[i, hd] for i, hd in items}
            ws = {(i, hd): _mm(jnp.concatenate([uw[ci[i], hd][:, GDN_DV:],
                                                qg_s[ci[i] * c:(ci[i] + 1) * c, hsl[hd]]], axis=0), st[i, hd])
                  for i, hd in items}
            yield
            v_new = {(i, hd): uw[ci[i], hd][:, :GDN_DV] - ws[i, hd][:c] for i, hd in items}
            for i, hd in items:
                r0 = ci[i] * c
                g_col = gcol_s[r0:r0 + c, hd:hd + 1]
                g_end = g_col[c - 1:c, :]
                k_end = kn_s[r0:r0 + c, hsl[hd]] * jnp.exp(g_end - g_col)
                st_ref[i, hd] = st[i, hd] * jnp.exp(g_end) + _mm_tn(k_end, v_new[i, hd])
            yield
            for i, hd in items:
                r0 = ci[i] * c
                o = ws[i, hd][c:] + _mm(qk_s[r0:r0 + c, hd * c:(hd + 1) * c], v_new[i, hd])
                o = o * lax.rsqrt(jnp.mean(o * o, axis=-1, keepdims=True) + EPS) * gnw * z_s[r0:r0 + c, hsl[hd]]
                o_ref[i, out_rows + j * c:out_rows + (j + 1) * c, hsl[hd]] = o.astype(o_ref.dtype)
                if hd == GDN_HEADS - 1:
                    yield

    @pl.when(pl.program_id(1) == 0)
    def _():
        st_ref[...] = jnp.zeros_like(st_ref)
        ext_ref[:, 0:HALO, :] = jnp.zeros((nb, HALO, qkv_w), F32)
        _alternate(prepare(x0_ref, set_a))

    _alternate(finish(set_a, 0), prepare(xa_ref, set_b))
    _alternate(finish(set_b, rows), prepare(xb_ref, set_a))


def _gdn_call(x, mod, nw, w, wsm, wsmt, cw, pcol, prow, gnw, nb, rows):
    bsz, s, d = x.shape
    kw = GDN_HEADS * GDN_DK
    vw = GDN_HEADS * GDN_DV
    qkv_w = 2 * kw + vw
    ts = nb * rows
    n_tiles = s // rows
    hc = GDN_HEADS * GDN_CHUNK
    one_set = [pltpu.VMEM((ts, kw), F32), pltpu.VMEM((ts, kw), F32),
               pltpu.VMEM((ts, vw + kw), F32), pltpu.VMEM((ts, vw), F32),
               pltpu.VMEM((ts, 128), F32), pltpu.VMEM((ts, hc), F32), pltpu.VMEM((ts, hc), F32)]
    tile = lambda f: pl.BlockSpec((nb, rows, d), f)
    return pl.pallas_call(
        functools.partial(_gdn_kernel, nb=nb, rows=rows),
        grid=(bsz // nb, n_tiles // 2),
        in_specs=[
            tile(lambda b, j: (b, 0, 0)),
            tile(lambda b, j: (b, 2 * j + 1, 0)),
            tile(lambda b, j: (b, jnp.minimum(2 * j + 2, n_tiles - 1), 0)),
            pl.BlockSpec((nb, 6, d), lambda b, j: (b, 0, 0)),
            _const_spec(nw.shape), _const_spec(w.shape), _const_spec(wsm.shape),
            _const_spec(wsmt.shape), _const_spec(cw.shape), _const_spec(pcol.shape),
            _const_spec(prow.shape), _const_spec(gnw.shape),
        ],
        out_specs=pl.BlockSpec((nb, 2 * rows, vw), lambda b, j: (b, j, 0)),
        out_shape=jax.ShapeDtypeStruct((bsz, s, vw), BF16),
        scratch_shapes=[pltpu.VMEM((nb, rows + HALO, qkv_w), F32),
                        pltpu.VMEM((nb, GDN_HEADS, GDN_DK, GDN_DV), F32)] + one_set + one_set,
        compiler_params=_cparams(("parallel", "arbitrary")),
    )(x, x, x, mod, nw, w, wsm, wsmt, cw, pcol, prow, gnw)


def _group_cumsum_rows(x, group):
    rows = lax.broadcasted_iota(jnp.int32, x.shape, 0) % group
    d = 1
    while d < group:
        x = x + jnp.where(rows >= d, pltpu.roll(x, d, axis=0), 0.0)
        d *= 2
    return x


def _hgrn_kernel(x_ref, mod_ref, nw_ref, w_ref, lbp_ref, gnw_ref, o_ref, *scratch, ts, layer, unit, group):
    kw = HGRN_HEADS * HGRN_DK
    c = HGRN_CHUNK
    n_sub = unit // c
    n_units = ts // unit
    sets = (scratch[0:7], scratch[7:14])
    st_ref = scratch[14]

    @pl.when(pl.program_id(1) == 0)
    def _():
        st_ref[...] = jnp.zeros_like(st_ref)

    m = mod_ref[0]
    nw = nw_ref[...]
    gnw = gnw_ref[...]
    incl = _tri(c)
    heads = range(HGRN_HEADS)
    sl = [slice(hd * HGRN_DK, (hd + 1) * HGRN_DK) for hd in heads]

    lbp = lbp_ref[...]
    e = jnp.exp(lbp - jnp.max(lbp, axis=0, keepdims=True))
    soft = e / jnp.sum(e, axis=0, keepdims=True)
    lb = jnp.zeros((1, kw), F32)
    for i in range(1, layer + 1):
        lb = lb + soft[i:i + 1]

    def prepare(u_idx, dst):
        qs_ref, ks_ref, qg_ref, ke_ref, v_ref, ge_ref, z_ref = dst
        rows = pl.ds(pl.multiple_of(u_idx * unit, unit), unit)
        h = _mod_norm(x_ref[0, rows, :], nw, m[0:1], m[1:2]).astype(BF16)
        yield
        f_raw = jnp.dot(h, w_ref[:, kw:2 * kw], preferred_element_type=F32)
        sig = _sigmoid(f_raw)
        g = _group_cumsum_rows(jnp.log(lb + (1.0 - lb) * sig), c).reshape(n_sub, c, kw)
        g_mid = g[:, c // 2 - 1:c // 2, :]
        g_last = g[:, c - 1:c, :]
        ge_ref[...] = jnp.broadcast_to(jnp.exp(g_last), (n_sub, c, kw)).reshape(unit, kw)
        yield
        k = ((1.0 - lb) * _sigmoid(-f_raw)).reshape(n_sub, c, kw)
        ks_ref[...] = (k * jnp.exp(g_mid - g)).reshape(unit, kw).astype(BF16)
        ke_ref[...] = (k * jnp.exp(g_last - g)).reshape(unit, kw).astype(BF16)
        yield
        q = _silu(jnp.dot(h, w_ref[:, :kw], preferred_element_type=F32)).reshape(n_sub, c, kw)
        qs_ref[...] = (q * jnp.exp(g - g_mid)).reshape(unit, kw).astype(BF16)
        qg_ref[...] = (q * jnp.exp(g)).reshape(unit, kw).astype(BF16)
        yield
        v_ref[...] = jnp.dot(h, w_ref[:, 2 * kw:3 * kw], preferred_element_type=F32).astype(BF16)
        yield
        z_ref[...] = _silu(jnp.dot(h, w_ref[:, 3 * kw:], preferred_element_type=F32))

    def recur(u_idx, src):
        qs_ref, ks_ref, qg_ref, ke_ref, v_ref, ge_ref, z_ref = src
        base = pl.multiple_of(u_idx * unit, unit)
        for g0 in range(0, n_sub, group):
            subs = range(g0, g0 + group)
            rows = {u: slice(u * c, (u + 1) * c) for u in subs}
            pairs = [(u, hd) for u in subs for hd in heads]
            vv = {u: v_ref[rows[u], :] for u in subs}
            sc = {(u, hd): _mm_nt(qs_ref[rows[u], sl[hd]], ks_ref[rows[u], sl[hd]]) for u, hd in pairs}
            yield
            kv = {(u, hd): _mm_tn(vv[u][:, sl[hd]], ke_ref[rows[u], sl[hd]]) for u, hd in pairs}
            yield
            oi = {(u, hd): _mm(jnp.where(incl, sc[u, hd], 0.0), vv[u][:, sl[hd]]) for u, hd in pairs}
            yield
            st = [st_ref[hd] for hd in heads]
            oo = {}
            for u in subs:
                g_end = ge_ref[u * c:u * c + 1, :]
                for hd in heads:
                    oo[u, hd] = oi[u, hd] + _mm_nt(qg_ref[rows[u], sl[hd]], st[hd])
                    st[hd] = st[hd] * g_end[:, sl[hd]] + kv[u, hd]
                if u == g0 + group // 2 - 1:
                    yield
            for hd in heads:
                st_ref[hd] = st[hd]
            yield
            for u, hd in pairs:
                o = oo[u, hd]
                o = o * lax.rsqrt(jnp.mean(o * o, axis=-1, keepdims=True) + EPS) * gnw * z_ref[rows[u], sl[hd]]
                o_ref[0, pl.ds(base + u * c, c), sl[hd]] = o.astype(o_ref.dtype)
            yield

    _alternate(prepare(0, sets[0]))

    def body(i, carry):
        _alternate(recur(2 * i, sets[0]), prepare(2 * i + 1, sets[1]))
        _alternate(recur(2 * i + 1, sets[1]), prepare(jnp.minimum(2 * i + 2, n_units - 1), sets[0]))
        return carry

    lax.fori_loop(0, n_units // 2, body, 0)


def _hgrn_call(x, mod, nw, w, lbp, gnw, ts, layer):
    bsz, s, d = x.shape
    kw = HGRN_HEADS * HGRN_DK
    group = 8
    unit = min(256, ts // 2)
    one_set = ([pltpu.VMEM((unit, kw), BF16) for _ in range(5)]
               + [pltpu.VMEM((unit, kw), F32) for _ in range(2)])
    return pl.pallas_call(
        functools.partial(_hgrn_kernel, ts=ts, layer=layer, unit=unit, group=group),
        grid=(bsz, s // ts),
        in_specs=[
            pl.BlockSpec((1, ts, d), lambda b, j: (b, j, 0)),
            pl.BlockSpec((1, 6, d), lambda b, j: (b, 0, 0)),
            _const_spec(nw.shape), _const_spec(w.shape), _const_spec(lbp.shape),
            _const_spec(gnw.shape),
        ],
        out_specs=pl.BlockSpec((1, ts, kw), lambda b, j: (b, j, 0)),
        out_shape=jax.ShapeDtypeStruct((bsz, s, kw), BF16),
        scratch_shapes=one_set + one_set + [pltpu.VMEM((HGRN_HEADS, HGRN_DK, HGRN_DK), F32)],
        compiler_params=_cparams(("parallel", "arbitrary")),
    )(x, mod, nw, w, lbp, gnw)


def _ssd_kernel(x_ref, mod_ref, nw_ref, w_ref, wdt_ref, wdtt_ref, cw_ref, cb_ref, pcol_ref,
                prow_ref, dsk_ref, gnw_ref, o_ref, ext_ref, y_ref, st_ref, *, ts):
    inner = SSD_HEADS * SSD_HEAD_DIM
    xbc_w = inner + 2 * SSD_GROUPS * SSD_STATE
    hpg = SSD_HEADS // SSD_GROUPS
    p = SSD_HEAD_DIM

    @pl.when(pl.program_id(1) == 0)
    def _():
        st_ref[...] = jnp.zeros_like(st_ref)
        ext_ref[0:HALO, :] = jnp.zeros((HALO, xbc_w), F32)

    m = mod_ref[0]
    h = _mod_norm(x_ref[0], nw_ref[...], m[0:1], m[1:2]).astype(BF16)
    proj = jnp.dot(h, w_ref[...], preferred_element_type=F32)
    z = proj[:, :inner]
    ext_ref[HALO:HALO + ts, :] = proj[:, inner:]
    cw = cw_ref[...]
    acc = cw[SSD_CONV - 1:SSD_CONV] * proj[:, inner:] + cb_ref[...]
    for i in range(SSD_CONV - 1):
        off = HALO - (SSD_CONV - 1) + i
        acc = acc + cw[i:i + 1] * ext_ref[off:off + ts, :]
    ext_ref[0:HALO, :] = ext_ref[ts:ts + HALO, :]
    xbc = _silu(acc)
    xs = xbc[:, :inner]

    pcol = pcol_ref[...]
    prow = prow_ref[...]
    dt_cols = _softplus(jnp.dot(h, wdt_ref[...], preferred_element_type=F32) + pcol[1:2])
    dt_rows = _softplus(lax.dot_general(wdtt_ref[...], h, (((1,), (1,)), ((), ())),
                                        preferred_element_type=F32) + prow[:, 1:2])
    incl = _tri(ts)
    low = jnp.where(incl, 1.0, 0.0).astype(F32)
    upp = jnp.where(lax.broadcasted_iota(jnp.int32, (ts, ts), 0)
                    <= lax.broadcasted_iota(jnp.int32, (ts, ts), 1), 1.0, 0.0).astype(F32)
    acs_cols = _mm_hi(low, dt_cols * (-jnp.exp(pcol[0:1])))
    acs_rows = _mm_hi(dt_rows * (-jnp.exp(prow[:, 0:1])), upp)

    first = lax.broadcasted_iota(jnp.int32, (ts, 2 * p), 1) < p
    first_rows = lax.broadcasted_iota(jnp.int32, (2 * p, SSD_STATE), 0) < p
    for g in range(SSD_GROUPS):
        bm = xbc[:, inner + g * SSD_STATE:inner + (g + 1) * SSD_STATE]
        cm = xbc[:, inner + (SSD_GROUPS + g) * SSD_STATE:inner + (SSD_GROUPS + g + 1) * SSD_STATE]
        cb = _mm_nt(cm, bm)
        for pr in range(hpg // 2):
            h0 = g * hpg + 2 * pr
            h1 = h0 + 1
            a0, a1 = acs_cols[:, h0:h0 + 1], acs_cols[:, h1:h1 + 1]
            seg0 = jnp.where(incl, jnp.exp(a0 - acs_rows[h0:h0 + 1, :]), 0.0)
            seg1 = jnp.where(incl, jnp.exp(a1 - acs_rows[h1:h1 + 1, :]), 0.0)
            xdt = xs[:, h0 * p:(h1 + 1) * p] * jnp.where(first, dt_cols[:, h0:h0 + 1], dt_cols[:, h1:h1 + 1])
            st = st_ref[g * (hpg // 2) + pr]
            y = (_mm(cb * seg0, jnp.where(first, xdt, 0.0)) + _mm(cb * seg1, jnp.where(first, 0.0, xdt))
                 + _mm_nt(cm, st) * jnp.where(first, jnp.exp(a0), jnp.exp(a1)))
            last0, last1 = a0[ts - 1:ts, :], a1[ts - 1:ts, :]
            w_in = jnp.where(first, jnp.exp(last0 - a0), jnp.exp(last1 - a1))
            st_ref[g * (hpg // 2) + pr] = (st * jnp.where(first_rows, jnp.exp(last0), jnp.exp(last1))
                                           + _mm_tn(xdt * w_in, bm))
            y_ref[:, h0 * p:(h1 + 1) * p] = y

    y = (y_ref[...] + dsk_ref[...] * xs) * _silu(z)
    gnw = gnw_ref[...]
    gw = hpg * p
    for g in range(SSD_GROUPS):
        yg = y[:, g * gw:(g + 1) * gw]
        yg = yg * lax.rsqrt(jnp.mean(yg * yg, axis=-1, keepdims=True) + EPS) * gnw[:, g * gw:(g + 1) * gw]
        o_ref[0, :, g * gw:(g + 1) * gw] = yg.astype(o_ref.dtype)


def _ssd_call(x, mod, nw, w, wdt, wdtt, cw, cb, pcol, prow, dsk, gnw, ts):
    bsz, s, d = x.shape
    inner = SSD_HEADS * SSD_HEAD_DIM
    xbc_w = inner + 2 * SSD_GROUPS * SSD_STATE
    return pl.pallas_call(
        functools.partial(_ssd_kernel, ts=ts),
        grid=(bsz, s // ts),
        in_specs=[
            pl.BlockSpec((1, ts, d), lambda b, j: (b, j, 0)),
            pl.BlockSpec((1, 6, d), lambda b, j: (b, 0, 0)),
            _const_spec(nw.shape), _const_spec(w.shape), _const_spec(wdt.shape),
            _const_spec(wdtt.shape), _const_spec(cw.shape), _const_spec(cb.shape),
            _const_spec(pcol.shape), _const_spec(prow.shape), _const_spec(dsk.shape),
            _const_spec(gnw.shape),
        ],
        out_specs=pl.BlockSpec((1, ts, inner), lambda b, j: (b, j, 0)),
        out_shape=jax.ShapeDtypeStruct((bsz, s, inner), BF16),
        scratch_shapes=[pltpu.VMEM((ts + HALO, xbc_w), F32),
                        pltpu.VMEM((ts, inner), F32),
                        pltpu.VMEM((SSD_HEADS // 2, 2 * SSD_HEAD_DIM, SSD_STATE), F32)],
        compiler_params=_cparams(("parallel", "arbitrary")),
    )(x, mod, nw, w, wdt, wdtt, cw, cb, pcol, prow, dsk, gnw)


def _merge_kernel(x_ref, mod_ref, nw_ref, oa_ref, ob_ref, oc_ref, wg_ref, wa_ref, wb_ref, wc_ref,
                  wo_ref, o_ref):
    d = x_ref.shape[-1]
    x = x_ref[0]
    m = mod_ref[0]
    h = _mod_norm(x, nw_ref[...], m[0:1], m[1:2]).astype(BF16)
    gates = _sigmoid(jnp.dot(h, wg_ref[...], preferred_element_type=F32))
    merged = (gates[:, :d] * jnp.dot(oa_ref[0], wa_ref[...], preferred_element_type=F32)
              + gates[:, d:2 * d] * jnp.dot(ob_ref[0], wb_ref[...], preferred_element_type=F32)
              + gates[:, 2 * d:] * jnp.dot(oc_ref[0], wc_ref[...], preferred_element_type=F32))
    mix = jnp.dot(merged.astype(BF16), wo_ref[...], preferred_element_type=F32)
    o_ref[0] = x + m[2:3] * mix


def _merge_call(x, mod, nw, oa, ob, oc, wg, wa, wb, wc, wo, tm):
    bsz, s, d = x.shape
    tok = lambda w: pl.BlockSpec((1, tm, w), lambda b, j: (b, j, 0))
    return pl.pallas_call(
        _merge_kernel,
        grid=(bsz, s // tm),
        in_specs=[
            tok(d), pl.BlockSpec((1, 6, d), lambda b, j: (b, 0, 0)), _const_spec(nw.shape),
            tok(oa.shape[-1]), tok(ob.shape[-1]), tok(oc.shape[-1]),
            _const_spec(wg.shape), _const_spec(wa.shape), _const_spec(wb.shape),
            _const_spec(wc.shape), _const_spec(wo.shape),
        ],
        out_specs=tok(d),
        out_shape=jax.ShapeDtypeStruct((bsz, s, d), F32),
        compiler_params=_cparams(("parallel", "parallel")),
    )(x, mod, nw, oa, ob, oc, wg, wa, wb, wc, wo)


def _ffn_kernel(x_ref, mod_ref, nw_ref, wup_ref, cw_ref, cb_ref, wdn_ref, fnw_ref, o_ref,
                ext_ref, tail_ref, *, tm, splits, final):
    hidden = wdn_ref.shape[0]

    @pl.when(pl.program_id(1) == 0)
    def _():
        tail_ref[...] = jnp.zeros_like(tail_ref)

    x = x_ref[0]
    m = mod_ref[0]
    h = _mod_norm(x, nw_ref[...], m[3:4], m[4:5]).astype(BF16)

    def conv_part(c0, fc):
        u = jnp.dot(h, wup_ref[:, c0:c0 + fc], preferred_element_type=F32)
        ext_ref[0:HALO, 0:fc] = tail_ref[:, c0:c0 + fc]
        ext_ref[HALO:HALO + tm, 0:fc] = u
        cw = cw_ref[:, c0:c0 + fc]
        acc = cw[FFN_CONV - 1:FFN_CONV] * u + cb_ref[:, c0:c0 + fc]
        for i in range(FFN_CONV - 1):
            off = HALO - (FFN_CONV - 1) + i
            acc = acc + cw[i:i + 1] * ext_ref[off:off + tm, 0:fc]
        tail_ref[:, c0:c0 + fc] = ext_ref[tm:tm + HALO, 0:fc]
        return acc

    y = jnp.zeros_like(x)
    for c0, fc in splits:
        gate = conv_part(c0, fc)
        val = conv_part(hidden + c0, fc)
        act = (_silu(gate) * val).astype(BF16)
        y = y + jnp.dot(act, wdn_ref[c0:c0 + fc, :], preferred_element_type=F32)
    out = x + m[5:6] * y
    if final:
        out = out * lax.rsqrt(jnp.mean(out * out, axis=-1, keepdims=True) + EPS) * fnw_ref[...]
    o_ref[0] = out


def _ffn_call(x, mod, nw, wup, cw, cb, wdn, fnw, tm, final):
    bsz, s, d = x.shape
    hidden = wdn.shape[0]
    first = -(-(hidden // MXU_TILE) // 2) * MXU_TILE
    splits = ((0, first), (first, hidden - first))
    tok = pl.BlockSpec((1, tm, d), lambda b, j: (b, j, 0))
    single = lambda a: pl.BlockSpec(a.shape, lambda *_: (0,) * a.ndim, pipeline_mode=pl.Buffered(1))
    return pl.pallas_call(
        functools.partial(_ffn_kernel, tm=tm, splits=splits, final=final),
        grid=(bsz, s // tm),
        in_specs=[
            tok, pl.BlockSpec((1, 6, d), lambda b, j: (b, 0, 0)), _const_spec(nw.shape),
            single(wup), _const_spec(cw.shape), _const_spec(cb.shape), single(wdn),
            _const_spec(fnw.shape),
        ],
        out_specs=tok,
        out_shape=jax.ShapeDtypeStruct((bsz, s, d), F32),
        scratch_shapes=[pltpu.VMEM((tm + HALO, first), F32),
                        pltpu.VMEM((HALO, 2 * hidden), F32)],
        compiler_params=_cparams(("parallel", "arbitrary")),
    )(x, mod, nw, wup, cw, cb, wdn, fnw)


def _pad_lanes(a, width=128):
    return jnp.pad(a, ((0, 0), (0, width - a.shape[-1])))


def kernel(x, c, w_ada, b_ada, norm1_w, w_in, gdn_conv_w, gdn_a_log, gdn_dt_bias, gdn_norm_w,
           hgrn_lb_param, hgrn_norm_w, ssd_conv_w, ssd_conv_b, ssd_a_log, ssd_dt_bias, ssd_d,
           ssd_norm_w, w_br_a, w_br_b, w_br_c, w_out, norm2_w, ffn_w_up, ffn_conv_w, ffn_conv_b,
           ffn_w_down, final_norm_w):
    bsz, s, d = x.shape
    depth = w_in.shape[0]
    gk = GDN_HEADS * GDN_DK
    gv = GDN_HEADS * GDN_DV
    hk = HGRN_HEADS * HGRN_DK
    inner = SSD_HEADS * SSD_HEAD_DIM
    xbc_w = inner + 2 * SSD_GROUPS * SSD_STATE

    sizes = (2 * gk + gv, GDN_HEADS, GDN_HEADS, gv, hk, hk, hk, hk, inner, xbc_w, SSD_HEADS, 3 * d)
    offs = [0]
    for sz in sizes:
        offs.append(offs[-1] + sz)
    (o_qkv, o_a, o_b, o_z, o_hq, _, _, _, o_sz, o_xbc, o_dt, o_gate, o_end) = offs

    nb_gdn = 4 if bsz % 4 == 0 else (2 if bsz % 2 == 0 else 1)
    rows_gdn = min(max(GDN_CHUNK, 256 // nb_gdn), s)
    ts_hgrn = min(1024, s)
    ts_ssd = min(256, s)
    tm_merge = min(512, s)
    tm_ffn = min(512, s)

    mod = _ada_call(c, w_ada, b_ada).reshape(depth, bsz, 6, d)
    fnw = final_norm_w.reshape(1, d)

    for l in range(depth):
        wl = w_in[l]
        mod_l = mod[l]
        nw1 = norm1_w[l].reshape(1, d)

        w_gdn = jnp.concatenate([wl[:, o_qkv:o_a], wl[:, o_z:o_hq]], axis=1).astype(BF16)
        w_ab = wl[:, o_a:o_z]
        o_ga = _gdn_call(
            x, mod_l, nw1, w_gdn, _pad_lanes(w_ab).astype(BF16), w_ab.T.astype(BF16),
            gdn_conv_w[l],
            _pad_lanes(jnp.stack([gdn_a_log[l], gdn_dt_bias[l]])),
            jnp.pad(jnp.stack([gdn_a_log[l], gdn_dt_bias[l]], axis=1), ((0, GDN_HEADS), (0, 0))),
            gdn_norm_w[l].reshape(1, GDN_DV), nb_gdn, rows_gdn)

        o_hg = _hgrn_call(x, mod_l, nw1, wl[:, o_hq:o_sz].astype(BF16), hgrn_lb_param,
                          hgrn_norm_w[l].reshape(1, HGRN_DK), ts_hgrn, l)

        w_dt = wl[:, o_dt:o_gate]
        o_sd = _ssd_call(
            x, mod_l, nw1, wl[:, o_sz:o_dt].astype(BF16), _pad_lanes(w_dt).astype(BF16),
            w_dt.T.astype(BF16), ssd_conv_w[l], ssd_conv_b[l].reshape(1, xbc_w),
            _pad_lanes(jnp.stack([ssd_a_log[l], ssd_dt_bias[l]])),
            jnp.stack([ssd_a_log[l], ssd_dt_bias[l]], axis=1),
            jnp.repeat(ssd_d[l], SSD_HEAD_DIM).reshape(1, inner),
            ssd_norm_w[l].reshape(1, inner), ts_ssd)

        x = _merge_call(x, mod_l, nw1, o_ga, o_hg, o_sd, wl[:, o_gate:o_end].astype(BF16),
                        w_br_a[l].astype(BF16), w_br_b[l].astype(BF16), w_br_c[l].astype(BF16),
                        w_out[l].astype(BF16), tm_merge)

        x = _ffn_call(x, mod_l, norm2_w[l].reshape(1, d), ffn_w_up[l].astype(BF16), ffn_conv_w[l],
                      ffn_conv_b[l].reshape(1, -1), ffn_w_down[l].astype(BF16), fnw, tm_ffn,
                      final=(l == depth - 1))
    return x
```

```python
import functools

import jax
import jax.numpy as jnp
from jax import lax
from jax.experimental import pallas as pl
from jax.experimental.pallas import tpu as pltpu

F32 = jnp.float32
BF16 = jnp.bfloat16
EPS = 1e-6

GDN_HEADS = 4
GDN_DK = 128
GDN_DV = 128
GDN_CONV = 4
GDN_CHUNK = 64
GDN_SOLVE_BASE = 8
HGRN_HEADS = 4
HGRN_DK = 128
HGRN_CHUNK = 16
SSD_HEADS = 8
SSD_HEAD_DIM = 64
SSD_GROUPS = 2
SSD_STATE = 128
SSD_CONV = 4
FFN_CONV = 3

HALO = 8
MXU_TILE = 256
VMEM_LIMIT = 56 * 1024 * 1024

_HI = lax.Precision.HIGHEST


def _mm(a, b):
    return jnp.dot(a.astype(BF16), b.astype(BF16), preferred_element_type=F32)


def _mm_nt(a, b):
    return lax.dot_general(a.astype(BF16), b.astype(BF16), (((1,), (1,)), ((), ())),
                           preferred_element_type=F32)


def _mm_tn(a, b):
    return lax.dot_general(a.astype(BF16), b.astype(BF16), (((0,), (0,)), ((), ())),
                           preferred_element_type=F32)


def _mm_hi(a, b):
    return jnp.dot(a, b, precision=_HI, preferred_element_type=F32)


def _sigmoid(x):
    return jax.nn.sigmoid(x)


def _silu(x):
    return x * jax.nn.sigmoid(x)


def _softplus(x):
    return jnp.maximum(x, 0.0) + jnp.log(1.0 + jnp.exp(-jnp.abs(x)))


def _mod_norm(x, nw, shift, scale):
    y = x * lax.rsqrt(jnp.mean(x * x, axis=-1, keepdims=True) + EPS) * nw
    return y * (1.0 + scale) + shift


def _tri(n, strict=False):
    r = lax.broadcasted_iota(jnp.int32, (n, n), 0)
    c = lax.broadcasted_iota(jnp.int32, (n, n), 1)
    return (r > c) if strict else (r >= c)


def _cparams(sem):
    return pltpu.CompilerParams(dimension_semantics=sem, vmem_limit_bytes=VMEM_LIMIT)


def _const_spec(shape):
    nd = len(shape)
    return pl.BlockSpec(shape, lambda *_: (0,) * nd)


def _alternate(*gens):
    live = list(gens)
    while live:
        for gen in list(live):
            if next(gen, live) is live:
                live.remove(gen)


def _ada_kernel(c_ref, w_ref, b_ref, o_ref):
    c = c_ref[...]
    o_ref[0] = _mm_hi(_silu(c), w_ref[0]) + b_ref[0]


def _ada_call(c, w_ada, b_ada):
    depth, d, n = w_ada.shape
    bsz = c.shape[0]
    tn = 1536
    return pl.pallas_call(
        _ada_kernel,
        grid=(depth, n // tn),
        in_specs=[
            pl.BlockSpec((bsz, d), lambda l, j: (0, 0)),
            pl.BlockSpec((1, d, tn), lambda l, j: (l, 0, j)),
            pl.BlockSpec((1, 1, tn), lambda l, j: (l, 0, j)),
        ],
        out_specs=pl.BlockSpec((1, bsz, tn), lambda l, j: (l, 0, j)),
        out_shape=jax.ShapeDtypeStruct((depth, bsz, n), F32),
        compiler_params=_cparams(("arbitrary", "arbitrary")),
    )(c, w_ada, b_ada.reshape(depth, 1, n))


def _split3(a):
    hi = a.astype(BF16)
    lo = (a - hi.astype(F32)).astype(BF16)
    return hi, lo


def _dot3(a_hi, a_lo, b_hi, b_lo):
    f = lambda p, q: jnp.dot(p, q, preferred_element_type=F32)
    return f(a_hi, b_hi) + (f(a_lo, b_hi) + f(a_hi, b_lo))


def _unit_lower_inverse(ms, bd_mask, base):
    c = ms[0].shape[0]
    nb = ms[0].shape[1] // c
    f = lambda p, q: jnp.dot(p, q, preferred_element_type=F32)
    row = lax.broadcasted_iota(jnp.int32, (c, nb * c), 0)
    col = lax.broadcasted_iota(jnp.int32, (c, nb * c), 1) % c
    eye = jnp.where(row == col, 1.0, 0.0).astype(F32)

    def blockdiag(x):
        return jnp.concatenate([x] * nb, axis=0) * bd_mask

    def dot3(a_hi, a_lo, b_hi, b_lo):
        r = a_hi.shape[0]
        t = f(jnp.concatenate([a_hi, a_lo], axis=0), blockdiag(b_hi))
        return t[:r] + (t[r:] + f(a_hi, blockdiag(b_lo)))

    n_items = range(len(ms))
    n_steps = base.bit_length() - 1
    in_base = (row // base) == (col // base)
    ds = [jnp.where(in_base, m, 0.0) for m in ms]
    xs = [eye - d for d in ds]
    if n_steps > 1:
        qs = []
        for d in ds:
            hi, lo = _split3(d)
            qs.append(dot3(hi, lo, hi, lo))
        for step in range(1, n_steps):
            for i in n_items:
                if step == n_steps - 1:
                    p_hi, p_lo = _split3(xs[i])
                    q_hi, q_lo = _split3(qs[i])
                    xs[i] = xs[i] + dot3(p_hi, p_lo, q_hi, q_lo)
                else:
                    hi, lo = _split3(jnp.concatenate([qs[i], xs[i]], axis=0))
                    r = dot3(hi, lo, hi[:c], lo[:c])
                    qs[i] = r[:c]
                    xs[i] = xs[i] + r[c:]
            yield
    b = base
    while b < c:
        level = ((row // (2 * b)) == (col // (2 * b))) & ((row // b) != (col // b))
        splits = [_split3(x) for x in xs]
        ts = []
        for i in n_items:
            l_hi, l_lo = _split3(jnp.where(level, ms[i], 0.0))
            ts.append(dot3(splits[i][0], splits[i][1], l_hi, l_lo))
        yield
        for i in n_items:
            t_hi, t_lo = _split3(ts[i])
            xs[i] = xs[i] - dot3(t_hi, t_lo, splits[i][0], splits[i][1])
        yield
        b *= 2
    return xs


def _gdn_kernel(x0_ref, xa_ref, xb_ref, mod_ref, nw_ref, w_ref, wsm_ref, wsmt_ref, cw_ref,
                pcol_ref, prow_ref, gnw_ref, o_ref, ext_ref, st_ref, *sets, nb, rows):
    kw = GDN_HEADS * GDN_DK
    vw = GDN_HEADS * GDN_DV
    qkv_w = 2 * kw + vw
    c = GDN_CHUNK
    ts = nb * rows
    n_chunks = ts // c
    heads = range(GDN_HEADS)
    hc = GDN_HEADS * c
    set_a, set_b = sets[:7], sets[7:]
    wd = GDN_DV + GDN_DK

    m = mod_ref[...]
    nw = nw_ref[...]
    gnw = gnw_ref[...]
    row_i = lax.broadcasted_iota(jnp.int32, (c, hc), 0)
    lane_i = lax.broadcasted_iota(jnp.int32, (c, hc), 1)
    lane_head = lane_i // c
    col_i = lane_i - lane_head * c
    incl_bd = row_i >= col_i
    strict_bd = row_i > col_i

    def prepare(x_ref, dst):
        qg_s, kn_s, rhs_s, z_s, gcol_s, m_s, qk_s = dst
        h = _mod_norm(x_ref[...], nw, m[:, 0:1, :], m[:, 1:2, :])
        h = h.reshape(ts, h.shape[-1]).astype(BF16)
        yield
        proj = jnp.dot(h, w_ref[...], preferred_element_type=F32)
        z_s[...] = _silu(proj[:, qkv_w:])
        yield
        cw = cw_ref[...]
        accs = []
        for i in range(nb):
            p_i = proj[i * rows:(i + 1) * rows, :qkv_w]
            ext_ref[i, HALO:HALO + rows, :] = p_i
            acc = cw[GDN_CONV - 1:GDN_CONV] * p_i
            for t in range(GDN_CONV - 1):
                off = HALO - (GDN_CONV - 1) + t
                acc = acc + cw[t:t + 1] * ext_ref[i, off:off + rows, :]
            ext_ref[i, 0:HALO, :] = ext_ref[i, rows:rows + HALO, :]
            accs.append(acc)
            yield
        qkv = _silu(jnp.concatenate(accs, axis=0))

        sm = jnp.dot(h, wsm_ref[...], preferred_element_type=F32)
        smt = lax.dot_general(wsmt_ref[...], h, (((1,), (1,)), ((), ())), preferred_element_type=F32)
        pcol = pcol_ref[...]
        prow = prow_ref[...]
        g_cols = -jnp.exp(pcol[0:1]) * _softplus(sm + pcol[1:2])
        beta_cols = _sigmoid(sm)
        g_rows = -jnp.exp(prow[:, 0:1]) * _softplus(smt + prow[:, 1:2])
        rr = lax.broadcasted_iota(jnp.int32, (ts, ts), 0)
        cc = lax.broadcasted_iota(jnp.int32, (ts, ts), 1)
        same = (rr // c) == (cc // c)
        gc_all = _mm_hi(jnp.where(same & (rr >= cc), 1.0, 0.0).astype(F32), g_cols)
        gr_all = _mm_hi(g_rows, jnp.where(same & (rr <= cc), 1.0, 0.0).astype(F32))
        gcol_s[...] = gc_all
        yield
        for ci in range(n_chunks):
            r0 = ci * c
            g_col_bd = jnp.zeros((c, hc), F32)
            kks, qks = [], []
            for hd in heads:
                q = qkv[r0:r0 + c, hd * GDN_DK:(hd + 1) * GDN_DK]
                k = qkv[r0:r0 + c, kw + hd * GDN_DK:kw + (hd + 1) * GDN_DK]
                v = qkv[r0:r0 + c, 2 * kw + hd * GDN_DV:2 * kw + (hd + 1) * GDN_DV]
                q = q * lax.rsqrt(jnp.sum(q * q, axis=-1, keepdims=True) + EPS) * (GDN_DK ** -0.5)
                k = k * lax.rsqrt(jnp.sum(k * k, axis=-1, keepdims=True) + EPS)
                g_col = gc_all[r0:r0 + c, hd:hd + 1]
                beta = beta_cols[r0:r0 + c, GDN_HEADS + hd:GDN_HEADS + hd + 1]
                eg = jnp.exp(g_col)
                kb = k * beta
                qg_s[r0:r0 + c, hd * GDN_DK:(hd + 1) * GDN_DK] = q * eg
                kn_s[r0:r0 + c, hd * GDN_DK:(hd + 1) * GDN_DK] = k
                rhs_s[r0:r0 + c, hd * wd:hd * wd + GDN_DV] = v * beta
                rhs_s[r0:r0 + c, hd * wd + GDN_DV:(hd + 1) * wd] = kb * eg
                kks.append(_mm_nt(kb, k))
                qks.append(_mm_nt(q, k))
                g_col_bd = jnp.where(lane_head == hd, g_col, g_col_bd)
            g_row_bd = jnp.concatenate([gr_all[hd:hd + 1, r0:r0 + c] for hd in heads], axis=1)
            decay = jnp.where(incl_bd, jnp.exp(g_col_bd - g_row_bd), 0.0)
            m_s[r0:r0 + c, :] = jnp.where(strict_bd, jnp.concatenate(kks, axis=1) * decay, 0.0)
            qk_s[r0:r0 + c, :] = jnp.concatenate(qks, axis=1) * decay
            yield

    def finish(src, out_rows):
        qg_s, kn_s, rhs_s, z_s, gcol_s, m_s, qk_s = src
        bd_mask = jnp.where(lax.broadcasted_iota(jnp.int32, (hc, hc), 0) // c
                            == lax.broadcasted_iota(jnp.int32, (hc, hc), 1) // c, 1.0, 0.0).astype(BF16)
        ms = [m_s[ci * c:(ci + 1) * c, :] for ci in range(n_chunks)]
        invs = yield from _unit_lower_inverse(ms, bd_mask, GDN_SOLVE_BASE)
        pairs = [(ci, hd) for ci in range(n_chunks) for hd in heads]
        uw = {}
        for ci, hd in pairs:
            r0 = ci * c
            rhs_hi, rhs_lo = _split3(rhs_s[r0:r0 + c, hd * wd:(hd + 1) * wd])
            inv_hi, inv_lo = _split3(invs[ci][:, hd * c:(hd + 1) * c])
            uw[ci, hd] = _dot3(inv_hi, inv_lo, rhs_hi, rhs_lo)
            if hd == GDN_HEADS - 1:
                yield
        cps = rows // c
        items = [(i, hd) for i in range(nb) for hd in heads]
        for j in range(cps):
            ci = {i: i * cps + j for i in range(nb)}
            hsl = {hd: slice(hd * GDN_DK, (hd + 1) * GDN_DK) for hd in heads}
            st = {(i, hd): st_ref[i, hd] for i, hd in items}
            ws = {(i, hd): _mm(jnp.concatenate([uw[ci[i], hd][:, GDN_DV:],
                                                qg_s[ci[i] * c:(ci[i] + 1) * c, hsl[hd]]], axis=0), st[i, hd])
                  for i, hd in items}
            yield
            v_new = {(i, hd): uw[ci[i], hd][:, :GDN_DV] - ws[i, hd][:c] for i, hd in items}
            for i, hd in items:
                r0 = ci[i] * c
                g_col = gcol_s[r0:r0 + c, hd:hd + 1]
                g_end = g_col[c - 1:c, :]
                k_end = kn_s[r0:r0 + c, hsl[hd]] * jnp.exp(g_end - g_col)
                st_ref[i, hd] = st[i, hd] * jnp.exp(g_end) + _mm_tn(k_end, v_new[i, hd])
            yield
            for i, hd in items:
                r0 = ci[i] * c
                o = ws[i, hd][c:] + _mm(qk_s[r0:r0 + c, hd * c:(hd + 1) * c], v_new[i, hd])
                o = o * lax.rsqrt(jnp.mean(o * o, axis=-1, keepdims=True) + EPS) * gnw * z_s[r0:r0 + c, hsl[hd]]
                o_ref[i, out_rows + j * c:out_rows + (j + 1) * c, hsl[hd]] = o.astype(o_ref.dtype)
                if hd == GDN_HEADS - 1:
                    yield

    @pl.when(pl.program_id(1) == 0)
    def _():
        st_ref[...] = jnp.zeros_like(st_ref)
        ext_ref[:, 0:HALO, :] = jnp.zeros((nb, HALO, qkv_w), F32)
        _alternate(prepare(x0_ref, set_a))

    _alternate(finish(set_a, 0), prepare(xa_ref, set_b))
    _alternate(finish(set_b, rows), prepare(xb_ref, set_a))


def _gdn_call(x, mod, nw, w, wsm, wsmt, cw, pcol, prow, gnw, nb, rows):
    bsz, s, d = x.shape
    kw = GDN_HEADS * GDN_DK
    vw = GDN_HEADS * GDN_DV
    qkv_w = 2 * kw + vw
    ts = nb * rows
    n_tiles = s // rows
    hc = GDN_HEADS * GDN_CHUNK
    one_set = [pltpu.VMEM((ts, kw), F32), pltpu.VMEM((ts, kw), F32),
               pltpu.VMEM((ts, vw + kw), F32), pltpu.VMEM((ts, vw), F32),
               pltpu.VMEM((ts, 128), F32), pltpu.VMEM((ts, hc), F32), pltpu.VMEM((ts, hc), F32)]
    tile = lambda f: pl.BlockSpec((nb, rows, d), f)
    return pl.pallas_call(
        functools.partial(_gdn_kernel, nb=nb, rows=rows),
        grid=(bsz // nb, n_tiles // 2),
        in_specs=[
            tile(lambda b, j: (b, 0, 0)),
            tile(lambda b, j: (b, 2 * j + 1, 0)),
            tile(lambda b, j: (b, jnp.minimum(2 * j + 2, n_tiles - 1), 0)),
            pl.BlockSpec((nb, 6, d), lambda b, j: (b, 0, 0)),
            _const_spec(nw.shape), _const_spec(w.shape), _const_spec(wsm.shape),
            _const_spec(wsmt.shape), _const_spec(cw.shape), _const_spec(pcol.shape),
            _const_spec(prow.shape), _const_spec(gnw.shape),
        ],
        out_specs=pl.BlockSpec((nb, 2 * rows, vw), lambda b, j: (b, j, 0)),
        out_shape=jax.ShapeDtypeStruct((bsz, s, vw), BF16),
        scratch_shapes=[pltpu.VMEM((nb, rows + HALO, qkv_w), F32),
                        pltpu.VMEM((nb, GDN_HEADS, GDN_DK, GDN_DV), F32)] + one_set + one_set,
        compiler_params=_cparams(("parallel", "arbitrary")),
    )(x, x, x, mod, nw, w, wsm, wsmt, cw, pcol, prow, gnw)


def _group_cumsum_rows(x, group):
    rows = lax.broadcasted_iota(jnp.int32, x.shape, 0) % group
    d = 1
    while d < group:
        x = x + jnp.where(rows >= d, pltpu.roll(x, d, axis=0), 0.0)
        d *= 2
    return x


def _hgrn_kernel(x_ref, mod_ref, nw_ref, w_ref, lbp_ref, gnw_ref, o_ref, *scratch, ts, layer, unit, group):
    kw = HGRN_HEADS * HGRN_DK
    c = HGRN_CHUNK
    n_sub = unit // c
    n_units = ts // unit
    sets = (scratch[0:7], scratch[7:14])
    st_ref = scratch[14]

    @pl.when(pl.program_id(1) == 0)
    def _():
        st_ref[...] = jnp.zeros_like(st_ref)

    m = mod_ref[0]
    nw = nw_ref[...]
    gnw = gnw_ref[...]
    incl = _tri(c)
    heads = range(HGRN_HEADS)
    sl = [slice(hd * HGRN_DK, (hd + 1) * HGRN_DK) for hd in heads]

    lbp = lbp_ref[...]
    e = jnp.exp(lbp - jnp.max(lbp, axis=0, keepdims=True))
    soft = e / jnp.sum(e, axis=0, keepdims=True)
    lb = jnp.zeros((1, kw), F32)
    for i in range(1, layer + 1):
        lb = lb + soft[i:i + 1]

    def prepare(u_idx, dst):
        qs_ref, ks_ref, qg_ref, ke_ref, v_ref, ge_ref, z_ref = dst
        rows = pl.ds(pl.multiple_of(u_idx * unit, unit), unit)
        h = _mod_norm(x_ref[0, rows, :], nw, m[0:1], m[1:2]).astype(BF16)
        yield
        f_raw = jnp.dot(h, w_ref[:, kw:2 * kw], preferred_element_type=F32)
        sig = _sigmoid(f_raw)
        g = _group_cumsum_rows(jnp.log(lb + (1.0 - lb) * sig), c).reshape(n_sub, c, kw)
        g_mid = g[:, c // 2 - 1:c // 2, :]
        g_last = g[:, c - 1:c, :]
        ge_ref[...] = jnp.broadcast_to(jnp.exp(g_last), (n_sub, c, kw)).reshape(unit, kw)
        yield
        k = ((1.0 - lb) * _sigmoid(-f_raw)).reshape(n_sub, c, kw)
        ks_ref[...] = (k * jnp.exp(g_mid - g)).reshape(unit, kw).astype(BF16)
        ke_ref[...] = (k * jnp.exp(g_last - g)).reshape(unit, kw).astype(BF16)
        yield
        q = _silu(jnp.dot(h, w_ref[:, :kw], preferred_element_type=F32)).reshape(n_sub, c, kw)
        qs_ref[...] = (q * jnp.exp(g - g_mid)).reshape(unit, kw).astype(BF16)
        qg_ref[...] = (q * jnp.exp(g)).reshape(unit, kw).astype(BF16)
        yield
        v_ref[...] = jnp.dot(h, w_ref[:, 2 * kw:3 * kw], preferred_element_type=F32).astype(BF16)
        yield
        z_ref[...] = _silu(jnp.dot(h, w_ref[:, 3 * kw:], preferred_element_type=F32))

    def recur(u_idx, src):
        qs_ref, ks_ref, qg_ref, ke_ref, v_ref, ge_ref, z_ref = src
        base = pl.multiple_of(u_idx * unit, unit)
        for g0 in range(0, n_sub, group):
            subs = range(g0, g0 + group)
            rows = {u: slice(u * c, (u + 1) * c) for u in subs}
            pairs = [(u, hd) for u in subs for hd in heads]
            vv = {u: v_ref[rows[u], :] for u in subs}
            sc = {(u, hd): _mm_nt(qs_ref[rows[u], sl[hd]], ks_ref[rows[u], sl[hd]]) for u, hd in pairs}
            yield
            kv = {(u, hd): _mm_tn(vv[u][:, sl[hd]], ke_ref[rows[u], sl[hd]]) for u, hd in pairs}
            yield
            oi = {(u, hd): _mm(jnp.where(incl, sc[u, hd], 0.0), vv[u][:, sl[hd]]) for u, hd in pairs}
            yield
            st = [st_ref[hd] for hd in heads]
            oo = {}
            for u in subs:
                g_end = ge_ref[u * c:u * c + 1, :]
                for hd in heads:
                    oo[u, hd] = oi[u, hd] + _mm_nt(qg_ref[rows[u], sl[hd]], st[hd])
                    st[hd] = st[hd] * g_end[:, sl[hd]] + kv[u, hd]
                if u == g0 + group // 2 - 1:
                    yield
            for hd in heads:
                st_ref[hd] = st[hd]
            yield
            for u, hd in pairs:
                o = oo[u, hd]
                o = o * lax.rsqrt(jnp.mean(o * o, axis=-1, keepdims=True) + EPS) * gnw * z_ref[rows[u], sl[hd]]
                o_ref[0, pl.ds(base + u * c, c), sl[hd]] = o.astype(o_ref.dtype)
            yield

    _alternate(prepare(0, sets[0]))

    def body(i, carry):
        _alternate(recur(2 * i, sets[0]), prepare(2 * i + 1, sets[1]))
        _alternate(recur(2 * i + 1, sets[1]), prepare(jnp.minimum(2 * i + 2, n_units - 1), sets[0]))
        return carry

    lax.fori_loop(0, n_units // 2, body, 0)


def _hgrn_call(x, mod, nw, w, lbp, gnw, ts, layer):
    bsz, s, d = x.shape
    kw = HGRN_HEADS * HGRN_DK
    group = 8
    unit = min(256, ts // 2)
    one_set = ([pltpu.VMEM((unit, kw), BF16) for _ in range(5)]
               + [pltpu.VMEM((unit, kw), F32) for _ in range(2)])
    return pl.pallas_call(
        functools.partial(_hgrn_kernel, ts=ts, layer=layer, unit=unit, group=group),
        grid=(bsz, s // ts),
        in_specs=[
            pl.BlockSpec((1, ts, d), lambda b, j: (b, j, 0)),
            pl.BlockSpec((1, 6, d), lambda b, j: (b, 0, 0)),
            _const_spec(nw.shape), _const_spec(w.shape), _const_spec(lbp.shape),
            _const_spec(gnw.shape),
        ],
        out_specs=pl.BlockSpec((1, ts, kw), lambda b, j: (b, j, 0)),
        out_shape=jax.ShapeDtypeStruct((bsz, s, kw), BF16),
        scratch_shapes=one_set + one_set + [pltpu.VMEM((HGRN_HEADS, HGRN_DK, HGRN_DK), F32)],
        compiler_params=_cparams(("parallel", "arbitrary")),
    )(x, mod, nw, w, lbp, gnw)


def _ssd_kernel(x0_ref, xa_ref, xb_ref, mod_ref, nw_ref, w_ref, wdt_ref, wdtt_ref, cw_ref, cb_ref,
                pcol_ref, prow_ref, dsk_ref, gnw_ref, o_ref, ext_ref, st_ref, *sets, ts):
    inner = SSD_HEADS * SSD_HEAD_DIM
    bc_w = 2 * SSD_GROUPS * SSD_STATE
    xbc_w = inner + bc_w
    hpg = SSD_HEADS // SSD_GROUPS
    p = SSD_HEAD_DIM
    set_a, set_b = sets[:7], sets[7:]

    m = mod_ref[0]
    nw = nw_ref[...]
    incl = _tri(ts)

    def prepare(x_ref, dst):
        xs_s, bc_s, zs_s, dtc_s, acc_s, acr_s, cb_s = dst
        h = _mod_norm(x_ref[0], nw, m[0:1], m[1:2]).astype(BF16)
        yield
        proj = jnp.dot(h, w_ref[...], preferred_element_type=F32)
        zs_s[...] = _silu(proj[:, :inner])
        yield
        ext_ref[HALO:HALO + ts, :] = proj[:, inner:]
        cw = cw_ref[...]
        acc = cw[SSD_CONV - 1:SSD_CONV] * proj[:, inner:] + cb_ref[...]
        for i in range(SSD_CONV - 1):
            off = HALO - (SSD_CONV - 1) + i
            acc = acc + cw[i:i + 1] * ext_ref[off:off + ts, :]
        ext_ref[0:HALO, :] = ext_ref[ts:ts + HALO, :]
        xbc = _silu(acc)
        xs_s[...] = xbc[:, :inner]
        bc_s[...] = xbc[:, inner:]
        yield
        pcol = pcol_ref[...]
        prow = prow_ref[...]
        dt_cols = _softplus(jnp.dot(h, wdt_ref[...], preferred_element_type=F32) + pcol[1:2])
        dt_rows = _softplus(lax.dot_general(wdtt_ref[...], h, (((1,), (1,)), ((), ())),
                                            preferred_element_type=F32) + prow[:, 1:2])
        low = jnp.where(incl, 1.0, 0.0).astype(F32)
        upp = jnp.where(lax.broadcasted_iota(jnp.int32, (ts, ts), 0)
                        <= lax.broadcasted_iota(jnp.int32, (ts, ts), 1), 1.0, 0.0).astype(F32)
        dtc_s[...] = dt_cols
        acc_s[...] = _mm_hi(low, dt_cols * (-jnp.exp(pcol[0:1])))
        acr_s[...] = _mm_hi(dt_rows * (-jnp.exp(prow[:, 0:1])), upp)
        yield
        for g in range(SSD_GROUPS):
            bm = xbc[:, inner + g * SSD_STATE:inner + (g + 1) * SSD_STATE]
            cm = xbc[:, inner + (SSD_GROUPS + g) * SSD_STATE:inner + (SSD_GROUPS + g + 1) * SSD_STATE]
            cb_s[g] = _mm_nt(cm, bm)
            yield

    def finish(src, out_rows):
        xs_s, bc_s, zs_s, dtc_s, acc_s, acr_s, cb_s = src
        first = lax.broadcasted_iota(jnp.int32, (ts, 2 * p), 1) < p
        first_rows = lax.broadcasted_iota(jnp.int32, (2 * p, SSD_STATE), 0) < p
        ys = []
        for g in range(SSD_GROUPS):
            bm = bc_s[:, g * SSD_STATE:(g + 1) * SSD_STATE]
            cm = bc_s[:, (SSD_GROUPS + g) * SSD_STATE:(SSD_GROUPS + g + 1) * SSD_STATE]
            cb = cb_s[g]
            for pr in range(hpg // 2):
                h0 = g * hpg + 2 * pr
                h1 = h0 + 1
                a0, a1 = acc_s[:, h0:h0 + 1], acc_s[:, h1:h1 + 1]
                seg0 = jnp.where(incl, jnp.exp(a0 - acr_s[h0:h0 + 1, :]), 0.0)
                seg1 = jnp.where(incl, jnp.exp(a1 - acr_s[h1:h1 + 1, :]), 0.0)
                xdt = xs_s[:, h0 * p:(h1 + 1) * p] * jnp.where(first, dtc_s[:, h0:h0 + 1], dtc_s[:, h1:h1 + 1])
                st = st_ref[g * (hpg // 2) + pr]
                y = (_mm(cb * seg0, jnp.where(first, xdt, 0.0)) + _mm(cb * seg1, jnp.where(first, 0.0, xdt))
                     + _mm_nt(cm, st) * jnp.where(first, jnp.exp(a0), jnp.exp(a1)))
                yield
                last0, last1 = a0[ts - 1:ts, :], a1[ts - 1:ts, :]
                w_in = jnp.where(first, jnp.exp(last0 - a0), jnp.exp(last1 - a1))
                st_ref[g * (hpg // 2) + pr] = (st * jnp.where(first_rows, jnp.exp(last0), jnp.exp(last1))
                                               + _mm_tn(xdt * w_in, bm))
                ys.append(y)
                yield
        gnw = gnw_ref[...]
        gw = hpg * p
        for g in range(SSD_GROUPS):
            cols = slice(g * gw, (g + 1) * gw)
            yg = jnp.concatenate(ys[g * (hpg // 2):(g + 1) * (hpg // 2)], axis=1)
            yg = (yg + dsk_ref[:, cols] * xs_s[:, cols]) * zs_s[:, cols]
            yg = yg * lax.rsqrt(jnp.mean(yg * yg, axis=-1, keepdims=True) + EPS) * gnw[:, cols]
            o_ref[0, out_rows:out_rows + ts, cols] = yg.astype(o_ref.dtype)
            yield

    @pl.when(pl.program_id(1) == 0)
    def _():
        st_ref[...] = jnp.zeros_like(st_ref)
        ext_ref[0:HALO, :] = jnp.zeros((HALO, xbc_w), F32)
        _alternate(prepare(x0_ref, set_a))

    _alternate(finish(set_a, 0), prepare(xa_ref, set_b))
    _alternate(finish(set_b, ts), prepare(xb_ref, set_a))


def _ssd_call(x, mod, nw, w, wdt, wdtt, cw, cb, pcol, prow, dsk, gnw, ts):
    bsz, s, d = x.shape
    inner = SSD_HEADS * SSD_HEAD_DIM
    bc_w = 2 * SSD_GROUPS * SSD_STATE
    xbc_w = inner + bc_w
    n_tiles = s // ts
    one_set = [pltpu.VMEM((ts, inner), F32), pltpu.VMEM((ts, bc_w), F32), pltpu.VMEM((ts, inner), F32),
               pltpu.VMEM((ts, 128), F32), pltpu.VMEM((ts, 128), F32), pltpu.VMEM((8, ts), F32),
               pltpu.VMEM((SSD_GROUPS, ts, ts), F32)]
    tile = lambda f: pl.BlockSpec((1, ts, d), f)
    return pl.pallas_call(
        functools.partial(_ssd_kernel, ts=ts),
        grid=(bsz, n_tiles // 2),
        in_specs=[
            tile(lambda b, j: (b, 0, 0)),
            tile(lambda b, j: (b, 2 * j + 1, 0)),
            tile(lambda b, j: (b, jnp.minimum(2 * j + 2, n_tiles - 1), 0)),
            pl.BlockSpec((1, 6, d), lambda b, j: (b, 0, 0)),
            _const_spec(nw.shape), _const_spec(w.shape), _const_spec(wdt.shape),
            _const_spec(wdtt.shape), _const_spec(cw.shape), _const_spec(cb.shape),
            _const_spec(pcol.shape), _const_spec(prow.shape), _const_spec(dsk.shape),
            _const_spec(gnw.shape),
        ],
        out_specs=pl.BlockSpec((1, 2 * ts, inner), lambda b, j: (b, j, 0)),
        out_shape=jax.ShapeDtypeStruct((bsz, s, inner), BF16),
        scratch_shapes=[pltpu.VMEM((ts + HALO, xbc_w), F32),
                        pltpu.VMEM((SSD_HEADS // 2, 2 * SSD_HEAD_DIM, SSD_STATE), F32)] + one_set + one_set,
        compiler_params=_cparams(("parallel", "arbitrary")),
    )(x, x, x, mod, nw, w, wdt, wdtt, cw, cb, pcol, prow, dsk, gnw)


def _merge_kernel(x_ref, mod_ref, nw_ref, oa_ref, ob_ref, oc_ref, wg_ref, wa_ref, wb_ref, wc_ref,
                  wo_ref, o_ref):
    d = x_ref.shape[-1]
    x = x_ref[0]
    m = mod_ref[0]
    h = _mod_norm(x, nw_ref[...], m[0:1], m[1:2]).astype(BF16)
    gates = _sigmoid(jnp.dot(h, wg_ref[...], preferred_element_type=F32))
    merged = (gates[:, :d] * jnp.dot(oa_ref[0], wa_ref[...], preferred_element_type=F32)
              + gates[:, d:2 * d] * jnp.dot(ob_ref[0], wb_ref[...], preferred_element_type=F32)
              + gates[:, 2 * d:] * jnp.dot(oc_ref[0], wc_ref[...], preferred_element_type=F32))
    mix = jnp.dot(merged.astype(BF16), wo_ref[...], preferred_element_type=F32)
    o_ref[0] = x + m[2:3] * mix


def _merge_call(x, mod, nw, oa, ob, oc, wg, wa, wb, wc, wo, tm):
    bsz, s, d = x.shape
    tok = lambda w: pl.BlockSpec((1, tm, w), lambda b, j: (b, j, 0))
    return pl.pallas_call(
        _merge_kernel,
        grid=(bsz, s // tm),
        in_specs=[
            tok(d), pl.BlockSpec((1, 6, d), lambda b, j: (b, 0, 0)), _const_spec(nw.shape),
            tok(oa.shape[-1]), tok(ob.shape[-1]), tok(oc.shape[-1]),
            _const_spec(wg.shape), _const_spec(wa.shape), _const_spec(wb.shape),
            _const_spec(wc.shape), _const_spec(wo.shape),
        ],
        out_specs=tok(d),
        out_shape=jax.ShapeDtypeStruct((bsz, s, d), F32),
        compiler_params=_cparams(("parallel", "parallel")),
    )(x, mod, nw, oa, ob, oc, wg, wa, wb, wc, wo)


def _ffn_kernel(x_ref, mod_ref, nw_ref, wup_ref, cw_ref, cb_ref, wdn_ref, fnw_ref, o_ref,
                ext_ref, tail_ref, *, tm, splits, final):
    hidden = wdn_ref.shape[0]

    @pl.when(pl.program_id(1) == 0)
    def _():
        tail_ref[...] = jnp.zeros_like(tail_ref)

    x = x_ref[0]
    m = mod_ref[0]
    h = _mod_norm(x, nw_ref[...], m[3:4], m[4:5]).astype(BF16)

    def conv_part(c0, fc):
        u = jnp.dot(h, wup_ref[:, c0:c0 + fc], preferred_element_type=F32)
        ext_ref[0:HALO, 0:fc] = tail_ref[:, c0:c0 + fc]
        ext_ref[HALO:HALO + tm, 0:fc] = u
        cw = cw_ref[:, c0:c0 + fc]
        acc = cw[FFN_CONV - 1:FFN_CONV] * u + cb_ref[:, c0:c0 + fc]
        for i in range(FFN_CONV - 1):
            off = HALO - (FFN_CONV - 1) + i
            acc = acc + cw[i:i + 1] * ext_ref[off:off + tm, 0:fc]
        tail_ref[:, c0:c0 + fc] = ext_ref[tm:tm + HALO, 0:fc]
        return acc

    y = jnp.zeros_like(x)
    for c0, fc in splits:
        gate = conv_part(c0, fc)
        val = conv_part(hidden + c0, fc)
        act = (_silu(gate) * val).astype(BF16)
        y = y + jnp.dot(act, wdn_ref[c0:c0 + fc, :], preferred_element_type=F32)
    out = x + m[5:6] * y
    if final:
        out = out * lax.rsqrt(jnp.mean(out * out, axis=-1, keepdims=True) + EPS) * fnw_ref[...]
    o_ref[0] = out


def _ffn_call(x, mod, nw, wup, cw, cb, wdn, fnw, tm, final):
    bsz, s, d = x.shape
    hidden = wdn.shape[0]
    first = -(-(hidden // MXU_TILE) // 2) * MXU_TILE
    splits = ((0, first), (first, hidden - first))
    tok = pl.BlockSpec((1, tm, d), lambda b, j: (b, j, 0))
    single = lambda a: pl.BlockSpec(a.shape, lambda *_: (0,) * a.ndim, pipeline_mode=pl.Buffered(1))
    return pl.pallas_call(
        functools.partial(_ffn_kernel, tm=tm, splits=splits, final=final),
        grid=(bsz, s // tm),
        in_specs=[
            tok, pl.BlockSpec((1, 6, d), lambda b, j: (b, 0, 0)), _const_spec(nw.shape),
            single(wup), _const_spec(cw.shape), _const_spec(cb.shape), single(wdn),
            _const_spec(fnw.shape),
        ],
        out_specs=tok,
        out_shape=jax.ShapeDtypeStruct((bsz, s, d), F32),
        scratch_shapes=[pltpu.VMEM((tm + HALO, first), F32),
                        pltpu.VMEM((HALO, 2 * hidden), F32)],
        compiler_params=_cparams(("parallel", "arbitrary")),
    )(x, mod, nw, wup, cw, cb, wdn, fnw)


def _pad_lanes(a, width=128):
    return jnp.pad(a, ((0, 0), (0, width - a.shape[-1])))


def kernel(x, c, w_ada, b_ada, norm1_w, w_in, gdn_conv_w, gdn_a_log, gdn_dt_bias, gdn_norm_w,
           hgrn_lb_param, hgrn_norm_w, ssd_conv_w, ssd_conv_b, ssd_a_log, ssd_dt_bias, ssd_d,
           ssd_norm_w, w_br_a, w_br_b, w_br_c, w_out, norm2_w, ffn_w_up, ffn_conv_w, ffn_conv_b,
           ffn_w_down, final_norm_w):
    bsz, s, d = x.shape
    depth = w_in.shape[0]
    gk = GDN_HEADS * GDN_DK
    gv = GDN_HEADS * GDN_DV
    hk = HGRN_HEADS * HGRN_DK
    inner = SSD_HEADS * SSD_HEAD_DIM
    xbc_w = inner + 2 * SSD_GROUPS * SSD_STATE

    sizes = (2 * gk + gv, GDN_HEADS, GDN_HEADS, gv, hk, hk, hk, hk, inner, xbc_w, SSD_HEADS, 3 * d)
    offs = [0]
    for sz in sizes:
        offs.append(offs[-1] + sz)
    (o_qkv, o_a, o_b, o_z, o_hq, _, _, _, o_sz, o_xbc, o_dt, o_gate, o_end) = offs

    nb_gdn = 4 if bsz % 4 == 0 else (2 if bsz % 2 == 0 else 1)
    rows_gdn = min(max(GDN_CHUNK, 256 // nb_gdn), s)
    ts_hgrn = min(1024, s)
    ts_ssd = min(256, s)
    tm_merge = min(512, s)
    tm_ffn = min(512, s)

    mod = _ada_call(c, w_ada, b_ada).reshape(depth, bsz, 6, d)
    fnw = final_norm_w.reshape(1, d)

    for l in range(depth):
        wl = w_in[l]
        mod_l = mod[l]
        nw1 = norm1_w[l].reshape(1, d)

        w_gdn = jnp.concatenate([wl[:, o_qkv:o_a], wl[:, o_z:o_hq]], axis=1).astype(BF16)
        w_ab = wl[:, o_a:o_z]
        o_ga = _gdn_call(
            x, mod_l, nw1, w_gdn, _pad_lanes(w_ab).astype(BF16), w_ab.T.astype(BF16),
            gdn_conv_w[l],
            _pad_lanes(jnp.stack([gdn_a_log[l], gdn_dt_bias[l]])),
            jnp.pad(jnp.stack([gdn_a_log[l], gdn_dt_bias[l]], axis=1), ((0, GDN_HEADS), (0, 0))),
            gdn_norm_w[l].reshape(1, GDN_DV), nb_gdn, rows_gdn)

        o_hg = _hgrn_call(x, mod_l, nw1, wl[:, o_hq:o_sz].astype(BF16), hgrn_lb_param,
                          hgrn_norm_w[l].reshape(1, HGRN_DK), ts_hgrn, l)

        w_dt = wl[:, o_dt:o_gate]
        o_sd = _ssd_call(
            x, mod_l, nw1, wl[:, o_sz:o_dt].astype(BF16), _pad_lanes(w_dt).astype(BF16),
            w_dt.T.astype(BF16), ssd_conv_w[l], ssd_conv_b[l].reshape(1, xbc_w),
            _pad_lanes(jnp.stack([ssd_a_log[l], ssd_dt_bias[l]])),
            jnp.stack([ssd_a_log[l], ssd_dt_bias[l]], axis=1),
            jnp.repeat(ssd_d[l], SSD_HEAD_DIM).reshape(1, inner),
            ssd_norm_w[l].reshape(1, inner), ts_ssd)

        x = _merge_call(x, mod_l, nw1, o_ga, o_hg, o_sd, wl[:, o_gate:o_end].astype(BF16),
                        w_br_a[l].astype(BF16), w_br_b[l].astype(BF16), w_br_c[l].astype(BF16),
                        w_out[l].astype(BF16), tm_merge)

        x = _ffn_call(x, mod_l, norm2_w[l].reshape(1, d), ffn_w_up[l].astype(BF16), ffn_conv_w[l],
                      ffn_conv_b[l].reshape(1, -1), ffn_w_down[l].astype(BF16), fnw, tm_ffn,
                      final=(l == depth - 1))
    return x
```

```python
import functools

import jax
import jax.numpy as jnp
from jax import lax
from jax.experimental import pallas as pl
from jax.experimental.pallas import tpu as pltpu

F32 = jnp.float32
BF16 = jnp.bfloat16
EPS = 1e-6

GDN_HEADS = 4
GDN_DK = 128
GDN_DV = 128
GDN_CONV = 4
GDN_CHUNK = 64
GDN_SOLVE_BASE = 8
HGRN_HEADS = 4
HGRN_DK = 128
HGRN_CHUNK = 16
SSD_HEADS = 8
SSD_HEAD_DIM = 64
SSD_GROUPS = 2
SSD_STATE = 128
SSD_CONV = 4
FFN_CONV = 3

HALO = 8
MXU_TILE = 256
VMEM_LIMIT = 56 * 1024 * 1024

_HI = lax.Precision.HIGHEST


def _mm(a, b):
    return jnp.dot(a.astype(BF16), b.astype(BF16), preferred_element_type=F32)


def _mm_nt(a, b):
    return lax.dot_general(a.astype(BF16), b.astype(BF16), (((1,), (1,)), ((), ())),
                           preferred_element_type=F32)


def _mm_tn(a, b):
    return lax.dot_general(a.astype(BF16), b.astype(BF16), (((0,), (0,)), ((), ())),
                           preferred_element_type=F32)


def _mm_hi(a, b):
    return jnp.dot(a, b, precision=_HI, preferred_element_type=F32)


def _sigmoid(x):
    return jax.nn.sigmoid(x)


def _silu(x):
    return x * jax.nn.sigmoid(x)


def _softplus(x):
    return jnp.maximum(x, 0.0) + jnp.log(1.0 + jnp.exp(-jnp.abs(x)))


def _mod_norm(x, nw, shift, scale):
    y = x * lax.rsqrt(jnp.mean(x * x, axis=-1, keepdims=True) + EPS) * nw
    return y * (1.0 + scale) + shift


def _tri(n, strict=False):
    r = lax.broadcasted_iota(jnp.int32, (n, n), 0)
    c = lax.broadcasted_iota(jnp.int32, (n, n), 1)
    return (r > c) if strict else (r >= c)


def _cparams(sem):
    return pltpu.CompilerParams(dimension_semantics=sem, vmem_limit_bytes=VMEM_LIMIT)


def _const_spec(shape):
    nd = len(shape)
    return pl.BlockSpec(shape, lambda *_: (0,) * nd)


def _alternate(*gens):
    live = list(gens)
    while live:
        for gen in list(live):
            if next(gen, live) is live:
                live.remove(gen)


def _ada_kernel(c_ref, w_ref, b_ref, o_ref):
    c = c_ref[...]
    o_ref[0] = _mm_hi(_silu(c), w_ref[0]) + b_ref[0]


def _ada_call(c, w_ada, b_ada):
    depth, d, n = w_ada.shape
    bsz = c.shape[0]
    tn = 1536
    return pl.pallas_call(
        _ada_kernel,
        grid=(depth, n // tn),
        in_specs=[
            pl.BlockSpec((bsz, d), lambda l, j: (0, 0)),
            pl.BlockSpec((1, d, tn), lambda l, j: (l, 0, j)),
            pl.BlockSpec((1, 1, tn), lambda l, j: (l, 0, j)),
        ],
        out_specs=pl.BlockSpec((1, bsz, tn), lambda l, j: (l, 0, j)),
        out_shape=jax.ShapeDtypeStruct((depth, bsz, n), F32),
        compiler_params=_cparams(("arbitrary", "arbitrary")),
    )(c, w_ada, b_ada.reshape(depth, 1, n))


def _split3(a):
    hi = a.astype(BF16)
    lo = (a - hi.astype(F32)).astype(BF16)
    return hi, lo


def _dot3(a_hi, a_lo, b_hi, b_lo):
    f = lambda p, q: jnp.dot(p, q, preferred_element_type=F32)
    return f(a_hi, b_hi) + (f(a_lo, b_hi) + f(a_hi, b_lo))


def _unit_lower_inverse(ms, bd_mask, base):
    c = ms[0].shape[0]
    nb = ms[0].shape[1] // c
    f = lambda p, q: jnp.dot(p, q, preferred_element_type=F32)
    row = lax.broadcasted_iota(jnp.int32, (c, nb * c), 0)
    col = lax.broadcasted_iota(jnp.int32, (c, nb * c), 1) % c
    eye = jnp.where(row == col, 1.0, 0.0).astype(F32)

    def blockdiag(x):
        return jnp.concatenate([x] * nb, axis=0) * bd_mask

    def dot3(a_hi, a_lo, b_hi, b_lo):
        r = a_hi.shape[0]
        t = f(jnp.concatenate([a_hi, a_lo], axis=0), blockdiag(b_hi))
        return t[:r] + (t[r:] + f(a_hi, blockdiag(b_lo)))

    n_items = range(len(ms))
    n_steps = base.bit_length() - 1
    in_base = (row // base) == (col // base)
    ds = [jnp.where(in_base, m, 0.0) for m in ms]
    xs = [eye - d for d in ds]
    if n_steps > 1:
        qs = []
        for d in ds:
            hi, lo = _split3(d)
            qs.append(dot3(hi, lo, hi, lo))
        for step in range(1, n_steps):
            for i in n_items:
                if step == n_steps - 1:
                    p_hi, p_lo = _split3(xs[i])
                    q_hi, q_lo = _split3(qs[i])
                    xs[i] = xs[i] + dot3(p_hi, p_lo, q_hi, q_lo)
                else:
                    hi, lo = _split3(jnp.concatenate([qs[i], xs[i]], axis=0))
                    r = dot3(hi, lo, hi[:c], lo[:c])
                    qs[i] = r[:c]
                    xs[i] = xs[i] + r[c:]
            yield
    b = base
    while b < c:
        level = ((row // (2 * b)) == (col // (2 * b))) & ((row // b) != (col // b))
        splits = [_split3(x) for x in xs]
        ts = []
        for i in n_items:
            l_hi, l_lo = _split3(jnp.where(level, ms[i], 0.0))
            ts.append(dot3(splits[i][0], splits[i][1], l_hi, l_lo))
        yield
        for i in n_items:
            t_hi, t_lo = _split3(ts[i])
            xs[i] = xs[i] - dot3(t_hi, t_lo, splits[i][0], splits[i][1])
        yield
        b *= 2
    return xs


def _gdn_kernel(x0_ref, xa_ref, xb_ref, mod_ref, nw_ref, w_ref, wsm_ref, wsmt_ref, cw_ref,
                pcol_ref, prow_ref, gnw_ref, o_ref, ext_ref, st_ref, *sets, nb, rows):
    kw = GDN_HEADS * GDN_DK
    vw = GDN_HEADS * GDN_DV
    qkv_w = 2 * kw + vw
    c = GDN_CHUNK
    ts = nb * rows
    n_chunks = ts // c
    heads = range(GDN_HEADS)
    hc = GDN_HEADS * c
    set_a, set_b = sets[:7], sets[7:]
    wd = GDN_DV + GDN_DK

    m = mod_ref[...]
    nw = nw_ref[...]
    gnw = gnw_ref[...]
    row_i = lax.broadcasted_iota(jnp.int32, (c, hc), 0)
    lane_i = lax.broadcasted_iota(jnp.int32, (c, hc), 1)
    lane_head = lane_i // c
    col_i = lane_i - lane_head * c
    incl_bd = row_i >= col_i
    strict_bd = row_i > col_i

    def prepare(x_ref, dst):
        qg_s, kn_s, rhs_s, z_s, gcol_s, m_s, qk_s = dst
        h = _mod_norm(x_ref[...], nw, m[:, 0:1, :], m[:, 1:2, :])
        h = h.reshape(ts, h.shape[-1]).astype(BF16)
        yield
        proj = jnp.dot(h, w_ref[...], preferred_element_type=F32)
        z_s[...] = _silu(proj[:, qkv_w:])
        yield
        cw = cw_ref[...]
        accs = []
        for i in range(nb):
            p_i = proj[i * rows:(i + 1) * rows, :qkv_w]
            ext_ref[i, HALO:HALO + rows, :] = p_i
            acc = cw[GDN_CONV - 1:GDN_CONV] * p_i
            for t in range(GDN_CONV - 1):
                off = HALO - (GDN_CONV - 1) + t
                acc = acc + cw[t:t + 1] * ext_ref[i, off:off + rows, :]
            ext_ref[i, 0:HALO, :] = ext_ref[i, rows:rows + HALO, :]
            accs.append(acc)
            yield
        qkv = _silu(jnp.concatenate(accs, axis=0))

        sm = jnp.dot(h, wsm_ref[...], preferred_element_type=F32)
        smt = lax.dot_general(wsmt_ref[...], h, (((1,), (1,)), ((), ())), preferred_element_type=F32)
        pcol = pcol_ref[...]
        prow = prow_ref[...]
        g_cols = -jnp.exp(pcol[0:1]) * _softplus(sm + pcol[1:2])
        beta_cols = _sigmoid(sm)
        g_rows = -jnp.exp(prow[:, 0:1]) * _softplus(smt + prow[:, 1:2])
        rr = lax.broadcasted_iota(jnp.int32, (ts, ts), 0)
        cc = lax.broadcasted_iota(jnp.int32, (ts, ts), 1)
        same = (rr // c) == (cc // c)
        gc_all = _mm_hi(jnp.where(same & (rr >= cc), 1.0, 0.0).astype(F32), g_cols)
        gr_all = _mm_hi(g_rows, jnp.where(same & (rr <= cc), 1.0, 0.0).astype(F32))
        gcol_s[...] = gc_all
        yield
        for ci in range(n_chunks):
            r0 = ci * c
            g_col_bd = jnp.zeros((c, hc), F32)
            kks, qks = [], []
            for hd in heads:
                q = qkv[r0:r0 + c, hd * GDN_DK:(hd + 1) * GDN_DK]
                k = qkv[r0:r0 + c, kw + hd * GDN_DK:kw + (hd + 1) * GDN_DK]
                v = qkv[r0:r0 + c, 2 * kw + hd * GDN_DV:2 * kw + (hd + 1) * GDN_DV]
                q = q * lax.rsqrt(jnp.sum(q * q, axis=-1, keepdims=True) + EPS) * (GDN_DK ** -0.5)
                k = k * lax.rsqrt(jnp.sum(k * k, axis=-1, keepdims=True) + EPS)
                g_col = gc_all[r0:r0 + c, hd:hd + 1]
                beta = beta_cols[r0:r0 + c, GDN_HEADS + hd:GDN_HEADS + hd + 1]
                eg = jnp.exp(g_col)
                kb = k * beta
                qg_s[r0:r0 + c, hd * GDN_DK:(hd + 1) * GDN_DK] = q * eg
                kn_s[r0:r0 + c, hd * GDN_DK:(hd + 1) * GDN_DK] = k
                rhs_s[r0:r0 + c, hd * wd:hd * wd + GDN_DV] = v * beta
                rhs_s[r0:r0 + c, hd * wd + GDN_DV:(hd + 1) * wd] = kb * eg
                kks.append(_mm_nt(kb, k))
                qks.append(_mm_nt(q, k))
                g_col_bd = jnp.where(lane_head == hd, g_col, g_col_bd)
            g_row_bd = jnp.concatenate([gr_all[hd:hd + 1, r0:r0 + c] for hd in heads], axis=1)
            decay = jnp.where(incl_bd, jnp.exp(g_col_bd - g_row_bd), 0.0)
            m_s[r0:r0 + c, :] = jnp.where(strict_bd, jnp.concatenate(kks, axis=1) * decay, 0.0)
            qk_s[r0:r0 + c, :] = jnp.concatenate(qks, axis=1) * decay
            yield

    def finish(src, out_rows):
        qg_s, kn_s, rhs_s, z_s, gcol_s, m_s, qk_s = src
        bd_mask = jnp.where(lax.broadcasted_iota(jnp.int32, (hc, hc), 0) // c
                            == lax.broadcasted_iota(jnp.int32, (hc, hc), 1) // c, 1.0, 0.0).astype(BF16)
        ms = [m_s[ci * c:(ci + 1) * c, :] for ci in range(n_chunks)]
        invs = yield from _unit_lower_inverse(ms, bd_mask, GDN_SOLVE_BASE)
        pairs = [(ci, hd) for ci in range(n_chunks) for hd in heads]
        uw = {}
        for ci, hd in pairs:
            r0 = ci * c
            rhs_hi, rhs_lo = _split3(rhs_s[r0:r0 + c, hd * wd:(hd + 1) * wd])
            inv_hi, inv_lo = _split3(invs[ci][:, hd * c:(hd + 1) * c])
            uw[ci, hd] = _dot3(inv_hi, inv_lo, rhs_hi, rhs_lo)
            if hd == GDN_HEADS - 1:
                yield
        cps = rows // c
        items = [(i, hd) for i in range(nb) for hd in heads]
        for j in range(cps):
            ci = {i: i * cps + j for i in range(nb)}
            hsl = {hd: slice(hd * GDN_DK, (hd + 1) * GDN_DK) for hd in heads}
            st = {(i, hd): st_ref[i, hd] for i, hd in items}
            ws = {(i, hd): _mm(jnp.concatenate([uw[ci[i], hd][:, GDN_DV:],
                                                qg_s[ci[i] * c:(ci[i] + 1) * c, hsl[hd]]], axis=0), st[i, hd])
                  for i, hd in items}
            yield
            v_new = {(i, hd): uw[ci[i], hd][:, :GDN_DV] - ws[i, hd][:c] for i, hd in items}
            for i, hd in items:
                r0 = ci[i] * c
                g_col = gcol_s[r0:r0 + c, hd:hd + 1]
                g_end = g_col[c - 1:c, :]
                k_end = kn_s[r0:r0 + c, hsl[hd]] * jnp.exp(g_end - g_col)
                st_ref[i, hd] = st[i, hd] * jnp.exp(g_end) + _mm_tn(k_end, v_new[i, hd])
            yield
            for i, hd in items:
                r0 = ci[i] * c
                o = ws[i, hd][c:] + _mm(qk_s[r0:r0 + c, hd * c:(hd + 1) * c], v_new[i, hd])
                o = o * lax.rsqrt(jnp.mean(o * o, axis=-1, keepdims=True) + EPS) * gnw * z_s[r0:r0 + c, hsl[hd]]
                o_ref[i, out_rows + j * c:out_rows + (j + 1) * c, hsl[hd]] = o.astype(o_ref.dtype)
                if hd == GDN_HEADS - 1:
                    yield

    @pl.when(pl.program_id(1) == 0)
    def _():
        st_ref[...] = jnp.zeros_like(st_ref)
        ext_ref[:, 0:HALO, :] = jnp.zeros((nb, HALO, qkv_w), F32)
        _alternate(prepare(x0_ref, set_a))

    _alternate(finish(set_a, 0), prepare(xa_ref, set_b))
    _alternate(finish(set_b, rows), prepare(xb_ref, set_a))


def _gdn_call(x, mod, nw, w, wsm, wsmt, cw, pcol, prow, gnw, nb, rows):
    bsz, s, d = x.shape
    kw = GDN_HEADS * GDN_DK
    vw = GDN_HEADS * GDN_DV
    qkv_w = 2 * kw + vw
    ts = nb * rows
    n_tiles = s // rows
    hc = GDN_HEADS * GDN_CHUNK
    one_set = [pltpu.VMEM((ts, kw), F32), pltpu.VMEM((ts, kw), F32),
               pltpu.VMEM((ts, vw + kw), F32), pltpu.VMEM((ts, vw), F32),
               pltpu.VMEM((ts, 128), F32), pltpu.VMEM((ts, hc), F32), pltpu.VMEM((ts, hc), F32)]
    tile = lambda f: pl.BlockSpec((nb, rows, d), f)
    return pl.pallas_call(
        functools.partial(_gdn_kernel, nb=nb, rows=rows),
        grid=(bsz // nb, n_tiles // 2),
        in_specs=[
            tile(lambda b, j: (b, 0, 0)),
            tile(lambda b, j: (b, 2 * j + 1, 0)),
            tile(lambda b, j: (b, jnp.minimum(2 * j + 2, n_tiles - 1), 0)),
            pl.BlockSpec((nb, 6, d), lambda b, j: (b, 0, 0)),
            _const_spec(nw.shape), _const_spec(w.shape), _const_spec(wsm.shape),
            _const_spec(wsmt.shape), _const_spec(cw.shape), _const_spec(pcol.shape),
            _const_spec(prow.shape), _const_spec(gnw.shape),
        ],
        out_specs=pl.BlockSpec((nb, 2 * rows, vw), lambda b, j: (b, j, 0)),
        out_shape=jax.ShapeDtypeStruct((bsz, s, vw), BF16),
        scratch_shapes=[pltpu.VMEM((nb, rows + HALO, qkv_w), F32),
                        pltpu.VMEM((nb, GDN_HEADS, GDN_DK, GDN_DV), F32)] + one_set + one_set,
        compiler_params=_cparams(("parallel", "arbitrary")),
    )(x, x, x, mod, nw, w, wsm, wsmt, cw, pcol, prow, gnw)


def _group_cumsum_rows(x, group):
    rows = lax.broadcasted_iota(jnp.int32, x.shape, 0) % group
    d = 1
    while d < group:
        x = x + jnp.where(rows >= d, pltpu.roll(x, d, axis=0), 0.0)
        d *= 2
    return x


def _hgrn_kernel(x_ref, mod_ref, nw_ref, w_ref, lbp_ref, gnw_ref, o_ref, *scratch, ts, layer, unit, group):
    kw = HGRN_HEADS * HGRN_DK
    c = HGRN_CHUNK
    n_sub = unit // c
    n_units = ts // unit
    sets = (scratch[0:7], scratch[7:14])
    st_ref = scratch[14]

    @pl.when(pl.program_id(1) == 0)
    def _():
        st_ref[...] = jnp.zeros_like(st_ref)

    m = mod_ref[0]
    nw = nw_ref[...]
    gnw = gnw_ref[...]
    incl = _tri(c)
    heads = range(HGRN_HEADS)
    sl = [slice(hd * HGRN_DK, (hd + 1) * HGRN_DK) for hd in heads]

    lbp = lbp_ref[...]
    e = jnp.exp(lbp - jnp.max(lbp, axis=0, keepdims=True))
    soft = e / jnp.sum(e, axis=0, keepdims=True)
    lb = jnp.zeros((1, kw), F32)
    for i in range(1, layer + 1):
        lb = lb + soft[i:i + 1]

    def prepare(u_idx, dst):
        qs_ref, ks_ref, qg_ref, ke_ref, v_ref, ge_ref, z_ref = dst
        rows = pl.ds(pl.multiple_of(u_idx * unit, unit), unit)
        h = _mod_norm(x_ref[0, rows, :], nw, m[0:1], m[1:2]).astype(BF16)
        yield
        f_raw = jnp.dot(h, w_ref[:, kw:2 * kw], preferred_element_type=F32)
        sig = _sigmoid(f_raw)
        g = _group_cumsum_rows(jnp.log(lb + (1.0 - lb) * sig), c).reshape(n_sub, c, kw)
        g_mid = g[:, c // 2 - 1:c // 2, :]
        g_last = g[:, c - 1:c, :]
        g_end = jnp.exp(g_last).reshape(n_sub, kw)
        for hd in heads:
            ge_ref[hd] = g_end[:, sl[hd]].T
        yield
        k = ((1.0 - lb) * _sigmoid(-f_raw)).reshape(n_sub, c, kw)
        ks_ref[...] = (k * jnp.exp(g_mid - g)).reshape(unit, kw).astype(BF16)
        ke_ref[...] = (k * jnp.exp(g_last - g)).reshape(unit, kw).astype(BF16)
        yield
        q = _silu(jnp.dot(h, w_ref[:, :kw], preferred_element_type=F32)).reshape(n_sub, c, kw)
        qs_ref[...] = (q * jnp.exp(g - g_mid)).reshape(unit, kw).astype(BF16)
        qg_ref[...] = (q * jnp.exp(g)).reshape(unit, kw).astype(BF16)
        yield
        v_ref[...] = jnp.dot(h, w_ref[:, 2 * kw:3 * kw], preferred_element_type=F32).astype(BF16)
        yield
        z_ref[...] = _silu(jnp.dot(h, w_ref[:, 3 * kw:], preferred_element_type=F32))

    def recur(u_idx, src):
        qs_ref, ks_ref, qg_ref, ke_ref, v_ref, ge_ref, z_ref = src
        base = pl.multiple_of(u_idx * unit, unit)
        for g0 in range(0, n_sub, group):
            subs = range(g0, g0 + group)
            rows = {u: slice(u * c, (u + 1) * c) for u in subs}
            pairs = [(u, hd) for u in subs for hd in heads]
            vv = {u: v_ref[rows[u], :] for u in subs}
            sc = {(u, hd): _mm_nt(qs_ref[rows[u], sl[hd]], ks_ref[rows[u], sl[hd]]) for u, hd in pairs}
            yield
            kv = {(u, hd): _mm_tn(ke_ref[rows[u], sl[hd]], vv[u][:, sl[hd]]) for u, hd in pairs}
            yield
            attn = {(u, hd): jnp.where(incl, sc[u, hd], 0.0).astype(BF16) for u, hd in pairs}
            yield
            st = [st_ref[hd] for hd in heads]
            oo = {}
            for u in subs:
                for hd in heads:
                    lhs = jnp.concatenate([qg_ref[rows[u], sl[hd]], attn[u, hd]], axis=1)
                    rhs = jnp.concatenate([st[hd].astype(BF16), vv[u][:, sl[hd]]], axis=0)
                    oo[u, hd] = jnp.dot(lhs, rhs, preferred_element_type=F32)
                    st[hd] = st[hd] * ge_ref[hd][:, u:u + 1] + kv[u, hd]
                if u == g0 + group // 2 - 1:
                    yield
            for hd in heads:
                st_ref[hd] = st[hd]
            yield
            for u, hd in pairs:
                o = oo[u, hd]
                o = o * lax.rsqrt(jnp.mean(o * o, axis=-1, keepdims=True) + EPS) * gnw * z_ref[rows[u], sl[hd]]
                o_ref[0, pl.ds(base + u * c, c), sl[hd]] = o.astype(o_ref.dtype)
            yield

    _alternate(prepare(0, sets[0]))

    def body(i, carry):
        _alternate(recur(2 * i, sets[0]), prepare(2 * i + 1, sets[1]))
        _alternate(recur(2 * i + 1, sets[1]), prepare(jnp.minimum(2 * i + 2, n_units - 1), sets[0]))
        return carry

    lax.fori_loop(0, n_units // 2, body, 0)


def _hgrn_call(x, mod, nw, w, lbp, gnw, ts, layer):
    bsz, s, d = x.shape
    kw = HGRN_HEADS * HGRN_DK
    group = 8
    unit = min(256, ts // 2)
    one_set = ([pltpu.VMEM((unit, kw), BF16) for _ in range(5)]
               + [pltpu.VMEM((HGRN_HEADS, HGRN_DK, unit // HGRN_CHUNK), F32), pltpu.VMEM((unit, kw), F32)])
    return pl.pallas_call(
        functools.partial(_hgrn_kernel, ts=ts, layer=layer, unit=unit, group=group),
        grid=(bsz, s // ts),
        in_specs=[
            pl.BlockSpec((1, ts, d), lambda b, j: (b, j, 0)),
            pl.BlockSpec((1, 6, d), lambda b, j: (b, 0, 0)),
            _const_spec(nw.shape), _const_spec(w.shape), _const_spec(lbp.shape),
            _const_spec(gnw.shape),
        ],
        out_specs=pl.BlockSpec((1, ts, kw), lambda b, j: (b, j, 0)),
        out_shape=jax.ShapeDtypeStruct((bsz, s, kw), BF16),
        scratch_shapes=one_set + one_set + [pltpu.VMEM((HGRN_HEADS, HGRN_DK, HGRN_DK), F32)],
        compiler_params=_cparams(("parallel", "arbitrary")),
    )(x, mod, nw, w, lbp, gnw)


def _ssd_kernel(x0_ref, xa_ref, xb_ref, mod_ref, nw_ref, w_ref, wdt_ref, wdtt_ref, cw_ref, cb_ref,
                pcol_ref, prow_ref, dsk_ref, gnw_ref, o_ref, ext_ref, st_ref, *sets, ts):
    inner = SSD_HEADS * SSD_HEAD_DIM
    bc_w = 2 * SSD_GROUPS * SSD_STATE
    xbc_w = inner + bc_w
    hpg = SSD_HEADS // SSD_GROUPS
    p = SSD_HEAD_DIM
    set_a, set_b = sets[:7], sets[7:]

    m = mod_ref[0]
    nw = nw_ref[...]
    incl = _tri(ts)

    def prepare(x_ref, dst):
        xs_s, bc_s, zs_s, dtc_s, acc_s, acr_s, cb_s = dst
        h = _mod_norm(x_ref[0], nw, m[0:1], m[1:2]).astype(BF16)
        yield
        proj = jnp.dot(h, w_ref[...], preferred_element_type=F32)
        zs_s[...] = _silu(proj[:, :inner])
        yield
        ext_ref[HALO:HALO + ts, :] = proj[:, inner:]
        cw = cw_ref[...]
        acc = cw[SSD_CONV - 1:SSD_CONV] * proj[:, inner:] + cb_ref[...]
        for i in range(SSD_CONV - 1):
            off = HALO - (SSD_CONV - 1) + i
            acc = acc + cw[i:i + 1] * ext_ref[off:off + ts, :]
        ext_ref[0:HALO, :] = ext_ref[ts:ts + HALO, :]
        xbc = _silu(acc)
        xs_s[...] = xbc[:, :inner]
        bc_s[...] = xbc[:, inner:]
        yield
        pcol = pcol_ref[...]
        prow = prow_ref[...]
        dt_cols = _softplus(jnp.dot(h, wdt_ref[...], preferred_element_type=F32) + pcol[1:2])
        dt_rows = _softplus(lax.dot_general(wdtt_ref[...], h, (((1,), (1,)), ((), ())),
                                            preferred_element_type=F32) + prow[:, 1:2])
        low = jnp.where(incl, 1.0, 0.0).astype(F32)
        upp = jnp.where(lax.broadcasted_iota(jnp.int32, (ts, ts), 0)
                        <= lax.broadcasted_iota(jnp.int32, (ts, ts), 1), 1.0, 0.0).astype(F32)
        dtc_s[...] = dt_cols
        acc_s[...] = _mm_hi(low, dt_cols * (-jnp.exp(pcol[0:1])))
        acr_s[...] = _mm_hi(dt_rows * (-jnp.exp(prow[:, 0:1])), upp)
        yield
        for g in range(SSD_GROUPS):
            bm = xbc[:, inner + g * SSD_STATE:inner + (g + 1) * SSD_STATE]
            cm = xbc[:, inner + (SSD_GROUPS + g) * SSD_STATE:inner + (SSD_GROUPS + g + 1) * SSD_STATE]
            cb_s[g] = _mm_nt(cm, bm)
            yield

    def finish(src, out_rows):
        xs_s, bc_s, zs_s, dtc_s, acc_s, acr_s, cb_s = src
        first = lax.broadcasted_iota(jnp.int32, (ts, 2 * p), 1) < p
        first_rows = lax.broadcasted_iota(jnp.int32, (2 * p, SSD_STATE), 0) < p
        ys = []
        for g in range(SSD_GROUPS):
            bm = bc_s[:, g * SSD_STATE:(g + 1) * SSD_STATE]
            cm = bc_s[:, (SSD_GROUPS + g) * SSD_STATE:(SSD_GROUPS + g + 1) * SSD_STATE]
            cb = cb_s[g]
            for pr in range(hpg // 2):
                h0 = g * hpg + 2 * pr
                h1 = h0 + 1
                a0, a1 = acc_s[:, h0:h0 + 1], acc_s[:, h1:h1 + 1]
                seg0 = jnp.where(incl, jnp.exp(a0 - acr_s[h0:h0 + 1, :]), 0.0)
                seg1 = jnp.where(incl, jnp.exp(a1 - acr_s[h1:h1 + 1, :]), 0.0)
                xdt = xs_s[:, h0 * p:(h1 + 1) * p] * jnp.where(first, dtc_s[:, h0:h0 + 1], dtc_s[:, h1:h1 + 1])
                st = st_ref[g * (hpg // 2) + pr]
                y = (_mm(cb * seg0, jnp.where(first, xdt, 0.0)) + _mm(cb * seg1, jnp.where(first, 0.0, xdt))
                     + _mm_nt(cm, st) * jnp.where(first, jnp.exp(a0), jnp.exp(a1)))
                yield
                last0, last1 = a0[ts - 1:ts, :], a1[ts - 1:ts, :]
                w_in = jnp.where(first, jnp.exp(last0 - a0), jnp.exp(last1 - a1))
                st_ref[g * (hpg // 2) + pr] = (st * jnp.where(first_rows, jnp.exp(last0), jnp.exp(last1))
                                               + _mm_tn(xdt * w_in, bm))
                ys.append(y)
                yield
        gnw = gnw_ref[...]
        gw = hpg * p
        for g in range(SSD_GROUPS):
            cols = slice(g * gw, (g + 1) * gw)
            yg = jnp.concatenate(ys[g * (hpg // 2):(g + 1) * (hpg // 2)], axis=1)
            yg = (yg + dsk_ref[:, cols] * xs_s[:, cols]) * zs_s[:, cols]
            yg = yg * lax.rsqrt(jnp.mean(yg * yg, axis=-1, keepdims=True) + EPS) * gnw[:, cols]
            o_ref[0, out_rows:out_rows + ts, cols] = yg.astype(o_ref.dtype)
            yield

    @pl.when(pl.program_id(1) == 0)
    def _():
        st_ref[...] = jnp.zeros_like(st_ref)
        ext_ref[0:HALO, :] = jnp.zeros((HALO, xbc_w), F32)
        _alternate(prepare(x0_ref, set_a))

    _alternate(finish(set_a, 0), prepare(xa_ref, set_b))
    _alternate(finish(set_b, ts), prepare(xb_ref, set_a))


def _ssd_call(x, mod, nw, w, wdt, wdtt, cw, cb, pcol, prow, dsk, gnw, ts):
    bsz, s, d = x.shape
    inner = SSD_HEADS * SSD_HEAD_DIM
    bc_w = 2 * SSD_GROUPS * SSD_STATE
    xbc_w = inner + bc_w
    n_tiles = s // ts
    one_set = [pltpu.VMEM((ts, inner), F32), pltpu.VMEM((ts, bc_w), F32), pltpu.VMEM((ts, inner), F32),
               pltpu.VMEM((ts, 128), F32), pltpu.VMEM((ts, 128), F32), pltpu.VMEM((8, ts), F32),
               pltpu.VMEM((SSD_GROUPS, ts, ts), F32)]
    tile = lambda f: pl.BlockSpec((1, ts, d), f)
    return pl.pallas_call(
        functools.partial(_ssd_kernel, ts=ts),
        grid=(bsz, n_tiles // 2),
        in_specs=[
            tile(lambda b, j: (b, 0, 0)),
            tile(lambda b, j: (b, 2 * j + 1, 0)),
            tile(lambda b, j: (b, jnp.minimum(2 * j + 2, n_tiles - 1), 0)),
            pl.BlockSpec((1, 6, d), lambda b, j: (b, 0, 0)),
            _const_spec(nw.shape), _const_spec(w.shape), _const_spec(wdt.shape),
            _const_spec(wdtt.shape), _const_spec(cw.shape), _const_spec(cb.shape),
            _const_spec(pcol.shape), _const_spec(prow.shape), _const_spec(dsk.shape),
            _const_spec(gnw.shape),
        ],
        out_specs=pl.BlockSpec((1, 2 * ts, inner), lambda b, j: (b, j, 0)),
        out_shape=jax.ShapeDtypeStruct((bsz, s, inner), BF16),
        scratch_shapes=[pltpu.VMEM((ts + HALO, xbc_w), F32),
                        pltpu.VMEM((SSD_HEADS // 2, 2 * SSD_HEAD_DIM, SSD_STATE), F32)] + one_set + one_set,
        compiler_params=_cparams(("parallel", "arbitrary")),
    )(x, x, x, mod, nw, w, wdt, wdtt, cw, cb, pcol, prow, dsk, gnw)


def _merge_kernel(x_ref, mod_ref, nw_ref, oa_ref, ob_ref, oc_ref, wg_ref, wa_ref, wb_ref, wc_ref,
                  wo_ref, o_ref):
    d = x_ref.shape[-1]
    x = x_ref[0]
    m = mod_ref[0]
    h = _mod_norm(x, nw_ref[...], m[0:1], m[1:2]).astype(BF16)
    gates = _sigmoid(jnp.dot(h, wg_ref[...], preferred_element_type=F32))
    merged = (gates[:, :d] * jnp.dot(oa_ref[0], wa_ref[...], preferred_element_type=F32)
              + gates[:, d:2 * d] * jnp.dot(ob_ref[0], wb_ref[...], preferred_element_type=F32)
              + gates[:, 2 * d:] * jnp.dot(oc_ref[0], wc_ref[...], preferred_element_type=F32))
    mix = jnp.dot(merged.astype(BF16), wo_ref[...], preferred_element_type=F32)
    o_ref[0] = x + m[2:3] * mix


def _merge_call(x, mod, nw, oa, ob, oc, wg, wa, wb, wc, wo, tm):
    bsz, s, d = x.shape
    tok = lambda w: pl.BlockSpec((1, tm, w), lambda b, j: (b, j, 0))
    return pl.pallas_call(
        _merge_kernel,
        grid=(bsz, s // tm),
        in_specs=[
            tok(d), pl.BlockSpec((1, 6, d), lambda b, j: (b, 0, 0)), _const_spec(nw.shape),
            tok(oa.shape[-1]), tok(ob.shape[-1]), tok(oc.shape[-1]),
            _const_spec(wg.shape), _const_spec(wa.shape), _const_spec(wb.shape),
            _const_spec(wc.shape), _const_spec(wo.shape),
        ],
        out_specs=tok(d),
        out_shape=jax.ShapeDtypeStruct((bsz, s, d), F32),
        compiler_params=_cparams(("parallel", "parallel")),
    )(x, mod, nw, oa, ob, oc, wg, wa, wb, wc, wo)


def _ffn_kernel(x_ref, mod_ref, nw_ref, wup_ref, cw_ref, cb_ref, wdn_ref, fnw_ref, o_ref,
                ext_ref, tail_ref, *, tm, splits, final):
    hidden = wdn_ref.shape[0]

    @pl.when(pl.program_id(1) == 0)
    def _():
        tail_ref[...] = jnp.zeros_like(tail_ref)

    x = x_ref[0]
    m = mod_ref[0]
    h = _mod_norm(x, nw_ref[...], m[3:4], m[4:5]).astype(BF16)

    def conv_part(c0, fc):
        u = jnp.dot(h, wup_ref[:, c0:c0 + fc], preferred_element_type=F32)
        ext_ref[0:HALO, 0:fc] = tail_ref[:, c0:c0 + fc]
        ext_ref[HALO:HALO + tm, 0:fc] = u
        cw = cw_ref[:, c0:c0 + fc]
        acc = cw[FFN_CONV - 1:FFN_CONV] * u + cb_ref[:, c0:c0 + fc]
        for i in range(FFN_CONV - 1):
            off = HALO - (FFN_CONV - 1) + i
            acc = acc + cw[i:i + 1] * ext_ref[off:off + tm, 0:fc]
        tail_ref[:, c0:c0 + fc] = ext_ref[tm:tm + HALO, 0:fc]
        return acc

    y = jnp.zeros_like(x)
    for c0, fc in splits:
        gate = conv_part(c0, fc)
        val = conv_part(hidden + c0, fc)
        act = (_silu(gate) * val).astype(BF16)
        y = y + jnp.dot(act, wdn_ref[c0:c0 + fc, :], preferred_element_type=F32)
    out = x + m[5:6] * y
    if final:
        out = out * lax.rsqrt(jnp.mean(out * out, axis=-1, keepdims=True) + EPS) * fnw_ref[...]
    o_ref[0] = out


def _ffn_call(x, mod, nw, wup, cw, cb, wdn, fnw, tm, final):
    bsz, s, d = x.shape
    hidden = wdn.shape[0]
    first = -(-(hidden // MXU_TILE) // 2) * MXU_TILE
    splits = ((0, first), (first, hidden - first))
    tok = pl.BlockSpec((1, tm, d), lambda b, j: (b, j, 0))
    single = lambda a: pl.BlockSpec(a.shape, lambda *_: (0,) * a.ndim, pipeline_mode=pl.Buffered(1))
    return pl.pallas_call(
        functools.partial(_ffn_kernel, tm=tm, splits=splits, final=final),
        grid=(bsz, s // tm),
        in_specs=[
            tok, pl.BlockSpec((1, 6, d), lambda b, j: (b, 0, 0)), _const_spec(nw.shape),
            single(wup), _const_spec(cw.shape), _const_spec(cb.shape), single(wdn),
            _const_spec(fnw.shape),
        ],
        out_specs=tok,
        out_shape=jax.ShapeDtypeStruct((bsz, s, d), F32),
        scratch_shapes=[pltpu.VMEM((tm + HALO, first), F32),
                        pltpu.VMEM((HALO, 2 * hidden), F32)],
        compiler_params=_cparams(("parallel", "arbitrary")),
    )(x, mod, nw, wup, cw, cb, wdn, fnw)


def _pad_lanes(a, width=128):
    return jnp.pad(a, ((0, 0), (0, width - a.shape[-1])))


def kernel(x, c, w_ada, b_ada, norm1_w, w_in, gdn_conv_w, gdn_a_log, gdn_dt_bias, gdn_norm_w,
           hgrn_lb_param, hgrn_norm_w, ssd_conv_w, ssd_conv_b, ssd_a_log, ssd_dt_bias, ssd_d,
           ssd_norm_w, w_br_a, w_br_b, w_br_c, w_out, norm2_w, ffn_w_up, ffn_conv_w, ffn_conv_b,
           ffn_w_down, final_norm_w):
    bsz, s, d = x.shape
    depth = w_in.shape[0]
    gk = GDN_HEADS * GDN_DK
    gv = GDN_HEADS * GDN_DV
    hk = HGRN_HEADS * HGRN_DK
    inner = SSD_HEADS * SSD_HEAD_DIM
    xbc_w = inner + 2 * SSD_GROUPS * SSD_STATE

    sizes = (2 * gk + gv, GDN_HEADS, GDN_HEADS, gv, hk, hk, hk, hk, inner, xbc_w, SSD_HEADS, 3 * d)
    offs = [0]
    for sz in sizes:
        offs.append(offs[-1] + sz)
    (o_qkv, o_a, o_b, o_z, o_hq, _, _, _, o_sz, o_xbc, o_dt, o_gate, o_end) = offs

    nb_gdn = 4 if bsz % 4 == 0 else (2 if bsz % 2 == 0 else 1)
    rows_gdn = min(max(GDN_CHUNK, 256 // nb_gdn), s)
    ts_hgrn = min(1024, s)
    ts_ssd = min(256, s)
    tm_merge = min(512, s)
    tm_ffn = min(512, s)

    mod = _ada_call(c, w_ada, b_ada).reshape(depth, bsz, 6, d)
    fnw = final_norm_w.reshape(1, d)

    for l in range(depth):
        wl = w_in[l]
        mod_l = mod[l]
        nw1 = norm1_w[l].reshape(1, d)

        w_gdn = jnp.concatenate([wl[:, o_qkv:o_a], wl[:, o_z:o_hq]], axis=1).astype(BF16)
        w_ab = wl[:, o_a:o_z]
        o_ga = _gdn_call(
            x, mod_l, nw1, w_gdn, _pad_lanes(w_ab).astype(BF16), w_ab.T.astype(BF16),
            gdn_conv_w[l],
            _pad_lanes(jnp.stack([gdn_a_log[l], gdn_dt_bias[l]])),
            jnp.pad(jnp.stack([gdn_a_log[l], gdn_dt_bias[l]], axis=1), ((0, GDN_HEADS), (0, 0))),
            gdn_norm_w[l].reshape(1, GDN_DV), nb_gdn, rows_gdn)

        o_hg = _hgrn_call(x, mod_l, nw1, wl[:, o_hq:o_sz].astype(BF16), hgrn_lb_param,
                          hgrn_norm_w[l].reshape(1, HGRN_DK), ts_hgrn, l)

        w_dt = wl[:, o_dt:o_gate]
        o_sd = _ssd_call(
            x, mod_l, nw1, wl[:, o_sz:o_dt].astype(BF16), _pad_lanes(w_dt).astype(BF16),
            w_dt.T.astype(BF16), ssd_conv_w[l], ssd_conv_b[l].reshape(1, xbc_w),
            _pad_lanes(jnp.stack([ssd_a_log[l], ssd_dt_bias[l]])),
            jnp.stack([ssd_a_log[l], ssd_dt_bias[l]], axis=1),
            jnp.repeat(ssd_d[l], SSD_HEAD_DIM).reshape(1, inner),
            ssd_norm_w[l].reshape(1, inner), ts_ssd)

        x = _merge_call(x, mod_l, nw1, o_ga, o_hg, o_sd, wl[:, o_gate:o_end].astype(BF16),
                        w_br_a[l].astype(BF16), w_br_b[l].astype(BF16), w_br_c[l].astype(BF16),
                        w_out[l].astype(BF16), tm_merge)

        x = _ffn_call(x, mod_l, norm2_w[l].reshape(1, d), ffn_w_up[l].astype(BF16), ffn_conv_w[l],
                      ffn_conv_b[l].reshape(1, -1), ffn_w_down[l].astype(BF16), fnw, tm_ffn,
                      final=(l == depth - 1))
    return x
```

```python
import functools

import jax
import jax.numpy as jnp
from jax import lax
from jax.experimental import pallas as pl
from jax.experimental.pallas import tpu as pltpu

F32 = jnp.float32
BF16 = jnp.bfloat16
EPS = 1e-6

GDN_HEADS = 4
GDN_DK = 128
GDN_DV = 128
GDN_CONV = 4
GDN_CHUNK = 64
GDN_SOLVE_BASE = 8
HGRN_HEADS = 4
HGRN_DK = 128
HGRN_CHUNK = 16
SSD_HEADS = 8
SSD_HEAD_DIM = 64
SSD_GROUPS = 2
SSD_STATE = 128
SSD_CONV = 4
FFN_CONV = 3

HALO = 8
MXU_TILE = 256
VMEM_LIMIT = 56 * 1024 * 1024

_HI = lax.Precision.HIGHEST


def _mm(a, b):
    return jnp.dot(a.astype(BF16), b.astype(BF16), preferred_element_type=F32)


def _mm_nt(a, b):
    return lax.dot_general(a.astype(BF16), b.astype(BF16), (((1,), (1,)), ((), ())),
                           preferred_element_type=F32)


def _mm_tn(a, b):
    return lax.dot_general(a.astype(BF16), b.astype(BF16), (((0,), (0,)), ((), ())),
                           preferred_element_type=F32)


def _mm_hi(a, b):
    return jnp.dot(a, b, precision=_HI, preferred_element_type=F32)


def _split_exact(a):
    p1 = a.astype(BF16)
    r1 = a - p1.astype(F32)
    p2 = r1.astype(BF16)
    p3 = (r1 - p2.astype(F32)).astype(BF16)
    return p1, p2, p3


def _mask_mm(mask, a):
    mb = mask.astype(BF16)
    return sum(jnp.dot(mb, p, preferred_element_type=F32) for p in _split_exact(a))


def _mm_mask(a, mask):
    mb = mask.astype(BF16)
    return sum(jnp.dot(p, mb, preferred_element_type=F32) for p in _split_exact(a))


def _sigmoid(x):
    return jax.nn.sigmoid(x)


def _silu(x):
    return x * jax.nn.sigmoid(x)


def _softplus(x):
    return jnp.maximum(x, 0.0) + jnp.log(1.0 + jnp.exp(-jnp.abs(x)))


def _mod_norm(x, nw, shift, scale):
    y = x * lax.rsqrt(jnp.mean(x * x, axis=-1, keepdims=True) + EPS) * nw
    return y * (1.0 + scale) + shift


def _tri(n, strict=False):
    r = lax.broadcasted_iota(jnp.int32, (n, n), 0)
    c = lax.broadcasted_iota(jnp.int32, (n, n), 1)
    return (r > c) if strict else (r >= c)


def _cparams(sem):
    return pltpu.CompilerParams(dimension_semantics=sem, vmem_limit_bytes=VMEM_LIMIT)


def _const_spec(shape):
    nd = len(shape)
    return pl.BlockSpec(shape, lambda *_: (0,) * nd)


def _alternate(*gens):
    live = list(gens)
    while live:
        for gen in list(live):
            if next(gen, live) is live:
                live.remove(gen)


def _ada_kernel(c_ref, w_ref, b_ref, o_ref):
    c = c_ref[...]
    o_ref[0] = _mm_hi(_silu(c), w_ref[0]) + b_ref[0]


def _ada_call(c, w_ada, b_ada):
    depth, d, n = w_ada.shape
    bsz = c.shape[0]
    tn = 1536
    return pl.pallas_call(
        _ada_kernel,
        grid=(depth, n // tn),
        in_specs=[
            pl.BlockSpec((bsz, d), lambda l, j: (0, 0)),
            pl.BlockSpec((1, d, tn), lambda l, j: (l, 0, j)),
            pl.BlockSpec((1, 1, tn), lambda l, j: (l, 0, j)),
        ],
        out_specs=pl.BlockSpec((1, bsz, tn), lambda l, j: (l, 0, j)),
        out_shape=jax.ShapeDtypeStruct((depth, bsz, n), F32),
        compiler_params=_cparams(("arbitrary", "arbitrary")),
    )(c, w_ada, b_ada.reshape(depth, 1, n))


def _split3(a):
    hi = a.astype(BF16)
    lo = (a - hi.astype(F32)).astype(BF16)
    return hi, lo


def _dot3(a_hi, a_lo, b_hi, b_lo):
    f = lambda p, q: jnp.dot(p, q, preferred_element_type=F32)
    return f(a_hi, b_hi) + (f(a_lo, b_hi) + f(a_hi, b_lo))


def _unit_lower_inverse(ms, bd_mask, base):
    c = ms[0].shape[0]
    nb = ms[0].shape[1] // c
    f = lambda p, q: jnp.dot(p, q, preferred_element_type=F32)
    row = lax.broadcasted_iota(jnp.int32, (c, nb * c), 0)
    col = lax.broadcasted_iota(jnp.int32, (c, nb * c), 1) % c
    eye = jnp.where(row == col, 1.0, 0.0).astype(F32)

    def blockdiag(x):
        return jnp.concatenate([x] * nb, axis=0) * bd_mask

    def dot3(a_hi, a_lo, b_hi, b_lo):
        r = a_hi.shape[0]
        t = f(jnp.concatenate([a_hi, a_lo], axis=0), blockdiag(b_hi))
        return t[:r] + (t[r:] + f(a_hi, blockdiag(b_lo)))

    n_items = range(len(ms))
    n_steps = base.bit_length() - 1
    in_base = (row // base) == (col // base)
    ds = [jnp.where(in_base, m, 0.0) for m in ms]
    xs = [eye - d for d in ds]
    if n_steps > 1:
        qs = []
        for d in ds:
            hi, lo = _split3(d)
            qs.append(dot3(hi, lo, hi, lo))
        for step in range(1, n_steps):
            for i in n_items:
                if step == n_steps - 1:
                    p_hi, p_lo = _split3(xs[i])
                    q_hi, q_lo = _split3(qs[i])
                    xs[i] = xs[i] + dot3(p_hi, p_lo, q_hi, q_lo)
                else:
                    hi, lo = _split3(jnp.concatenate([qs[i], xs[i]], axis=0))
                    r = dot3(hi, lo, hi[:c], lo[:c])
                    qs[i] = r[:c]
                    xs[i] = xs[i] + r[c:]
            yield
    b = base
    while b < c:
        level = ((row // (2 * b)) == (col // (2 * b))) & ((row // b) != (col // b))
        splits = [_split3(x) for x in xs]
        ts = []
        for i in n_items:
            l_hi, l_lo = _split3(jnp.where(level, ms[i], 0.0))
            ts.append(dot3(splits[i][0], splits[i][1], l_hi, l_lo))
        yield
        for i in n_items:
            t_hi, t_lo = _split3(ts[i])
            xs[i] = xs[i] - dot3(t_hi, t_lo, splits[i][0], splits[i][1])
        yield
        b *= 2
    return xs


def _gdn_kernel(x0_ref, xa_ref, xb_ref, mod_ref, nw_ref, w_ref, wsm_ref, wsmt_ref, cw_ref,
                pcol_ref, prow_ref, gnw_ref, o_ref, ext_ref, st_ref, *sets, nb, rows):
    kw = GDN_HEADS * GDN_DK
    vw = GDN_HEADS * GDN_DV
    qkv_w = 2 * kw + vw
    c = GDN_CHUNK
    ts = nb * rows
    n_chunks = ts // c
    heads = range(GDN_HEADS)
    hc = GDN_HEADS * c
    set_a, set_b = sets[:7], sets[7:]
    wd = GDN_DV + GDN_DK

    m = mod_ref[...]
    nw = nw_ref[...]
    gnw = gnw_ref[...]
    row_i = lax.broadcasted_iota(jnp.int32, (c, hc), 0)
    lane_i = lax.broadcasted_iota(jnp.int32, (c, hc), 1)
    lane_head = lane_i // c
    col_i = lane_i - lane_head * c
    incl_bd = row_i >= col_i
    strict_bd = row_i > col_i

    def prepare(x_ref, dst):
        qg_s, kn_s, rhs_s, z_s, gcol_s, m_s, qk_s = dst
        h = _mod_norm(x_ref[...], nw, m[:, 0:1, :], m[:, 1:2, :])
        h = h.reshape(ts, h.shape[-1]).astype(BF16)
        yield
        proj = jnp.dot(h, w_ref[...], preferred_element_type=F32)
        z_s[...] = _silu(proj[:, qkv_w:])
        yield
        cw = cw_ref[...]
        accs = []
        for i in range(nb):
            p_i = proj[i * rows:(i + 1) * rows, :qkv_w]
            ext_ref[i, HALO:HALO + rows, :] = p_i
            acc = cw[GDN_CONV - 1:GDN_CONV] * p_i
            for t in range(GDN_CONV - 1):
                off = HALO - (GDN_CONV - 1) + t
                acc = acc + cw[t:t + 1] * ext_ref[i, off:off + rows, :]
            ext_ref[i, 0:HALO, :] = ext_ref[i, rows:rows + HALO, :]
            accs.append(acc)
            yield
        qkv = _silu(jnp.concatenate(accs, axis=0))

        sm = jnp.dot(h, wsm_ref[...], preferred_element_type=F32)
        smt = lax.dot_general(wsmt_ref[...], h, (((1,), (1,)), ((), ())), preferred_element_type=F32)
        pcol = pcol_ref[...]
        prow = prow_ref[...]
        g_cols = -jnp.exp(pcol[0:1]) * _softplus(sm + pcol[1:2])
        beta_cols = _sigmoid(sm)
        g_rows = -jnp.exp(prow[:, 0:1]) * _softplus(smt + prow[:, 1:2])
        rr = lax.broadcasted_iota(jnp.int32, (ts, ts), 0)
        cc = lax.broadcasted_iota(jnp.int32, (ts, ts), 1)
        same = (rr // c) == (cc // c)
        gc_all = _mask_mm(jnp.where(same & (rr >= cc), 1.0, 0.0), g_cols)
        gr_all = _mm_mask(g_rows, jnp.where(same & (rr <= cc), 1.0, 0.0))
        gcol_s[...] = gc_all
        yield
        for ci in range(n_chunks):
            r0 = ci * c
            g_col_bd = jnp.zeros((c, hc), F32)
            kks, qks = [], []
            for hd in heads:
                q = qkv[r0:r0 + c, hd * GDN_DK:(hd + 1) * GDN_DK]
                k = qkv[r0:r0 + c, kw + hd * GDN_DK:kw + (hd + 1) * GDN_DK]
                v = qkv[r0:r0 + c, 2 * kw + hd * GDN_DV:2 * kw + (hd + 1) * GDN_DV]
                q = q * lax.rsqrt(jnp.sum(q * q, axis=-1, keepdims=True) + EPS) * (GDN_DK ** -0.5)
                k = k * lax.rsqrt(jnp.sum(k * k, axis=-1, keepdims=True) + EPS)
                g_col = gc_all[r0:r0 + c, hd:hd + 1]
                beta = beta_cols[r0:r0 + c, GDN_HEADS + hd:GDN_HEADS + hd + 1]
                eg = jnp.exp(g_col)
                kb = k * beta
                qg_s[r0:r0 + c, hd * GDN_DK:(hd + 1) * GDN_DK] = q * eg
                kn_s[r0:r0 + c, hd * GDN_DK:(hd + 1) * GDN_DK] = k
                rhs_s[r0:r0 + c, hd * wd:hd * wd + GDN_DV] = v * beta
                rhs_s[r0:r0 + c, hd * wd + GDN_DV:(hd + 1) * wd] = kb * eg
                kks.append(_mm_nt(kb, k))
                qks.append(_mm_nt(q, k))
                g_col_bd = jnp.where(lane_head == hd, g_col, g_col_bd)
            g_row_bd = jnp.concatenate([gr_all[hd:hd + 1, r0:r0 + c] for hd in heads], axis=1)
            decay = jnp.where(incl_bd, jnp.exp(g_col_bd - g_row_bd), 0.0)
            m_s[r0:r0 + c, :] = jnp.where(strict_bd, jnp.concatenate(kks, axis=1) * decay, 0.0)
            qk_s[r0:r0 + c, :] = jnp.concatenate(qks, axis=1) * decay
            yield

    def finish(src, out_rows):
        qg_s, kn_s, rhs_s, z_s, gcol_s, m_s, qk_s = src
        bd_mask = jnp.where(lax.broadcasted_iota(jnp.int32, (hc, hc), 0) // c
                            == lax.broadcasted_iota(jnp.int32, (hc, hc), 1) // c, 1.0, 0.0).astype(BF16)
        ms = [m_s[ci * c:(ci + 1) * c, :] for ci in range(n_chunks)]
        invs = yield from _unit_lower_inverse(ms, bd_mask, GDN_SOLVE_BASE)
        pairs = [(ci, hd) for ci in range(n_chunks) for hd in heads]
        uw = {}
        for ci, hd in pairs:
            r0 = ci * c
            rhs_hi, rhs_lo = _split3(rhs_s[r0:r0 + c, hd * wd:(hd + 1) * wd])
            inv_hi, inv_lo = _split3(invs[ci][:, hd * c:(hd + 1) * c])
            uw[ci, hd] = _dot3(inv_hi, inv_lo, rhs_hi, rhs_lo)
            if hd == GDN_HEADS - 1:
                yield
        cps = rows // c
        items = [(i, hd) for i in range(nb) for hd in heads]
        for j in range(cps):
            ci = {i: i * cps + j for i in range(nb)}
            hsl = {hd: slice(hd * GDN_DK, (hd + 1) * GDN_DK) for hd in heads}
            st = {(i, hd): st_ref[i, hd] for i, hd in items}
            ws = {(i, hd): _mm(jnp.concatenate([uw[ci[i], hd][:, GDN_DV:],
                                                qg_s[ci[i] * c:(ci[i] + 1) * c, hsl[hd]]], axis=0), st[i, hd])
                  for i, hd in items}
            yield
            v_new = {(i, hd): uw[ci[i], hd][:, :GDN_DV] - ws[i, hd][:c] for i, hd in items}
            for i, hd in items:
                r0 = ci[i] * c
                g_col = gcol_s[r0:r0 + c, hd:hd + 1]
                g_end = g_col[c - 1:c, :]
                k_end = kn_s[r0:r0 + c, hsl[hd]] * jnp.exp(g_end - g_col)
                st_ref[i, hd] = st[i, hd] * jnp.exp(g_end) + _mm_tn(k_end, v_new[i, hd])
            yield
            for i, hd in items:
                r0 = ci[i] * c
                o = ws[i, hd][c:] + _mm(qk_s[r0:r0 + c, hd * c:(hd + 1) * c], v_new[i, hd])
                o = o * lax.rsqrt(jnp.mean(o * o, axis=-1, keepdims=True) + EPS) * gnw * z_s[r0:r0 + c, hsl[hd]]
                o_ref[i, out_rows + j * c:out_rows + (j + 1) * c, hsl[hd]] = o.astype(o_ref.dtype)
                if hd == GDN_HEADS - 1:
                    yield

    @pl.when(pl.program_id(1) == 0)
    def _():
        st_ref[...] = jnp.zeros_like(st_ref)
        ext_ref[:, 0:HALO, :] = jnp.zeros((nb, HALO, qkv_w), F32)
        _alternate(prepare(x0_ref, set_a))

    _alternate(finish(set_a, 0), prepare(xa_ref, set_b))
    _alternate(finish(set_b, rows), prepare(xb_ref, set_a))


def _gdn_call(x, mod, nw, w, wsm, wsmt, cw, pcol, prow, gnw, nb, rows):
    bsz, s, d = x.shape
    kw = GDN_HEADS * GDN_DK
    vw = GDN_HEADS * GDN_DV
    qkv_w = 2 * kw + vw
    ts = nb * rows
    n_tiles = s // rows
    assert s % (2 * rows) == 0 and bsz % nb == 0, "the GDN kernel takes two tiles per grid step"
    hc = GDN_HEADS * GDN_CHUNK
    one_set = [pltpu.VMEM((ts, kw), F32), pltpu.VMEM((ts, kw), F32),
               pltpu.VMEM((ts, vw + kw), F32), pltpu.VMEM((ts, vw), F32),
               pltpu.VMEM((ts, 128), F32), pltpu.VMEM((ts, hc), F32), pltpu.VMEM((ts, hc), F32)]
    tile = lambda f: pl.BlockSpec((nb, rows, d), f)
    return pl.pallas_call(
        functools.partial(_gdn_kernel, nb=nb, rows=rows),
        grid=(bsz // nb, n_tiles // 2),
        in_specs=[
            tile(lambda b, j: (b, 0, 0)),
            tile(lambda b, j: (b, 2 * j + 1, 0)),
            tile(lambda b, j: (b, jnp.minimum(2 * j + 2, n_tiles - 1), 0)),
            pl.BlockSpec((nb, 6, d), lambda b, j: (b, 0, 0)),
            _const_spec(nw.shape), _const_spec(w.shape), _const_spec(wsm.shape),
            _const_spec(wsmt.shape), _const_spec(cw.shape), _const_spec(pcol.shape),
            _const_spec(prow.shape), _const_spec(gnw.shape),
        ],
        out_specs=pl.BlockSpec((nb, 2 * rows, vw), lambda b, j: (b, j, 0)),
        out_shape=jax.ShapeDtypeStruct((bsz, s, vw), BF16),
        scratch_shapes=[pltpu.VMEM((nb, rows + HALO, qkv_w), F32),
                        pltpu.VMEM((nb, GDN_HEADS, GDN_DK, GDN_DV), F32)] + one_set + one_set,
        compiler_params=_cparams(("parallel", "arbitrary")),
    )(x, x, x, mod, nw, w, wsm, wsmt, cw, pcol, prow, gnw)


def _group_cumsum_rows(x, group):
    rows = lax.broadcasted_iota(jnp.int32, x.shape, 0) % group
    d = 1
    while d < group:
        x = x + jnp.where(rows >= d, pltpu.roll(x, d, axis=0), 0.0)
        d *= 2
    return x


def _hgrn_kernel(x_ref, mod_ref, nw_ref, w_ref, lbp_ref, gnw_ref, o_ref, *scratch, ts, layer, unit, group):
    kw = HGRN_HEADS * HGRN_DK
    c = HGRN_CHUNK
    n_sub = unit // c
    n_units = ts // unit
    sets = (scratch[0:7], scratch[7:14])
    st_ref = scratch[14]

    @pl.when(pl.program_id(1) == 0)
    def _():
        st_ref[...] = jnp.zeros_like(st_ref)

    m = mod_ref[0]
    nw = nw_ref[...]
    gnw = gnw_ref[...]
    incl = _tri(c)
    heads = range(HGRN_HEADS)
    sl = [slice(hd * HGRN_DK, (hd + 1) * HGRN_DK) for hd in heads]

    lbp = lbp_ref[...]
    e = jnp.exp(lbp - jnp.max(lbp, axis=0, keepdims=True))
    soft = e / jnp.sum(e, axis=0, keepdims=True)
    lb = jnp.zeros((1, kw), F32)
    for i in range(1, layer + 1):
        lb = lb + soft[i:i + 1]

    def prepare(u_idx, dst):
        qs_ref, ks_ref, qg_ref, ke_ref, v_ref, ge_ref, z_ref = dst
        rows = pl.ds(pl.multiple_of(u_idx * unit, unit), unit)
        h = _mod_norm(x_ref[0, rows, :], nw, m[0:1], m[1:2]).astype(BF16)
        yield
        f_raw = jnp.dot(h, w_ref[:, kw:2 * kw], preferred_element_type=F32)
        sig = _sigmoid(f_raw)
        g = _group_cumsum_rows(jnp.log(lb + (1.0 - lb) * sig), c).reshape(n_sub, c, kw)
        g_mid = g[:, c // 2 - 1:c // 2, :]
        g_last = g[:, c - 1:c, :]
        g_end = jnp.exp(g_last).reshape(n_sub, kw)
        for hd in heads:
            ge_ref[hd] = g_end[:, sl[hd]].T
        yield
        k = ((1.0 - lb) * _sigmoid(-f_raw)).reshape(n_sub, c, kw)
        ks_ref[...] = (k * jnp.exp(g_mid - g)).reshape(unit, kw).astype(BF16)
        ke_ref[...] = (k * jnp.exp(g_last - g)).reshape(unit, kw).astype(BF16)
        yield
        q = _silu(jnp.dot(h, w_ref[:, :kw], preferred_element_type=F32)).reshape(n_sub, c, kw)
        qs_ref[...] = (q * jnp.exp(g - g_mid)).reshape(unit, kw).astype(BF16)
        qg_ref[...] = (q * jnp.exp(g)).reshape(unit, kw).astype(BF16)
        yield
        v_ref[...] = jnp.dot(h, w_ref[:, 2 * kw:3 * kw], preferred_element_type=F32).astype(BF16)
        yield
        z_ref[...] = _silu(jnp.dot(h, w_ref[:, 3 * kw:], preferred_element_type=F32))

    def recur(u_idx, src):
        qs_ref, ks_ref, qg_ref, ke_ref, v_ref, ge_ref, z_ref = src
        base = pl.multiple_of(u_idx * unit, unit)
        for g0 in range(0, n_sub, group):
            subs = range(g0, g0 + group)
            rows = {u: slice(u * c, (u + 1) * c) for u in subs}
            pairs = [(u, hd) for u in subs for hd in heads]
            vv = {u: v_ref[rows[u], :] for u in subs}
            sc = {(u, hd): _mm_nt(qs_ref[rows[u], sl[hd]], ks_ref[rows[u], sl[hd]]) for u, hd in pairs}
            yield
            kv = {(u, hd): _mm_tn(ke_ref[rows[u], sl[hd]], vv[u][:, sl[hd]]) for u, hd in pairs}
            yield
            attn = {(u, hd): jnp.where(incl, sc[u, hd], 0.0).astype(BF16) for u, hd in pairs}
            yield
            st = [st_ref[hd] for hd in heads]
            oo = {}
            for u in subs:
                for hd in heads:
                    lhs = jnp.concatenate([qg_ref[rows[u], sl[hd]], attn[u, hd]], axis=1)
                    rhs = jnp.concatenate([st[hd].astype(BF16), vv[u][:, sl[hd]]], axis=0)
                    oo[u, hd] = jnp.dot(lhs, rhs, preferred_element_type=F32)
                    st[hd] = st[hd] * ge_ref[hd][:, u:u + 1] + kv[u, hd]
                if u == g0 + group // 2 - 1:
                    yield
            for hd in heads:
                st_ref[hd] = st[hd]
            yield
            for u, hd in pairs:
                o = oo[u, hd]
                o = o * lax.rsqrt(jnp.mean(o * o, axis=-1, keepdims=True) + EPS) * gnw * z_ref[rows[u], sl[hd]]
                o_ref[0, pl.ds(base + u * c, c), sl[hd]] = o.astype(o_ref.dtype)
            yield

    _alternate(prepare(0, sets[0]))

    def body(i, carry):
        _alternate(recur(2 * i, sets[0]), prepare(2 * i + 1, sets[1]))
        _alternate(recur(2 * i + 1, sets[1]), prepare(jnp.minimum(2 * i + 2, n_units - 1), sets[0]))
        return carry

    lax.fori_loop(0, n_units // 2, body, 0)


def _hgrn_call(x, mod, nw, w, lbp, gnw, ts, layer):
    bsz, s, d = x.shape
    kw = HGRN_HEADS * HGRN_DK
    group = 8
    unit = min(256, ts // 2)
    assert s % ts == 0 and ts % (2 * unit) == 0 and unit % (group * HGRN_CHUNK) == 0
    one_set = ([pltpu.VMEM((unit, kw), BF16) for _ in range(5)]
               + [pltpu.VMEM((HGRN_HEADS, HGRN_DK, unit // HGRN_CHUNK), F32), pltpu.VMEM((unit, kw), F32)])
    return pl.pallas_call(
        functools.partial(_hgrn_kernel, ts=ts, layer=layer, unit=unit, group=group),
        grid=(bsz, s // ts),
        in_specs=[
            pl.BlockSpec((1, ts, d), lambda b, j: (b, j, 0)),
            pl.BlockSpec((1, 6, d), lambda b, j: (b, 0, 0)),
            _const_spec(nw.shape), _const_spec(w.shape), _const_spec(lbp.shape),
            _const_spec(gnw.shape),
        ],
        out_specs=pl.BlockSpec((1, ts, kw), lambda b, j: (b, j, 0)),
        out_shape=jax.ShapeDtypeStruct((bsz, s, kw), BF16),
        scratch_shapes=one_set + one_set + [pltpu.VMEM((HGRN_HEADS, HGRN_DK, HGRN_DK), F32)],
        compiler_params=_cparams(("parallel", "arbitrary")),
    )(x, mod, nw, w, lbp, gnw)


def _ssd_kernel(x0_ref, xa_ref, xb_ref, mod_ref, nw_ref, w_ref, wdt_ref, wdtt_ref, cw_ref, cb_ref,
                pcol_ref, prow_ref, dsk_ref, gnw_ref, o_ref, ext_ref, st_ref, *sets, ts):
    inner = SSD_HEADS * SSD_HEAD_DIM
    bc_w = 2 * SSD_GROUPS * SSD_STATE
    xbc_w = inner + bc_w
    hpg = SSD_HEADS // SSD_GROUPS
    p = SSD_HEAD_DIM
    set_a, set_b = sets[:7], sets[7:]

    m = mod_ref[0]
    nw = nw_ref[...]
    incl = _tri(ts)

    def prepare(x_ref, dst):
        xs_s, bc_s, zs_s, dtc_s, acc_s, acr_s, cb_s = dst
        h = _mod_norm(x_ref[0], nw, m[0:1], m[1:2]).astype(BF16)
        yield
        proj = jnp.dot(h, w_ref[...], preferred_element_type=F32)
        zs_s[...] = _silu(proj[:, :inner])
        yield
        ext_ref[HALO:HALO + ts, :] = proj[:, inner:]
        cw = cw_ref[...]
        acc = cw[SSD_CONV - 1:SSD_CONV] * proj[:, inner:] + cb_ref[...]
        for i in range(SSD_CONV - 1):
            off = HALO - (SSD_CONV - 1) + i
            acc = acc + cw[i:i + 1] * ext_ref[off:off + ts, :]
        ext_ref[0:HALO, :] = ext_ref[ts:ts + HALO, :]
        xbc = _silu(acc)
        xs_s[...] = xbc[:, :inner]
        bc_s[...] = xbc[:, inner:]
        yield
        pcol = pcol_ref[...]
        prow = prow_ref[...]
        dt_cols = _softplus(jnp.dot(h, wdt_ref[...], preferred_element_type=F32) + pcol[1:2])
        dt_rows = _softplus(lax.dot_general(wdtt_ref[...], h, (((1,), (1,)), ((), ())),
                                            preferred_element_type=F32) + prow[:, 1:2])
        low = jnp.where(incl, 1.0, 0.0).astype(F32)
        upp = jnp.where(lax.broadcasted_iota(jnp.int32, (ts, ts), 0)
                        <= lax.broadcasted_iota(jnp.int32, (ts, ts), 1), 1.0, 0.0).astype(F32)
        dtc_s[...] = dt_cols
        acc_s[...] = _mask_mm(low, dt_cols * (-jnp.exp(pcol[0:1])))
        acr_s[...] = _mm_mask(dt_rows * (-jnp.exp(prow[:, 0:1])), upp)
        yield
        for g in range(SSD_GROUPS):
            bm = xbc[:, inner + g * SSD_STATE:inner + (g + 1) * SSD_STATE]
            cm = xbc[:, inner + (SSD_GROUPS + g) * SSD_STATE:inner + (SSD_GROUPS + g + 1) * SSD_STATE]
            cb_s[g] = _mm_nt(cm, bm)
            yield

    def finish(src, out_rows):
        xs_s, bc_s, zs_s, dtc_s, acc_s, acr_s, cb_s = src
        first = lax.broadcasted_iota(jnp.int32, (ts, 2 * p), 1) < p
        first_rows = lax.broadcasted_iota(jnp.int32, (2 * p, SSD_STATE), 0) < p
        ys = []
        for g in range(SSD_GROUPS):
            bm = bc_s[:, g * SSD_STATE:(g + 1) * SSD_STATE]
            cm = bc_s[:, (SSD_GROUPS + g) * SSD_STATE:(SSD_GROUPS + g + 1) * SSD_STATE]
            cb = cb_s[g]
            for pr in range(hpg // 2):
                h0 = g * hpg + 2 * pr
                h1 = h0 + 1
                a0, a1 = acc_s[:, h0:h0 + 1], acc_s[:, h1:h1 + 1]
                seg0 = jnp.where(incl, jnp.exp(a0 - acr_s[h0:h0 + 1, :]), 0.0)
                seg1 = jnp.where(incl, jnp.exp(a1 - acr_s[h1:h1 + 1, :]), 0.0)
                xdt = xs_s[:, h0 * p:(h1 + 1) * p] * jnp.where(first, dtc_s[:, h0:h0 + 1], dtc_s[:, h1:h1 + 1])
                st = st_ref[g * (hpg // 2) + pr]
                y = (_mm(cb * seg0, jnp.where(first, xdt, 0.0)) + _mm(cb * seg1, jnp.where(first, 0.0, xdt))
                     + _mm_nt(cm, st) * jnp.where(first, jnp.exp(a0), jnp.exp(a1)))
                yield
                last0, last1 = a0[ts - 1:ts, :], a1[ts - 1:ts, :]
                w_in = jnp.where(first, jnp.exp(last0 - a0), jnp.exp(last1 - a1))
                st_ref[g * (hpg // 2) + pr] = (st * jnp.where(first_rows, jnp.exp(last0), jnp.exp(last1))
                                               + _mm_tn(xdt * w_in, bm))
                ys.append(y)
                yield
        gnw = gnw_ref[...]
        gw = hpg * p
        for g in range(SSD_GROUPS):
            cols = slice(g * gw, (g + 1) * gw)
            yg = jnp.concatenate(ys[g * (hpg // 2):(g + 1) * (hpg // 2)], axis=1)
            yg = (yg + dsk_ref[:, cols] * xs_s[:, cols]) * zs_s[:, cols]
            yg = yg * lax.rsqrt(jnp.mean(yg * yg, axis=-1, keepdims=True) + EPS) * gnw[:, cols]
            o_ref[0, out_rows:out_rows + ts, cols] = yg.astype(o_ref.dtype)
            yield

    @pl.when(pl.program_id(1) == 0)
    def _():
        st_ref[...] = jnp.zeros_like(st_ref)
        ext_ref[0:HALO, :] = jnp.zeros((HALO, xbc_w), F32)
        _alternate(prepare(x0_ref, set_a))

    _alternate(finish(set_a, 0), prepare(xa_ref, set_b))
    _alternate(finish(set_b, ts), prepare(xb_ref, set_a))


def _ssd_call(x, mod, nw, w, wdt, wdtt, cw, cb, pcol, prow, dsk, gnw, ts):
    bsz, s, d = x.shape
    inner = SSD_HEADS * SSD_HEAD_DIM
    bc_w = 2 * SSD_GROUPS * SSD_STATE
    xbc_w = inner + bc_w
    n_tiles = s // ts
    assert s % (2 * ts) == 0, "the SSD kernel takes two tiles per grid step"
    one_set = [pltpu.VMEM((ts, inner), F32), pltpu.VMEM((ts, bc_w), F32), pltpu.VMEM((ts, inner), F32),
               pltpu.VMEM((ts, 128), F32), pltpu.VMEM((ts, 128), F32), pltpu.VMEM((8, ts), F32),
               pltpu.VMEM((SSD_GROUPS, ts, ts), F32)]
    tile = lambda f: pl.BlockSpec((1, ts, d), f)
    return pl.pallas_call(
        functools.partial(_ssd_kernel, ts=ts),
        grid=(bsz, n_tiles // 2),
        in_specs=[
            tile(lambda b, j: (b, 0, 0)),
            tile(lambda b, j: (b, 2 * j + 1, 0)),
            tile(lambda b, j: (b, jnp.minimum(2 * j + 2, n_tiles - 1), 0)),
            pl.BlockSpec((1, 6, d), lambda b, j: (b, 0, 0)),
            _const_spec(nw.shape), _const_spec(w.shape), _const_spec(wdt.shape),
            _const_spec(wdtt.shape), _const_spec(cw.shape), _const_spec(cb.shape),
            _const_spec(pcol.shape), _const_spec(prow.shape), _const_spec(dsk.shape),
            _const_spec(gnw.shape),
        ],
        out_specs=pl.BlockSpec((1, 2 * ts, inner), lambda b, j: (b, j, 0)),
        out_shape=jax.ShapeDtypeStruct((bsz, s, inner), BF16),
        scratch_shapes=[pltpu.VMEM((ts + HALO, xbc_w), F32),
                        pltpu.VMEM((SSD_HEADS // 2, 2 * SSD_HEAD_DIM, SSD_STATE), F32)] + one_set + one_set,
        compiler_params=_cparams(("parallel", "arbitrary")),
    )(x, x, x, mod, nw, w, wdt, wdtt, cw, cb, pcol, prow, dsk, gnw)


def _merge_kernel(x_ref, mod_ref, nw_ref, oa_ref, ob_ref, oc_ref, wg_ref, wa_ref, wb_ref, wc_ref,
                  wo_ref, o_ref):
    d = x_ref.shape[-1]
    x = x_ref[0]
    m = mod_ref[0]
    h = _mod_norm(x, nw_ref[...], m[0:1], m[1:2]).astype(BF16)
    gates = _sigmoid(jnp.dot(h, wg_ref[...], preferred_element_type=F32))
    merged = (gates[:, :d] * jnp.dot(oa_ref[0], wa_ref[...], preferred_element_type=F32)
              + gates[:, d:2 * d] * jnp.dot(ob_ref[0], wb_ref[...], preferred_element_type=F32)
              + gates[:, 2 * d:] * jnp.dot(oc_ref[0], wc_ref[...], preferred_element_type=F32))
    mix = jnp.dot(merged.astype(BF16), wo_ref[...], preferred_element_type=F32)
    o_ref[0] = x + m[2:3] * mix


def _merge_call(x, mod, nw, oa, ob, oc, wg, wa, wb, wc, wo, tm):
    bsz, s, d = x.shape
    tok = lambda w: pl.BlockSpec((1, tm, w), lambda b, j: (b, j, 0))
    return pl.pallas_call(
        _merge_kernel,
        grid=(bsz, s // tm),
        in_specs=[
            tok(d), pl.BlockSpec((1, 6, d), lambda b, j: (b, 0, 0)), _const_spec(nw.shape),
            tok(oa.shape[-1]), tok(ob.shape[-1]), tok(oc.shape[-1]),
            _const_spec(wg.shape), _const_spec(wa.shape), _const_spec(wb.shape),
            _const_spec(wc.shape), _const_spec(wo.shape),
        ],
        out_specs=tok(d),
        out_shape=jax.ShapeDtypeStruct((bsz, s, d), F32),
        compiler_params=_cparams(("parallel", "parallel")),
    )(x, mod, nw, oa, ob, oc, wg, wa, wb, wc, wo)


def _ffn_kernel(x_ref, mod_ref, nw_ref, wup_ref, cw_ref, cb_ref, wdn_ref, fnw_ref, o_ref,
                ext_ref, tail_ref, *, tm, splits, final):
    hidden = wdn_ref.shape[0]

    @pl.when(pl.program_id(1) == 0)
    def _():
        tail_ref[...] = jnp.zeros_like(tail_ref)

    x = x_ref[0]
    m = mod_ref[0]
    h = _mod_norm(x, nw_ref[...], m[3:4], m[4:5]).astype(BF16)

    def conv_part(c0, fc):
        u = jnp.dot(h, wup_ref[:, c0:c0 + fc], preferred_element_type=F32)
        ext_ref[0:HALO, 0:fc] = tail_ref[:, c0:c0 + fc]
        ext_ref[HALO:HALO + tm, 0:fc] = u
        cw = cw_ref[:, c0:c0 + fc]
        acc = cw[FFN_CONV - 1:FFN_CONV] * u + cb_ref[:, c0:c0 + fc]
        for i in range(FFN_CONV - 1):
            off = HALO - (FFN_CONV - 1) + i
            acc = acc + cw[i:i + 1] * ext_ref[off:off + tm, 0:fc]
        tail_ref[:, c0:c0 + fc] = ext_ref[tm:tm + HALO, 0:fc]
        return acc

    y = jnp.zeros_like(x)
    for c0, fc in splits:
        gate = conv_part(c0, fc)
        val = conv_part(hidden + c0, fc)
        act = (_silu(gate) * val).astype(BF16)
        y = y + jnp.dot(act, wdn_ref[c0:c0 + fc, :], preferred_element_type=F32)
    out = x + m[5:6] * y
    if final:
        out = out * lax.rsqrt(jnp.mean(out * out, axis=-1, keepdims=True) + EPS) * fnw_ref[...]
    o_ref[0] = out


def _ffn_call(x, mod, nw, wup, cw, cb, wdn, fnw, tm, final):
    bsz, s, d = x.shape
    hidden = wdn.shape[0]
    first = -(-(hidden // MXU_TILE) // 2) * MXU_TILE
    splits = ((0, first), (first, hidden - first))
    tok = pl.BlockSpec((1, tm, d), lambda b, j: (b, j, 0))
    single = lambda a: pl.BlockSpec(a.shape, lambda *_: (0,) * a.ndim, pipeline_mode=pl.Buffered(1))
    return pl.pallas_call(
        functools.partial(_ffn_kernel, tm=tm, splits=splits, final=final),
        grid=(bsz, s // tm),
        in_specs=[
            tok, pl.BlockSpec((1, 6, d), lambda b, j: (b, 0, 0)), _const_spec(nw.shape),
            single(wup), _const_spec(cw.shape), _const_spec(cb.shape), single(wdn),
            _const_spec(fnw.shape),
        ],
        out_specs=tok,
        out_shape=jax.ShapeDtypeStruct((bsz, s, d), F32),
        scratch_shapes=[pltpu.VMEM((tm + HALO, first), F32),
                        pltpu.VMEM((HALO, 2 * hidden), F32)],
        compiler_params=_cparams(("parallel", "arbitrary")),
    )(x, mod, nw, wup, cw, cb, wdn, fnw)


def _pad_lanes(a, width=128):
    return jnp.pad(a, ((0, 0), (0, width - a.shape[-1])))


def kernel(x, c, w_ada, b_ada, norm1_w, w_in, gdn_conv_w, gdn_a_log, gdn_dt_bias, gdn_norm_w,
           hgrn_lb_param, hgrn_norm_w, ssd_conv_w, ssd_conv_b, ssd_a_log, ssd_dt_bias, ssd_d,
           ssd_norm_w, w_br_a, w_br_b, w_br_c, w_out, norm2_w, ffn_w_up, ffn_conv_w, ffn_conv_b,
           ffn_w_down, final_norm_w):
    bsz, s, d = x.shape
    depth = w_in.shape[0]
    gk = GDN_HEADS * GDN_DK
    gv = GDN_HEADS * GDN_DV
    hk = HGRN_HEADS * HGRN_DK
    inner = SSD_HEADS * SSD_HEAD_DIM
    xbc_w = inner + 2 * SSD_GROUPS * SSD_STATE

    sizes = (2 * gk + gv, GDN_HEADS, GDN_HEADS, gv, hk, hk, hk, hk, inner, xbc_w, SSD_HEADS, 3 * d)
    offs = [0]
    for sz in sizes:
        offs.append(offs[-1] + sz)
    (o_qkv, o_a, o_b, o_z, o_hq, _, _, _, o_sz, o_xbc, o_dt, o_gate, o_end) = offs

    nb_gdn = 4 if bsz % 4 == 0 else (2 if bsz % 2 == 0 else 1)
    rows_gdn = min(max(GDN_CHUNK, 256 // nb_gdn), s)
    ts_hgrn = min(2048, s)
    ts_ssd = min(256, s)
    tm_merge = min(512, s)
    tm_ffn = min(512, s)

    mod = _ada_call(c, w_ada, b_ada).reshape(depth, bsz, 6, d)
    fnw = final_norm_w.reshape(1, d)

    for l in range(depth):
        wl = w_in[l]
        mod_l = mod[l]
        nw1 = norm1_w[l].reshape(1, d)

        w_gdn = jnp.concatenate([wl[:, o_qkv:o_a], wl[:, o_z:o_hq]], axis=1).astype(BF16)
        w_ab = wl[:, o_a:o_z]
        o_ga = _gdn_call(
            x, mod_l, nw1, w_gdn, _pad_lanes(w_ab).astype(BF16), w_ab.T.astype(BF16),
            gdn_conv_w[l],
            _pad_lanes(jnp.stack([gdn_a_log[l], gdn_dt_bias[l]])),
            jnp.pad(jnp.stack([gdn_a_log[l], gdn_dt_bias[l]], axis=1), ((0, GDN_HEADS), (0, 0))),
            gdn_norm_w[l].reshape(1, GDN_DV), nb_gdn, rows_gdn)

        o_hg = _hgrn_call(x, mod_l, nw1, wl[:, o_hq:o_sz].astype(BF16), hgrn_lb_param,
                          hgrn_norm_w[l].reshape(1, HGRN_DK), ts_hgrn, l)

        w_dt = wl[:, o_dt:o_gate]
        o_sd = _ssd_call(
            x, mod_l, nw1, wl[:, o_sz:o_dt].astype(BF16), _pad_lanes(w_dt).astype(BF16),
            w_dt.T.astype(BF16), ssd_conv_w[l], ssd_conv_b[l].reshape(1, xbc_w),
            _pad_lanes(jnp.stack([ssd_a_log[l], ssd_dt_bias[l]])),
            jnp.stack([ssd_a_log[l], ssd_dt_bias[l]], axis=1),
            jnp.repeat(ssd_d[l], SSD_HEAD_DIM).reshape(1, inner),
            ssd_norm_w[l].reshape(1, inner), ts_ssd)

        x = _merge_call(x, mod_l, nw1, o_ga, o_hg, o_sd, wl[:, o_gate:o_end].astype(BF16),
                        w_br_a[l].astype(BF16), w_br_b[l].astype(BF16), w_br_c[l].astype(BF16),
                        w_out[l].astype(BF16), tm_merge)

        x = _ffn_call(x, mod_l, norm2_w[l].reshape(1, d), ffn_w_up[l].astype(BF16), ffn_conv_w[l],
                      ffn_conv_b[l].reshape(1, -1), ffn_w_down[l].astype(BF16), fnw, tm_ffn,
                      final=(l == depth - 1))
    return x
```

```python
import functools

import jax
import jax.numpy as jnp
from jax import lax
from jax.experimental import pallas as pl
from jax.experimental.pallas import tpu as pltpu

F32 = jnp.float32
BF16 = jnp.bfloat16
EPS = 1e-6

GDN_HEADS = 4
GDN_DK = 128
GDN_DV = 128
GDN_CONV = 4
GDN_CHUNK = 64
GDN_SOLVE_BASE = 8
HGRN_HEADS = 4
HGRN_DK = 128
HGRN_CHUNK = 16
SSD_HEADS = 8
SSD_HEAD_DIM = 64
SSD_GROUPS = 2
SSD_STATE = 128
SSD_CONV = 4
FFN_CONV = 3

HALO = 8
MXU_TILE = 256
VMEM_LIMIT = 56 * 1024 * 1024

_HI = lax.Precision.HIGHEST


def _mm(a, b):
    return jnp.dot(a.astype(BF16), b.astype(BF16), preferred_element_type=F32)


def _mm_nt(a, b):
    return lax.dot_general(a.astype(BF16), b.astype(BF16), (((1,), (1,)), ((), ())),
                           preferred_element_type=F32)


def _mm_tn(a, b):
    return lax.dot_general(a.astype(BF16), b.astype(BF16), (((0,), (0,)), ((), ())),
                           preferred_element_type=F32)


def _mm_hi(a, b):
    return jnp.dot(a, b, precision=_HI, preferred_element_type=F32)


def _split_exact(a):
    p1 = a.astype(BF16)
    r1 = a - p1.astype(F32)
    p2 = r1.astype(BF16)
    p3 = (r1 - p2.astype(F32)).astype(BF16)
    return p1, p2, p3


def _mask_mm(mask, a):
    mb = mask.astype(BF16)
    return sum(jnp.dot(mb, p, preferred_element_type=F32) for p in _split_exact(a))


def _mm_mask(a, mask):
    mb = mask.astype(BF16)
    return sum(jnp.dot(p, mb, preferred_element_type=F32) for p in _split_exact(a))


def _sigmoid(x):
    return jax.nn.sigmoid(x)


def _silu(x):
    return x * jax.nn.sigmoid(x)


def _softplus(x):
    return jnp.maximum(x, 0.0) + jnp.log(1.0 + jnp.exp(-jnp.abs(x)))


def _mod_norm(x, nw, shift, scale):
    y = x * lax.rsqrt(jnp.mean(x * x, axis=-1, keepdims=True) + EPS) * nw
    return y * (1.0 + scale) + shift


def _tri(n, strict=False):
    r = lax.broadcasted_iota(jnp.int32, (n, n), 0)
    c = lax.broadcasted_iota(jnp.int32, (n, n), 1)
    return (r > c) if strict else (r >= c)


def _cparams(sem):
    return pltpu.CompilerParams(dimension_semantics=sem, vmem_limit_bytes=VMEM_LIMIT)


def _const_spec(shape):
    nd = len(shape)
    return pl.BlockSpec(shape, lambda *_: (0,) * nd)


def _alternate(*gens):
    live = list(gens)
    while live:
        for gen in list(live):
            if next(gen, live) is live:
                live.remove(gen)


def _ada_kernel(c_ref, w_ref, b_ref, o_ref):
    c = c_ref[...]
    o_ref[0] = _mm_hi(_silu(c), w_ref[0]) + b_ref[0]


def _ada_call(c, w_ada, b_ada):
    depth, d, n = w_ada.shape
    bsz = c.shape[0]
    tn = 1536
    return pl.pallas_call(
        _ada_kernel,
        grid=(depth, n // tn),
        in_specs=[
            pl.BlockSpec((bsz, d), lambda l, j: (0, 0)),
            pl.BlockSpec((1, d, tn), lambda l, j: (l, 0, j)),
            pl.BlockSpec((1, 1, tn), lambda l, j: (l, 0, j)),
        ],
        out_specs=pl.BlockSpec((1, bsz, tn), lambda l, j: (l, 0, j)),
        out_shape=jax.ShapeDtypeStruct((depth, bsz, n), F32),
        compiler_params=_cparams(("arbitrary", "arbitrary")),
    )(c, w_ada, b_ada.reshape(depth, 1, n))


def _split3(a):
    hi = a.astype(BF16)
    lo = (a - hi.astype(F32)).astype(BF16)
    return hi, lo


def _dot3(a_hi, a_lo, b_hi, b_lo):
    f = lambda p, q: jnp.dot(p, q, preferred_element_type=F32)
    return f(a_hi, b_hi) + (f(a_lo, b_hi) + f(a_hi, b_lo))


def _unit_lower_inverse(ms, bd_mask, base):
    c = ms[0].shape[0]
    nb = ms[0].shape[1] // c
    f = lambda p, q: jnp.dot(p, q, preferred_element_type=F32)
    row = lax.broadcasted_iota(jnp.int32, (c, nb * c), 0)
    col = lax.broadcasted_iota(jnp.int32, (c, nb * c), 1) % c
    eye = jnp.where(row == col, 1.0, 0.0).astype(F32)

    def blockdiag(x):
        return jnp.concatenate([x] * nb, axis=0) * bd_mask

    def dot3(a_hi, a_lo, b_hi, b_lo):
        r = a_hi.shape[0]
        t = f(jnp.concatenate([a_hi, a_lo], axis=0), blockdiag(b_hi))
        return t[:r] + (t[r:] + f(a_hi, blockdiag(b_lo)))

    n_items = range(len(ms))
    n_steps = base.bit_length() - 1
    in_base = (row // base) == (col // base)
    ds = [jnp.where(in_base, m, 0.0) for m in ms]
    xs = [eye - d for d in ds]
    if n_steps > 1:
        qs = []
        for d in ds:
            hi, lo = _split3(d)
            qs.append(dot3(hi, lo, hi, lo))
        for step in range(1, n_steps):
            for i in n_items:
                if step == n_steps - 1:
                    p_hi, p_lo = _split3(xs[i])
                    q_hi, q_lo = _split3(qs[i])
                    xs[i] = xs[i] + dot3(p_hi, p_lo, q_hi, q_lo)
                else:
                    hi, lo = _split3(jnp.concatenate([qs[i], xs[i]], axis=0))
                    r = dot3(hi, lo, hi[:c], lo[:c])
                    qs[i] = r[:c]
                    xs[i] = xs[i] + r[c:]
            yield
    b = base
    while b < c:
        level = ((row // (2 * b)) == (col // (2 * b))) & ((row // b) != (col // b))
        splits = [_split3(x) for x in xs]
        ts = []
        for i in n_items:
            l_hi, l_lo = _split3(jnp.where(level, ms[i], 0.0))
            ts.append(dot3(splits[i][0], splits[i][1], l_hi, l_lo))
        yield
        for i in n_items:
            t_hi, t_lo = _split3(ts[i])
            xs[i] = xs[i] - dot3(t_hi, t_lo, splits[i][0], splits[i][1])
        yield
        b *= 2
    return xs


def _gdn_kernel(x0_ref, xa_ref, xb_ref, mod_ref, nw_ref, w_ref, wsm_ref, wsmt_ref, cw_ref,
                pcol_ref, prow_ref, gnw_ref, o_ref, ext_ref, st_ref, *sets, nb, rows):
    kw = GDN_HEADS * GDN_DK
    vw = GDN_HEADS * GDN_DV
    qkv_w = 2 * kw + vw
    c = GDN_CHUNK
    ts = nb * rows
    n_chunks = ts // c
    heads = range(GDN_HEADS)
    hc = GDN_HEADS * c
    set_a, set_b = sets[:7], sets[7:]
    wd = GDN_DV + GDN_DK

    m = mod_ref[...]
    nw = nw_ref[...]
    gnw = gnw_ref[...]
    row_i = lax.broadcasted_iota(jnp.int32, (c, hc), 0)
    lane_i = lax.broadcasted_iota(jnp.int32, (c, hc), 1)
    lane_head = lane_i // c
    col_i = lane_i - lane_head * c
    incl_bd = row_i >= col_i
    strict_bd = row_i > col_i

    def prepare(x_ref, dst):
        qg_s, kn_s, rhs_s, z_s, gcol_s, m_s, qk_s = dst
        h = _mod_norm(x_ref[...], nw, m[:, 0:1, :], m[:, 1:2, :])
        h = h.reshape(ts, h.shape[-1]).astype(BF16)
        yield
        proj = jnp.dot(h, w_ref[...], preferred_element_type=F32)
        z_s[...] = _silu(proj[:, qkv_w:])
        yield
        cw = cw_ref[...]
        accs = []
        for i in range(nb):
            p_i = proj[i * rows:(i + 1) * rows, :qkv_w]
            ext_ref[i, HALO:HALO + rows, :] = p_i
            acc = cw[GDN_CONV - 1:GDN_CONV] * p_i
            for t in range(GDN_CONV - 1):
                off = HALO - (GDN_CONV - 1) + t
                acc = acc + cw[t:t + 1] * ext_ref[i, off:off + rows, :]
            ext_ref[i, 0:HALO, :] = ext_ref[i, rows:rows + HALO, :]
            accs.append(acc)
            yield
        qkv = _silu(jnp.concatenate(accs, axis=0))

        sm = jnp.dot(h, wsm_ref[...], preferred_element_type=F32)
        smt = lax.dot_general(wsmt_ref[...], h, (((1,), (1,)), ((), ())), preferred_element_type=F32)
        pcol = pcol_ref[...]
        prow = prow_ref[...]
        g_cols = -jnp.exp(pcol[0:1]) * _softplus(sm + pcol[1:2])
        beta_cols = _sigmoid(sm)
        g_rows = -jnp.exp(prow[:, 0:1]) * _softplus(smt + prow[:, 1:2])
        rr = lax.broadcasted_iota(jnp.int32, (ts, ts), 0)
        cc = lax.broadcasted_iota(jnp.int32, (ts, ts), 1)
        same = (rr // c) == (cc // c)
        gc_all = _mask_mm(jnp.where(same & (rr >= cc), 1.0, 0.0), g_cols)
        gr_all = _mm_mask(g_rows, jnp.where(same & (rr <= cc), 1.0, 0.0))
        gcol_s[...] = gc_all
        yield
        for ci in range(n_chunks):
            r0 = ci * c
            g_col_bd = jnp.zeros((c, hc), F32)
            kks, qks = [], []
            for hd in heads:
                q = qkv[r0:r0 + c, hd * GDN_DK:(hd + 1) * GDN_DK]
                k = qkv[r0:r0 + c, kw + hd * GDN_DK:kw + (hd + 1) * GDN_DK]
                v = qkv[r0:r0 + c, 2 * kw + hd * GDN_DV:2 * kw + (hd + 1) * GDN_DV]
                q = q * lax.rsqrt(jnp.sum(q * q, axis=-1, keepdims=True) + EPS) * (GDN_DK ** -0.5)
                k = k * lax.rsqrt(jnp.sum(k * k, axis=-1, keepdims=True) + EPS)
                g_col = gc_all[r0:r0 + c, hd:hd + 1]
                beta = beta_cols[r0:r0 + c, GDN_HEADS + hd:GDN_HEADS + hd + 1]
                eg = jnp.exp(g_col)
                kb = k * beta
                qg_s[r0:r0 + c, hd * GDN_DK:(hd + 1) * GDN_DK] = q * eg
                kn_s[r0:r0 + c, hd * GDN_DK:(hd + 1) * GDN_DK] = k
                rhs_s[r0:r0 + c, hd * wd:hd * wd + GDN_DV] = v * beta
                rhs_s[r0:r0 + c, hd * wd + GDN_DV:(hd + 1) * wd] = kb * eg
                kks.append(_mm_nt(kb, k))
                qks.append(_mm_nt(q, k))
                g_col_bd = jnp.where(lane_head == hd, g_col, g_col_bd)
            g_row_bd = jnp.concatenate([gr_all[hd:hd + 1, r0:r0 + c] for hd in heads], axis=1)
            decay = jnp.where(incl_bd, jnp.exp(g_col_bd - g_row_bd), 0.0)
            m_s[r0:r0 + c, :] = jnp.where(strict_bd, jnp.concatenate(kks, axis=1) * decay, 0.0)
            qk_s[r0:r0 + c, :] = jnp.concatenate(qks, axis=1) * decay
            yield

    def finish(src, out_rows):
        qg_s, kn_s, rhs_s, z_s, gcol_s, m_s, qk_s = src
        bd_mask = jnp.where(lax.broadcasted_iota(jnp.int32, (hc, hc), 0) // c
                            == lax.broadcasted_iota(jnp.int32, (hc, hc), 1) // c, 1.0, 0.0).astype(BF16)
        ms = [m_s[ci * c:(ci + 1) * c, :] for ci in range(n_chunks)]
        invs = yield from _unit_lower_inverse(ms, bd_mask, GDN_SOLVE_BASE)
        pairs = [(ci, hd) for ci in range(n_chunks) for hd in heads]
        uw = {}
        for ci, hd in pairs:
            r0 = ci * c
            rhs_hi, rhs_lo = _split3(rhs_s[r0:r0 + c, hd * wd:(hd + 1) * wd])
            inv_hi, inv_lo = _split3(invs[ci][:, hd * c:(hd + 1) * c])
            uw[ci, hd] = _dot3(inv_hi, inv_lo, rhs_hi, rhs_lo)
            if hd == GDN_HEADS - 1:
                yield
        cps = rows // c
        items = [(i, hd) for i in range(nb) for hd in heads]
        for j in range(cps):
            ci = {i: i * cps + j for i in range(nb)}
            hsl = {hd: slice(hd * GDN_DK, (hd + 1) * GDN_DK) for hd in heads}
            st = {(i, hd): st_ref[i, hd] for i, hd in items}
            ws = {(i, hd): _mm(jnp.concatenate([uw[ci[i], hd][:, GDN_DV:],
                                                qg_s[ci[i] * c:(ci[i] + 1) * c, hsl[hd]]], axis=0), st[i, hd])
                  for i, hd in items}
            yield
            v_new = {(i, hd): uw[ci[i], hd][:, :GDN_DV] - ws[i, hd][:c] for i, hd in items}
            for i, hd in items:
                r0 = ci[i] * c
                g_col = gcol_s[r0:r0 + c, hd:hd + 1]
                g_end = g_col[c - 1:c, :]
                k_end = kn_s[r0:r0 + c, hsl[hd]] * jnp.exp(g_end - g_col)
                st_ref[i, hd] = st[i, hd] * jnp.exp(g_end) + _mm_tn(k_end, v_new[i, hd])
            yield
            for i, hd in items:
                r0 = ci[i] * c
                o = ws[i, hd][c:] + _mm(qk_s[r0:r0 + c, hd * c:(hd + 1) * c], v_new[i, hd])
                o = o * lax.rsqrt(jnp.mean(o * o, axis=-1, keepdims=True) + EPS) * gnw * z_s[r0:r0 + c, hsl[hd]]
                o_ref[i, out_rows + j * c:out_rows + (j + 1) * c, hsl[hd]] = o.astype(o_ref.dtype)
                if hd == GDN_HEADS - 1:
                    yield

    @pl.when(pl.program_id(1) == 0)
    def _():
        st_ref[...] = jnp.zeros_like(st_ref)
        ext_ref[:, 0:HALO, :] = jnp.zeros((nb, HALO, qkv_w), F32)
        _alternate(prepare(x0_ref, set_a))

    _alternate(finish(set_a, 0), prepare(xa_ref, set_b))
    _alternate(finish(set_b, rows), prepare(xb_ref, set_a))


def _gdn_call(x, mod, nw, w, wsm, wsmt, cw, pcol, prow, gnw, nb, rows):
    bsz, s, d = x.shape
    kw = GDN_HEADS * GDN_DK
    vw = GDN_HEADS * GDN_DV
    qkv_w = 2 * kw + vw
    ts = nb * rows
    n_tiles = s // rows
    assert s % (2 * rows) == 0 and bsz % nb == 0, "the GDN kernel takes two tiles per grid step"
    hc = GDN_HEADS * GDN_CHUNK
    one_set = [pltpu.VMEM((ts, kw), F32), pltpu.VMEM((ts, kw), F32),
               pltpu.VMEM((ts, vw + kw), F32), pltpu.VMEM((ts, vw), F32),
               pltpu.VMEM((ts, 128), F32), pltpu.VMEM((ts, hc), F32), pltpu.VMEM((ts, hc), F32)]
    tile = lambda f: pl.BlockSpec((nb, rows, d), f)
    return pl.pallas_call(
        functools.partial(_gdn_kernel, nb=nb, rows=rows),
        grid=(bsz // nb, n_tiles // 2),
        in_specs=[
            tile(lambda b, j: (b, 0, 0)),
            tile(lambda b, j: (b, 2 * j + 1, 0)),
            tile(lambda b, j: (b, jnp.minimum(2 * j + 2, n_tiles - 1), 0)),
            pl.BlockSpec((nb, 6, d), lambda b, j: (b, 0, 0)),
            _const_spec(nw.shape), _const_spec(w.shape), _const_spec(wsm.shape),
            _const_spec(wsmt.shape), _const_spec(cw.shape), _const_spec(pcol.shape),
            _const_spec(prow.shape), _const_spec(gnw.shape),
        ],
        out_specs=pl.BlockSpec((nb, 2 * rows, vw), lambda b, j: (b, j, 0)),
        out_shape=jax.ShapeDtypeStruct((bsz, s, vw), BF16),
        scratch_shapes=[pltpu.VMEM((nb, rows + HALO, qkv_w), F32),
                        pltpu.VMEM((nb, GDN_HEADS, GDN_DK, GDN_DV), F32)] + one_set + one_set,
        compiler_params=_cparams(("parallel", "arbitrary")),
    )(x, x, x, mod, nw, w, wsm, wsmt, cw, pcol, prow, gnw)


def _group_cumsum_rows(x, group):
    rows = lax.broadcasted_iota(jnp.int32, x.shape, 0) % group
    d = 1
    while d < group:
        x = x + jnp.where(rows >= d, pltpu.roll(x, d, axis=0), 0.0)
        d *= 2
    return x


def _hgrn_kernel(x_ref, xn_ref, mod_ref, nw_ref, w_ref, lbp_ref, gnw_ref, o_ref, *scratch,
                 ts, layer, unit, group):
    kw = HGRN_HEADS * HGRN_DK
    c = HGRN_CHUNK
    n_sub = unit // c
    n_units = ts // unit
    sets = (scratch[0:7], scratch[7:14])
    st_ref = scratch[14]

    @pl.when(pl.program_id(1) == 0)
    def _():
        st_ref[...] = jnp.zeros_like(st_ref)

    m = mod_ref[0]
    nw = nw_ref[...]
    gnw = gnw_ref[...]
    incl = _tri(c)
    heads = range(HGRN_HEADS)
    sl = [slice(hd * HGRN_DK, (hd + 1) * HGRN_DK) for hd in heads]

    lbp = lbp_ref[...]
    e = jnp.exp(lbp - jnp.max(lbp, axis=0, keepdims=True))
    soft = e / jnp.sum(e, axis=0, keepdims=True)
    lb = jnp.zeros((1, kw), F32)
    for i in range(1, layer + 1):
        lb = lb + soft[i:i + 1]

    def prepare(u_idx, dst, src_ref=x_ref):
        qs_ref, ks_ref, qg_ref, ke_ref, v_ref, ge_ref, z_ref = dst
        rows = pl.ds(pl.multiple_of(u_idx * unit, unit), unit)
        h = _mod_norm(src_ref[0, rows, :], nw, m[0:1], m[1:2]).astype(BF16)
        yield
        f_raw = jnp.dot(h, w_ref[:, kw:2 * kw], preferred_element_type=F32)
        sig = _sigmoid(f_raw)
        g = _group_cumsum_rows(jnp.log(lb + (1.0 - lb) * sig), c).reshape(n_sub, c, kw)
        g_mid = g[:, c // 2 - 1:c // 2, :]
        g_last = g[:, c - 1:c, :]
        g_end = jnp.exp(g_last).reshape(n_sub, kw)
        for hd in heads:
            ge_ref[hd] = g_end[:, sl[hd]].T
        yield
        k = ((1.0 - lb) * _sigmoid(-f_raw)).reshape(n_sub, c, kw)
        ks_ref[...] = (k * jnp.exp(g_mid - g)).reshape(unit, kw).astype(BF16)
        ke_ref[...] = (k * jnp.exp(g_last - g)).reshape(unit, kw).astype(BF16)
        yield
        q = _silu(jnp.dot(h, w_ref[:, :kw], preferred_element_type=F32)).reshape(n_sub, c, kw)
        qs_ref[...] = (q * jnp.exp(g - g_mid)).reshape(unit, kw).astype(BF16)
        qg_ref[...] = (q * jnp.exp(g)).reshape(unit, kw).astype(BF16)
        yield
        v_ref[...] = jnp.dot(h, w_ref[:, 2 * kw:3 * kw], preferred_element_type=F32).astype(BF16)
        yield
        z_ref[...] = _silu(jnp.dot(h, w_ref[:, 3 * kw:], preferred_element_type=F32))

    def recur(u_idx, src):
        qs_ref, ks_ref, qg_ref, ke_ref, v_ref, ge_ref, z_ref = src
        base = pl.multiple_of(u_idx * unit, unit)
        for g0 in range(0, n_sub, group):
            subs = range(g0, g0 + group)
            rows = {u: slice(u * c, (u + 1) * c) for u in subs}
            pairs = [(u, hd) for u in subs for hd in heads]
            vv = {u: v_ref[rows[u], :] for u in subs}
            sc = {(u, hd): _mm_nt(qs_ref[rows[u], sl[hd]], ks_ref[rows[u], sl[hd]]) for u, hd in pairs}
            yield
            kv = {(u, hd): _mm_tn(ke_ref[rows[u], sl[hd]], vv[u][:, sl[hd]]) for u, hd in pairs}
            yield
            attn = {(u, hd): jnp.where(incl, sc[u, hd], 0.0).astype(BF16) for u, hd in pairs}
            yield
            st = [st_ref[hd] for hd in heads]
            oo = {}
            for u in subs:
                for hd in heads:
                    lhs = jnp.concatenate([qg_ref[rows[u], sl[hd]], attn[u, hd]], axis=1)
                    rhs = jnp.concatenate([st[hd].astype(BF16), vv[u][:, sl[hd]]], axis=0)
                    oo[u, hd] = jnp.dot(lhs, rhs, preferred_element_type=F32)
                    st[hd] = st[hd] * ge_ref[hd][:, u:u + 1] + kv[u, hd]
                if u == g0 + group // 2 - 1:
                    yield
            for hd in heads:
                st_ref[hd] = st[hd]
            yield
            for u, hd in pairs:
                o = oo[u, hd]
                o = o * lax.rsqrt(jnp.mean(o * o, axis=-1, keepdims=True) + EPS) * gnw * z_ref[rows[u], sl[hd]]
                o_ref[0, pl.ds(base + u * c, c), sl[hd]] = o.astype(o_ref.dtype)
            yield

    @pl.when(pl.program_id(1) == 0)
    def _():
        _alternate(prepare(0, sets[0]))

    def body(i, carry):
        _alternate(recur(2 * i, sets[0]), prepare(2 * i + 1, sets[1]))
        _alternate(recur(2 * i + 1, sets[1]), prepare(2 * i + 2, sets[0]))
        return carry

    lax.fori_loop(0, n_units // 2 - 1, body, 0)
    _alternate(recur(n_units - 2, sets[0]), prepare(n_units - 1, sets[1]))
    _alternate(recur(n_units - 1, sets[1]), prepare(0, sets[0], xn_ref))


def _hgrn_call(x, mod, nw, w, lbp, gnw, ts, layer):
    bsz, s, d = x.shape
    kw = HGRN_HEADS * HGRN_DK
    group = 8
    unit = min(256, ts // 2)
    assert s % ts == 0 and ts % (2 * unit) == 0 and unit % (group * HGRN_CHUNK) == 0
    one_set = ([pltpu.VMEM((unit, kw), BF16) for _ in range(5)]
               + [pltpu.VMEM((HGRN_HEADS, HGRN_DK, unit // HGRN_CHUNK), F32), pltpu.VMEM((unit, kw), F32)])
    return pl.pallas_call(
        functools.partial(_hgrn_kernel, ts=ts, layer=layer, unit=unit, group=group),
        grid=(bsz, s // ts),
        in_specs=[
            pl.BlockSpec((1, ts, d), lambda b, j: (b, j, 0)),
            pl.BlockSpec((1, unit, d), lambda b, j: (b, jnp.minimum((j + 1) * (ts // unit), s // unit - 1), 0)),
            pl.BlockSpec((1, 6, d), lambda b, j: (b, 0, 0)),
            _const_spec(nw.shape), _const_spec(w.shape), _const_spec(lbp.shape),
            _const_spec(gnw.shape),
        ],
        out_specs=pl.BlockSpec((1, ts, kw), lambda b, j: (b, j, 0)),
        out_shape=jax.ShapeDtypeStruct((bsz, s, kw), BF16),
        scratch_shapes=one_set + one_set + [pltpu.VMEM((HGRN_HEADS, HGRN_DK, HGRN_DK), F32)],
        compiler_params=_cparams(("parallel", "arbitrary")),
    )(x, x, mod, nw, w, lbp, gnw)


def _ssd_kernel(x0_ref, xa_ref, xb_ref, mod_ref, nw_ref, w_ref, wdt_ref, wdtt_ref, cw_ref, cb_ref,
                pcol_ref, prow_ref, dsk_ref, gnw_ref, o_ref, ext_ref, st_ref, *sets, ts):
    inner = SSD_HEADS * SSD_HEAD_DIM
    bc_w = 2 * SSD_GROUPS * SSD_STATE
    xbc_w = inner + bc_w
    hpg = SSD_HEADS // SSD_GROUPS
    p = SSD_HEAD_DIM
    set_a, set_b = sets[:7], sets[7:]

    m = mod_ref[0]
    nw = nw_ref[...]
    incl = _tri(ts)

    def prepare(x_ref, dst):
        xs_s, bc_s, zs_s, dtc_s, acc_s, acr_s, cb_s = dst
        h = _mod_norm(x_ref[0], nw, m[0:1], m[1:2]).astype(BF16)
        yield
        proj = jnp.dot(h, w_ref[...], preferred_element_type=F32)
        zs_s[...] = _silu(proj[:, :inner])
        yield
        ext_ref[HALO:HALO + ts, :] = proj[:, inner:]
        cw = cw_ref[...]
        acc = cw[SSD_CONV - 1:SSD_CONV] * proj[:, inner:] + cb_ref[...]
        for i in range(SSD_CONV - 1):
            off = HALO - (SSD_CONV - 1) + i
            acc = acc + cw[i:i + 1] * ext_ref[off:off + ts, :]
        ext_ref[0:HALO, :] = ext_ref[ts:ts + HALO, :]
        xbc = _silu(acc)
        xs_s[...] = xbc[:, :inner]
        bc_s[...] = xbc[:, inner:]
        yield
        pcol = pcol_ref[...]
        prow = prow_ref[...]
        dt_cols = _softplus(jnp.dot(h, wdt_ref[...], preferred_element_type=F32) + pcol[1:2])
        dt_rows = _softplus(lax.dot_general(wdtt_ref[...], h, (((1,), (1,)), ((), ())),
                                            preferred_element_type=F32) + prow[:, 1:2])
        low = jnp.where(incl, 1.0, 0.0).astype(F32)
        upp = jnp.where(lax.broadcasted_iota(jnp.int32, (ts, ts), 0)
                        <= lax.broadcasted_iota(jnp.int32, (ts, ts), 1), 1.0, 0.0).astype(F32)
        dtc_s[...] = dt_cols
        acc_s[...] = _mask_mm(low, dt_cols * (-jnp.exp(pcol[0:1])))
        acr_s[...] = _mm_mask(dt_rows * (-jnp.exp(prow[:, 0:1])), upp)
        yield
        for g in range(SSD_GROUPS):
            bm = xbc[:, inner + g * SSD_STATE:inner + (g + 1) * SSD_STATE]
            cm = xbc[:, inner + (SSD_GROUPS + g) * SSD_STATE:inner + (SSD_GROUPS + g + 1) * SSD_STATE]
            cb_s[g] = _mm_nt(cm, bm)
            yield

    def finish(src, out_rows):
        xs_s, bc_s, zs_s, dtc_s, acc_s, acr_s, cb_s = src
        first = lax.broadcasted_iota(jnp.int32, (ts, 2 * p), 1) < p
        first_rows = lax.broadcasted_iota(jnp.int32, (2 * p, SSD_STATE), 0) < p
        ys = []
        for g in range(SSD_GROUPS):
            bm = bc_s[:, g * SSD_STATE:(g + 1) * SSD_STATE]
            cm = bc_s[:, (SSD_GROUPS + g) * SSD_STATE:(SSD_GROUPS + g + 1) * SSD_STATE]
            cb = cb_s[g]
            for pr in range(hpg // 2):
                h0 = g * hpg + 2 * pr
                h1 = h0 + 1
                a0, a1 = acc_s[:, h0:h0 + 1], acc_s[:, h1:h1 + 1]
                seg0 = jnp.where(incl, jnp.exp(a0 - acr_s[h0:h0 + 1, :]), 0.0)
                seg1 = jnp.where(incl, jnp.exp(a1 - acr_s[h1:h1 + 1, :]), 0.0)
                xdt = xs_s[:, h0 * p:(h1 + 1) * p] * jnp.where(first, dtc_s[:, h0:h0 + 1], dtc_s[:, h1:h1 + 1])
                st = st_ref[g * (hpg // 2) + pr]
                y = (_mm(cb * seg0, jnp.where(first, xdt, 0.0)) + _mm(cb * seg1, jnp.where(first, 0.0, xdt))
                     + _mm_nt(cm, st) * jnp.where(first, jnp.exp(a0), jnp.exp(a1)))
                yield
                last0, last1 = a0[ts - 1:ts, :], a1[ts - 1:ts, :]
                w_in = jnp.where(first, jnp.exp(last0 - a0), jnp.exp(last1 - a1))
                st_ref[g * (hpg // 2) + pr] = (st * jnp.where(first_rows, jnp.exp(last0), jnp.exp(last1))
                                               + _mm_tn(xdt * w_in, bm))
                ys.append(y)
                yield
        gnw = gnw_ref[...]
        gw = hpg * p
        for g in range(SSD_GROUPS):
            cols = slice(g * gw, (g + 1) * gw)
            yg = jnp.concatenate(ys[g * (hpg // 2):(g + 1) * (hpg // 2)], axis=1)
            yg = (yg + dsk_ref[:, cols] * xs_s[:, cols]) * zs_s[:, cols]
            yg = yg * lax.rsqrt(jnp.mean(yg * yg, axis=-1, keepdims=True) + EPS) * gnw[:, cols]
            o_ref[0, out_rows:out_rows + ts, cols] = yg.astype(o_ref.dtype)
            yield

    @pl.when(pl.program_id(1) == 0)
    def _():
        st_ref[...] = jnp.zeros_like(st_ref)
        ext_ref[0:HALO, :] = jnp.zeros((HALO, xbc_w), F32)
        _alternate(prepare(x0_ref, set_a))

    _alternate(finish(set_a, 0), prepare(xa_ref, set_b))
    _alternate(finish(set_b, ts), prepare(xb_ref, set_a))


def _ssd_call(x, mod, nw, w, wdt, wdtt, cw, cb, pcol, prow, dsk, gnw, ts):
    bsz, s, d = x.shape
    inner = SSD_HEADS * SSD_HEAD_DIM
    bc_w = 2 * SSD_GROUPS * SSD_STATE
    xbc_w = inner + bc_w
    n_tiles = s // ts
    assert s % (2 * ts) == 0, "the SSD kernel takes two tiles per grid step"
    one_set = [pltpu.VMEM((ts, inner), F32), pltpu.VMEM((ts, bc_w), F32), pltpu.VMEM((ts, inner), F32),
               pltpu.VMEM((ts, 128), F32), pltpu.VMEM((ts, 128), F32), pltpu.VMEM((8, ts), F32),
               pltpu.VMEM((SSD_GROUPS, ts, ts), F32)]
    tile = lambda f: pl.BlockSpec((1, ts, d), f)
    return pl.pallas_call(
        functools.partial(_ssd_kernel, ts=ts),
        grid=(bsz, n_tiles // 2),
        in_specs=[
            tile(lambda b, j: (b, 0, 0)),
            tile(lambda b, j: (b, 2 * j + 1, 0)),
            tile(lambda b, j: (b, jnp.minimum(2 * j + 2, n_tiles - 1), 0)),
            pl.BlockSpec((1, 6, d), lambda b, j: (b, 0, 0)),
            _const_spec(nw.shape), _const_spec(w.shape), _const_spec(wdt.shape),
            _const_spec(wdtt.shape), _const_spec(cw.shape), _const_spec(cb.shape),
            _const_spec(pcol.shape), _const_spec(prow.shape), _const_spec(dsk.shape),
            _const_spec(gnw.shape),
        ],
        out_specs=pl.BlockSpec((1, 2 * ts, inner), lambda b, j: (b, j, 0)),
        out_shape=jax.ShapeDtypeStruct((bsz, s, inner), BF16),
        scratch_shapes=[pltpu.VMEM((ts + HALO, xbc_w), F32),
                        pltpu.VMEM((SSD_HEADS // 2, 2 * SSD_HEAD_DIM, SSD_STATE), F32)] + one_set + one_set,
        compiler_params=_cparams(("parallel", "arbitrary")),
    )(x, x, x, mod, nw, w, wdt, wdtt, cw, cb, pcol, prow, dsk, gnw)


def _merge_kernel(x_ref, mod_ref, nw_ref, oa_ref, ob_ref, oc_ref, wg_ref, wa_ref, wb_ref, wc_ref,
                  wo_ref, o_ref):
    d = x_ref.shape[-1]
    x = x_ref[0]
    m = mod_ref[0]
    h = _mod_norm(x, nw_ref[...], m[0:1], m[1:2]).astype(BF16)
    gates = _sigmoid(jnp.dot(h, wg_ref[...], preferred_element_type=F32))
    merged = (gates[:, :d] * jnp.dot(oa_ref[0], wa_ref[...], preferred_element_type=F32)
              + gates[:, d:2 * d] * jnp.dot(ob_ref[0], wb_ref[...], preferred_element_type=F32)
              + gates[:, 2 * d:] * jnp.dot(oc_ref[0], wc_ref[...], preferred_element_type=F32))
    mix = jnp.dot(merged.astype(BF16), wo_ref[...], preferred_element_type=F32)
    o_ref[0] = x + m[2:3] * mix


def _merge_call(x, mod, nw, oa, ob, oc, wg, wa, wb, wc, wo, tm):
    bsz, s, d = x.shape
    tok = lambda w: pl.BlockSpec((1, tm, w), lambda b, j: (b, j, 0))
    return pl.pallas_call(
        _merge_kernel,
        grid=(bsz, s // tm),
        in_specs=[
            tok(d), pl.BlockSpec((1, 6, d), lambda b, j: (b, 0, 0)), _const_spec(nw.shape),
            tok(oa.shape[-1]), tok(ob.shape[-1]), tok(oc.shape[-1]),
            _const_spec(wg.shape), _const_spec(wa.shape), _const_spec(wb.shape),
            _const_spec(wc.shape), _const_spec(wo.shape),
        ],
        out_specs=tok(d),
        out_shape=jax.ShapeDtypeStruct((bsz, s, d), F32),
        compiler_params=_cparams(("parallel", "parallel")),
    )(x, mod, nw, oa, ob, oc, wg, wa, wb, wc, wo)


def _ffn_kernel(x_ref, mod_ref, nw_ref, wup_ref, cw_ref, cb_ref, wdn_ref, fnw_ref, o_ref,
                ext_ref, tail_ref, *, tm, splits, final):
    hidden = wdn_ref.shape[0]

    @pl.when(pl.program_id(1) == 0)
    def _():
        tail_ref[...] = jnp.zeros_like(tail_ref)

    x = x_ref[0]
    m = mod_ref[0]
    h = _mod_norm(x, nw_ref[...], m[3:4], m[4:5]).astype(BF16)

    def conv_part(c0, fc):
        u = jnp.dot(h, wup_ref[:, c0:c0 + fc], preferred_element_type=F32)
        ext_ref[0:HALO, 0:fc] = tail_ref[:, c0:c0 + fc]
        ext_ref[HALO:HALO + tm, 0:fc] = u
        cw = cw_ref[:, c0:c0 + fc]
        acc = cw[FFN_CONV - 1:FFN_CONV] * u + cb_ref[:, c0:c0 + fc]
        for i in range(FFN_CONV - 1):
            off = HALO - (FFN_CONV - 1) + i
            acc = acc + cw[i:i + 1] * ext_ref[off:off + tm, 0:fc]
        tail_ref[:, c0:c0 + fc] = ext_ref[tm:tm + HALO, 0:fc]
        return acc

    y = jnp.zeros_like(x)
    for c0, fc in splits:
        gate = conv_part(c0, fc)
        val = conv_part(hidden + c0, fc)
        act = (_silu(gate) * val).astype(BF16)
        y = y + jnp.dot(act, wdn_ref[c0:c0 + fc, :], preferred_element_type=F32)
    out = x + m[5:6] * y
    if final:
        out = out * lax.rsqrt(jnp.mean(out * out, axis=-1, keepdims=True) + EPS) * fnw_ref[...]
    o_ref[0] = out


def _ffn_call(x, mod, nw, wup, cw, cb, wdn, fnw, tm, final):
    bsz, s, d = x.shape
    hidden = wdn.shape[0]
    first = -(-(hidden // MXU_TILE) // 2) * MXU_TILE
    splits = ((0, first), (first, hidden - first))
    tok = pl.BlockSpec((1, tm, d), lambda b, j: (b, j, 0))
    single = lambda a: pl.BlockSpec(a.shape, lambda *_: (0,) * a.ndim, pipeline_mode=pl.Buffered(1))
    return pl.pallas_call(
        functools.partial(_ffn_kernel, tm=tm, splits=splits, final=final),
        grid=(bsz, s // tm),
        in_specs=[
            tok, pl.BlockSpec((1, 6, d), lambda b, j: (b, 0, 0)), _const_spec(nw.shape),
            single(wup), _const_spec(cw.shape), _const_spec(cb.shape), single(wdn),
            _const_spec(fnw.shape),
        ],
        out_specs=tok,
        out_shape=jax.ShapeDtypeStruct((bsz, s, d), F32),
        scratch_shapes=[pltpu.VMEM((tm + HALO, first), F32),
                        pltpu.VMEM((HALO, 2 * hidden), F32)],
        compiler_params=_cparams(("parallel", "arbitrary")),
    )(x, mod, nw, wup, cw, cb, wdn, fnw)


def _pad_lanes(a, width=128):
    return jnp.pad(a, ((0, 0), (0, width - a.shape[-1])))


def kernel(x, c, w_ada, b_ada, norm1_w, w_in, gdn_conv_w, gdn_a_log, gdn_dt_bias, gdn_norm_w,
           hgrn_lb_param, hgrn_norm_w, ssd_conv_w, ssd_conv_b, ssd_a_log, ssd_dt_bias, ssd_d,
           ssd_norm_w, w_br_a, w_br_b, w_br_c, w_out, norm2_w, ffn_w_up, ffn_conv_w, ffn_conv_b,
           ffn_w_down, final_norm_w):
    bsz, s, d = x.shape
    depth = w_in.shape[0]
    gk = GDN_HEADS * GDN_DK
    gv = GDN_HEADS * GDN_DV
    hk = HGRN_HEADS * HGRN_DK
    inner = SSD_HEADS * SSD_HEAD_DIM
    xbc_w = inner + 2 * SSD_GROUPS * SSD_STATE

    sizes = (2 * gk + gv, GDN_HEADS, GDN_HEADS, gv, hk, hk, hk, hk, inner, xbc_w, SSD_HEADS, 3 * d)
    offs = [0]
    for sz in sizes:
        offs.append(offs[-1] + sz)
    (o_qkv, o_a, o_b, o_z, o_hq, _, _, _, o_sz, o_xbc, o_dt, o_gate, o_end) = offs

    nb_gdn = 4 if bsz % 4 == 0 else (2 if bsz % 2 == 0 else 1)
    rows_gdn = min(max(GDN_CHUNK, 256 // nb_gdn), s)
    ts_hgrn = min(2048, s)
    ts_ssd = min(256, s)
    tm_merge = min(512, s)
    tm_ffn = min(512, s)

    mod = _ada_call(c, w_ada, b_ada).reshape(depth, bsz, 6, d)
    fnw = final_norm_w.reshape(1, d)

    for l in range(depth):
        wl = w_in[l]
        mod_l = mod[l]
        nw1 = norm1_w[l].reshape(1, d)

        w_gdn = jnp.concatenate([wl[:, o_qkv:o_a], wl[:, o_z:o_hq]], axis=1).astype(BF16)
        w_ab = wl[:, o_a:o_z]
        o_ga = _gdn_call(
            x, mod_l, nw1, w_gdn, _pad_lanes(w_ab).astype(BF16), w_ab.T.astype(BF16),
            gdn_conv_w[l],
            _pad_lanes(jnp.stack([gdn_a_log[l], gdn_dt_bias[l]])),
            jnp.pad(jnp.stack([gdn_a_log[l], gdn_dt_bias[l]], axis=1), ((0, GDN_HEADS), (0, 0))),
            gdn_norm_w[l].reshape(1, GDN_DV), nb_gdn, rows_gdn)

        o_hg = _hgrn_call(x, mod_l, nw1, wl[:, o_hq:o_sz].astype(BF16), hgrn_lb_param,
                          hgrn_norm_w[l].reshape(1, HGRN_DK), ts_hgrn, l)

        w_dt = wl[:, o_dt:o_gate]
        o_sd = _ssd_call(
            x, mod_l, nw1, wl[:, o_sz:o_dt].astype(BF16), _pad_lanes(w_dt).astype(BF16),
            w_dt.T.astype(BF16), ssd_conv_w[l], ssd_conv_b[l].reshape(1, xbc_w),
            _pad_lanes(jnp.stack([ssd_a_log[l], ssd_dt_bias[l]])),
            jnp.stack([ssd_a_log[l], ssd_dt_bias[l]], axis=1),
            jnp.repeat(ssd_d[l], SSD_HEAD_DIM).reshape(1, inner),
            ssd_norm_w[l].reshape(1, inner), ts_ssd)

        x = _merge_call(x, mod_l, nw1, o_ga, o_hg, o_sd, wl[:, o_gate:o_end].astype(BF16),
                        w_br_a[l].astype(BF16), w_br_b[l].astype(BF16), w_br_c[l].astype(BF16),
                        w_out[l].astype(BF16), tm_merge)

        x = _ffn_call(x, mod_l, norm2_w[l].reshape(1, d), ffn_w_up[l].astype(BF16), ffn_conv_w[l],
                      ffn_conv_b[l].reshape(1, -1), ffn_w_down[l].astype(BF16), fnw, tm_ffn,
                      final=(l == depth - 1))
    return x
```

```python
import functools

import jax
import jax.numpy as jnp
from jax import lax
from jax.experimental import pallas as pl
from jax.experimental.pallas import tpu as pltpu

F32 = jnp.float32
BF16 = jnp.bfloat16
EPS = 1e-6

GDN_HEADS = 4
GDN_DK = 128
GDN_DV = 128
GDN_CONV = 4
GDN_CHUNK = 64
GDN_SOLVE_BASE = 8
HGRN_HEADS = 4
HGRN_DK = 128
HGRN_CHUNK = 16
SSD_HEADS = 8
SSD_HEAD_DIM = 64
SSD_GROUPS = 2
SSD_STATE = 128
SSD_CONV = 4
FFN_CONV = 3

HALO = 8
MXU_TILE = 256
VMEM_LIMIT = 56 * 1024 * 1024

_HI = lax.Precision.HIGHEST


def _mm(a, b):
    return jnp.dot(a.astype(BF16), b.astype(BF16), preferred_element_type=F32)


def _mm_nt(a, b):
    return lax.dot_general(a.astype(BF16), b.astype(BF16), (((1,), (1,)), ((), ())),
                           preferred_element_type=F32)


def _mm_tn(a, b):
    return lax.dot_general(a.astype(BF16), b.astype(BF16), (((0,), (0,)), ((), ())),
                           preferred_element_type=F32)


def _mm_hi(a, b):
    return jnp.dot(a, b, precision=_HI, preferred_element_type=F32)


def _split_exact(a):
    p1 = a.astype(BF16)
    r1 = a - p1.astype(F32)
    p2 = r1.astype(BF16)
    p3 = (r1 - p2.astype(F32)).astype(BF16)
    return p1, p2, p3


def _mask_mm(mask, a):
    mb = mask.astype(BF16)
    return sum(jnp.dot(mb, p, preferred_element_type=F32) for p in _split_exact(a))


def _mm_mask(a, mask):
    mb = mask.astype(BF16)
    return sum(jnp.dot(p, mb, preferred_element_type=F32) for p in _split_exact(a))


def _sigmoid(x):
    return jax.nn.sigmoid(x)


def _silu(x):
    return x * jax.nn.sigmoid(x)


def _softplus(x):
    return jnp.maximum(x, 0.0) + jnp.log(1.0 + jnp.exp(-jnp.abs(x)))


def _mod_norm(x, nw, shift, scale):
    y = x * lax.rsqrt(jnp.mean(x * x, axis=-1, keepdims=True) + EPS) * nw
    return y * (1.0 + scale) + shift


def _tri(n, strict=False):
    r = lax.broadcasted_iota(jnp.int32, (n, n), 0)
    c = lax.broadcasted_iota(jnp.int32, (n, n), 1)
    return (r > c) if strict else (r >= c)


def _cparams(sem):
    return pltpu.CompilerParams(dimension_semantics=sem, vmem_limit_bytes=VMEM_LIMIT)


def _const_spec(shape):
    nd = len(shape)
    return pl.BlockSpec(shape, lambda *_: (0,) * nd)


def _alternate(*gens):
    live = list(gens)
    while live:
        for gen in list(live):
            if next(gen, live) is live:
                live.remove(gen)


def _ada_kernel(c_ref, w_ref, b_ref, o_ref):
    c = c_ref[...]
    o_ref[0] = _mm_hi(_silu(c), w_ref[0]) + b_ref[0]


def _ada_call(c, w_ada, b_ada):
    depth, d, n = w_ada.shape
    bsz = c.shape[0]
    tn = 1536
    return pl.pallas_call(
        _ada_kernel,
        grid=(depth, n // tn),
        in_specs=[
            pl.BlockSpec((bsz, d), lambda l, j: (0, 0)),
            pl.BlockSpec((1, d, tn), lambda l, j: (l, 0, j)),
            pl.BlockSpec((1, 1, tn), lambda l, j: (l, 0, j)),
        ],
        out_specs=pl.BlockSpec((1, bsz, tn), lambda l, j: (l, 0, j)),
        out_shape=jax.ShapeDtypeStruct((depth, bsz, n), F32),
        compiler_params=_cparams(("arbitrary", "arbitrary")),
    )(c, w_ada, b_ada.reshape(depth, 1, n))


def _split3(a):
    hi = a.astype(BF16)
    lo = (a - hi.astype(F32)).astype(BF16)
    return hi, lo


def _dot3(a_hi, a_lo, b_hi, b_lo):
    f = lambda p, q: jnp.dot(p, q, preferred_element_type=F32)
    return f(a_hi, b_hi) + (f(a_lo, b_hi) + f(a_hi, b_lo))


def _unit_lower_inverse(ms, bd_mask, base):
    c = ms[0].shape[0]
    nb = ms[0].shape[1] // c
    f = lambda p, q: jnp.dot(p, q, preferred_element_type=F32)
    row = lax.broadcasted_iota(jnp.int32, (c, nb * c), 0)
    col = lax.broadcasted_iota(jnp.int32, (c, nb * c), 1) % c
    eye = jnp.where(row == col, 1.0, 0.0).astype(F32)

    def blockdiag(x):
        return jnp.concatenate([x] * nb, axis=0) * bd_mask

    def dot3(a_hi, a_lo, b_hi, b_lo):
        r = a_hi.shape[0]
        t = f(jnp.concatenate([a_hi, a_lo], axis=0), blockdiag(b_hi))
        return t[:r] + (t[r:] + f(a_hi, blockdiag(b_lo)))

    n_items = range(len(ms))
    n_steps = base.bit_length() - 1
    in_base = (row // base) == (col // base)
    ds = [jnp.where(in_base, m, 0.0) for m in ms]
    xs = [eye - d for d in ds]
    if n_steps > 1:
        qs = []
        for d in ds:
            hi, lo = _split3(d)
            qs.append(dot3(hi, lo, hi, lo))
        for step in range(1, n_steps):
            for i in n_items:
                if step == n_steps - 1:
                    p_hi, p_lo = _split3(xs[i])
                    q_hi, q_lo = _split3(qs[i])
                    xs[i] = xs[i] + dot3(p_hi, p_lo, q_hi, q_lo)
                else:
                    hi, lo = _split3(jnp.concatenate([qs[i], xs[i]], axis=0))
                    r = dot3(hi, lo, hi[:c], lo[:c])
                    qs[i] = r[:c]
                    xs[i] = xs[i] + r[c:]
            yield
    b = base
    while b < c:
        level = ((row // (2 * b)) == (col // (2 * b))) & ((row // b) != (col // b))
        splits = [_split3(x) for x in xs]
        ts = []
        for i in n_items:
            l_hi, l_lo = _split3(jnp.where(level, ms[i], 0.0))
            ts.append(dot3(splits[i][0], splits[i][1], l_hi, l_lo))
        yield
        for i in n_items:
            t_hi, t_lo = _split3(ts[i])
            xs[i] = xs[i] - dot3(t_hi, t_lo, splits[i][0], splits[i][1])
        yield
        b *= 2
    return xs


def _gdn_kernel(x0_ref, xa_ref, xb_ref, mod_ref, nw_ref, w_ref, wsm_ref, wsmt_ref, cw_ref,
                pcol_ref, prow_ref, gnw_ref, o_ref, ext_ref, st_ref, *sets, nb, rows):
    kw = GDN_HEADS * GDN_DK
    vw = GDN_HEADS * GDN_DV
    qkv_w = 2 * kw + vw
    c = GDN_CHUNK
    ts = nb * rows
    n_chunks = ts // c
    heads = range(GDN_HEADS)
    hc = GDN_HEADS * c
    set_a, set_b = sets[:7], sets[7:]
    wd = GDN_DV + GDN_DK

    m = mod_ref[...]
    nw = nw_ref[...]
    gnw = gnw_ref[...]
    row_i = lax.broadcasted_iota(jnp.int32, (c, hc), 0)
    lane_i = lax.broadcasted_iota(jnp.int32, (c, hc), 1)
    lane_head = lane_i // c
    col_i = lane_i - lane_head * c
    incl_bd = row_i >= col_i
    strict_bd = row_i > col_i

    def prepare(x_ref, dst):
        qg_s, kn_s, rhs_s, z_s, gcol_s, m_s, qk_s = dst
        h = _mod_norm(x_ref[...], nw, m[:, 0:1, :], m[:, 1:2, :])
        h = h.reshape(ts, h.shape[-1]).astype(BF16)
        yield
        proj = jnp.dot(h, w_ref[...], preferred_element_type=F32)
        z_s[...] = _silu(proj[:, qkv_w:])
        yield
        cw = cw_ref[...]
        accs = []
        for i in range(nb):
            p_i = proj[i * rows:(i + 1) * rows, :qkv_w]
            ext_ref[i, HALO:HALO + rows, :] = p_i
            acc = cw[GDN_CONV - 1:GDN_CONV] * p_i
            for t in range(GDN_CONV - 1):
                off = HALO - (GDN_CONV - 1) + t
                acc = acc + cw[t:t + 1] * ext_ref[i, off:off + rows, :]
            ext_ref[i, 0:HALO, :] = ext_ref[i, rows:rows + HALO, :]
            accs.append(acc)
            yield
        qkv = _silu(jnp.concatenate(accs, axis=0))

        sm = jnp.dot(h, wsm_ref[...], preferred_element_type=F32)
        smt = lax.dot_general(wsmt_ref[...], h, (((1,), (1,)), ((), ())), preferred_element_type=F32)
        pcol = pcol_ref[...]
        prow = prow_ref[...]
        g_cols = -jnp.exp(pcol[0:1]) * _softplus(sm + pcol[1:2])
        beta_cols = _sigmoid(sm)
        g_rows = -jnp.exp(prow[:, 0:1]) * _softplus(smt + prow[:, 1:2])
        rr = lax.broadcasted_iota(jnp.int32, (ts, ts), 0)
        cc = lax.broadcasted_iota(jnp.int32, (ts, ts), 1)
        same = (rr // c) == (cc // c)
        gc_all = _mask_mm(jnp.where(same & (rr >= cc), 1.0, 0.0), g_cols)
        gr_all = _mm_mask(g_rows, jnp.where(same & (rr <= cc), 1.0, 0.0))
        gcol_s[...] = gc_all
        yield
        for ci in range(n_chunks):
            r0 = ci * c
            g_col_bd = jnp.zeros((c, hc), F32)
            kks, qks = [], []
            for hd in heads:
                q = qkv[r0:r0 + c, hd * GDN_DK:(hd + 1) * GDN_DK]
                k = qkv[r0:r0 + c, kw + hd * GDN_DK:kw + (hd + 1) * GDN_DK]
                v = qkv[r0:r0 + c, 2 * kw + hd * GDN_DV:2 * kw + (hd + 1) * GDN_DV]
                q = q * lax.rsqrt(jnp.sum(q * q, axis=-1, keepdims=True) + EPS) * (GDN_DK ** -0.5)
                k = k * lax.rsqrt(jnp.sum(k * k, axis=-1, keepdims=True) + EPS)
                g_col = gc_all[r0:r0 + c, hd:hd + 1]
                beta = beta_cols[r0:r0 + c, GDN_HEADS + hd:GDN_HEADS + hd + 1]
                eg = jnp.exp(g_col)
                kb = k * beta
                qg_s[r0:r0 + c, hd * GDN_DK:(hd + 1) * GDN_DK] = q * eg
                kn_s[r0:r0 + c, hd * GDN_DK:(hd + 1) * GDN_DK] = k
                rhs_s[r0:r0 + c, hd * wd:hd * wd + GDN_DV] = v * beta
                rhs_s[r0:r0 + c, hd * wd + GDN_DV:(hd + 1) * wd] = kb * eg
                kks.append(_mm_nt(kb, k))
                qks.append(_mm_nt(q, k))
                g_col_bd = jnp.where(lane_head == hd, g_col, g_col_bd)
            g_row_bd = jnp.concatenate([gr_all[hd:hd + 1, r0:r0 + c] for hd in heads], axis=1)
            decay = jnp.where(incl_bd, jnp.exp(g_col_bd - g_row_bd), 0.0)
            m_s[r0:r0 + c, :] = jnp.where(strict_bd, jnp.concatenate(kks, axis=1) * decay, 0.0)
            qk_s[r0:r0 + c, :] = jnp.concatenate(qks, axis=1) * decay
            yield

    def finish(src, out_rows):
        qg_s, kn_s, rhs_s, z_s, gcol_s, m_s, qk_s = src
        bd_mask = jnp.where(lax.broadcasted_iota(jnp.int32, (hc, hc), 0) // c
                            == lax.broadcasted_iota(jnp.int32, (hc, hc), 1) // c, 1.0, 0.0).astype(BF16)
        ms = [m_s[ci * c:(ci + 1) * c, :] for ci in range(n_chunks)]
        invs = yield from _unit_lower_inverse(ms, bd_mask, GDN_SOLVE_BASE)
        pairs = [(ci, hd) for ci in range(n_chunks) for hd in heads]
        uw = {}
        for ci, hd in pairs:
            r0 = ci * c
            rhs_hi, rhs_lo = _split3(rhs_s[r0:r0 + c, hd * wd:(hd + 1) * wd])
            inv_hi, inv_lo = _split3(invs[ci][:, hd * c:(hd + 1) * c])
            uw[ci, hd] = _dot3(inv_hi, inv_lo, rhs_hi, rhs_lo)
            if hd == GDN_HEADS - 1:
                yield
        cps = rows // c
        items = [(i, hd) for i in range(nb) for hd in heads]
        for j in range(cps):
            ci = {i: i * cps + j for i in range(nb)}
            hsl = {hd: slice(hd * GDN_DK, (hd + 1) * GDN_DK) for hd in heads}
            st = {(i, hd): st_ref[i, hd] for i, hd in items}
            ws = {(i, hd): _mm(jnp.concatenate([uw[ci[i], hd][:, GDN_DV:],
                                                qg_s[ci[i] * c:(ci[i] + 1) * c, hsl[hd]]], axis=0), st[i, hd])
                  for i, hd in items}
            yield
            v_new = {(i, hd): uw[ci[i], hd][:, :GDN_DV] - ws[i, hd][:c] for i, hd in items}
            for i, hd in items:
                r0 = ci[i] * c
                g_col = gcol_s[r0:r0 + c, hd:hd + 1]
                g_end = g_col[c - 1:c, :]
                k_end = kn_s[r0:r0 + c, hsl[hd]] * jnp.exp(g_end - g_col)
                st_ref[i, hd] = st[i, hd] * jnp.exp(g_end) + _mm_tn(k_end, v_new[i, hd])
            yield
            for i, hd in items:
                r0 = ci[i] * c
                o = ws[i, hd][c:] + _mm(qk_s[r0:r0 + c, hd * c:(hd + 1) * c], v_new[i, hd])
                o = o * lax.rsqrt(jnp.mean(o * o, axis=-1, keepdims=True) + EPS) * gnw * z_s[r0:r0 + c, hsl[hd]]
                o_ref[i, out_rows + j * c:out_rows + (j + 1) * c, hsl[hd]] = o.astype(o_ref.dtype)
                if hd == GDN_HEADS - 1:
                    yield

    @pl.when(pl.program_id(1) == 0)
    def _():
        st_ref[...] = jnp.zeros_like(st_ref)
        ext_ref[:, 0:HALO, :] = jnp.zeros((nb, HALO, qkv_w), F32)
        _alternate(prepare(x0_ref, set_a))

    _alternate(finish(set_a, 0), prepare(xa_ref, set_b))
    _alternate(finish(set_b, rows), prepare(xb_ref, set_a))


def _gdn_call(x, mod, nw, w, wsm, wsmt, cw, pcol, prow, gnw, nb, rows):
    bsz, s, d = x.shape
    kw = GDN_HEADS * GDN_DK
    vw = GDN_HEADS * GDN_DV
    qkv_w = 2 * kw + vw
    ts = nb * rows
    n_tiles = s // rows
    assert s % (2 * rows) == 0 and bsz % nb == 0, "the GDN kernel takes two tiles per grid step"
    hc = GDN_HEADS * GDN_CHUNK
    one_set = [pltpu.VMEM((ts, kw), F32), pltpu.VMEM((ts, kw), F32),
               pltpu.VMEM((ts, vw + kw), F32), pltpu.VMEM((ts, vw), F32),
               pltpu.VMEM((ts, 128), F32), pltpu.VMEM((ts, hc), F32), pltpu.VMEM((ts, hc), F32)]
    tile = lambda f: pl.BlockSpec((nb, rows, d), f)
    return pl.pallas_call(
        functools.partial(_gdn_kernel, nb=nb, rows=rows),
        grid=(bsz // nb, n_tiles // 2),
        in_specs=[
            tile(lambda b, j: (b, 0, 0)),
            tile(lambda b, j: (b, 2 * j + 1, 0)),
            tile(lambda b, j: (b, jnp.minimum(2 * j + 2, n_tiles - 1), 0)),
            pl.BlockSpec((nb, 6, d), lambda b, j: (b, 0, 0)),
            _const_spec(nw.shape), _const_spec(w.shape), _const_spec(wsm.shape),
            _const_spec(wsmt.shape), _const_spec(cw.shape), _const_spec(pcol.shape),
            _const_spec(prow.shape), _const_spec(gnw.shape),
        ],
        out_specs=pl.BlockSpec((nb, 2 * rows, vw), lambda b, j: (b, j, 0)),
        out_shape=jax.ShapeDtypeStruct((bsz, s, vw), BF16),
        scratch_shapes=[pltpu.VMEM((nb, rows + HALO, qkv_w), F32),
                        pltpu.VMEM((nb, GDN_HEADS, GDN_DK, GDN_DV), F32)] + one_set + one_set,
        compiler_params=_cparams(("parallel", "arbitrary")),
    )(x, x, x, mod, nw, w, wsm, wsmt, cw, pcol, prow, gnw)


def _group_cumsum_rows(x, group):
    rows = lax.broadcasted_iota(jnp.int32, x.shape, 0) % group
    d = 1
    while d < group:
        x = x + jnp.where(rows >= d, pltpu.roll(x, d, axis=0), 0.0)
        d *= 2
    return x


def _hgrn_kernel(x_ref, xn_ref, mod_ref, nw_ref, w_ref, lbp_ref, gnw_ref, o_ref, *scratch,
                 ts, layer, unit, group):
    kw = HGRN_HEADS * HGRN_DK
    c = HGRN_CHUNK
    n_sub = unit // c
    n_units = ts // unit
    sets = (scratch[0:7], scratch[7:14])
    st_ref = scratch[14]

    @pl.when(pl.program_id(1) == 0)
    def _():
        st_ref[...] = jnp.zeros_like(st_ref)

    m = mod_ref[0]
    nw = nw_ref[...]
    gnw = gnw_ref[...]
    incl = _tri(c)
    heads = range(HGRN_HEADS)
    sl = [slice(hd * HGRN_DK, (hd + 1) * HGRN_DK) for hd in heads]

    lbp = lbp_ref[...]
    e = jnp.exp(lbp - jnp.max(lbp, axis=0, keepdims=True))
    soft = e / jnp.sum(e, axis=0, keepdims=True)
    lb = jnp.zeros((1, kw), F32)
    for i in range(1, layer + 1):
        lb = lb + soft[i:i + 1]

    def prepare(u_idx, dst, src_ref=x_ref):
        qs_ref, ks_ref, qg_ref, ke_ref, v_ref, ge_ref, z_ref = dst
        rows = pl.ds(pl.multiple_of(u_idx * unit, unit), unit)
        h = _mod_norm(src_ref[0, rows, :], nw, m[0:1], m[1:2]).astype(BF16)
        yield
        f_raw = jnp.dot(h, w_ref[:, kw:2 * kw], preferred_element_type=F32)
        sig = _sigmoid(f_raw)
        g = _group_cumsum_rows(jnp.log(lb + (1.0 - lb) * sig), c).reshape(n_sub, c, kw)
        g_mid = g[:, c // 2 - 1:c // 2, :]
        g_last = g[:, c - 1:c, :]
        g_end = jnp.exp(g_last).reshape(n_sub, kw)
        for hd in heads:
            ge_ref[hd] = g_end[:, sl[hd]].T
        yield
        k = ((1.0 - lb) * _sigmoid(-f_raw)).reshape(n_sub, c, kw)
        ks_ref[...] = (k * jnp.exp(g_mid - g)).reshape(unit, kw).astype(BF16)
        ke_ref[...] = (k * jnp.exp(g_last - g)).reshape(unit, kw).astype(BF16)
        yield
        q = _silu(jnp.dot(h, w_ref[:, :kw], preferred_element_type=F32)).reshape(n_sub, c, kw)
        qs_ref[...] = (q * jnp.exp(g - g_mid)).reshape(unit, kw).astype(BF16)
        qg_ref[...] = (q * jnp.exp(g)).reshape(unit, kw).astype(BF16)
        yield
        v_ref[...] = jnp.dot(h, w_ref[:, 2 * kw:3 * kw], preferred_element_type=F32).astype(BF16)
        yield
        z_ref[...] = _silu(jnp.dot(h, w_ref[:, 3 * kw:], preferred_element_type=F32))

    def recur(u_idx, src):
        qs_ref, ks_ref, qg_ref, ke_ref, v_ref, ge_ref, z_ref = src
        base = pl.multiple_of(u_idx * unit, unit)
        for g0 in range(0, n_sub, group):
            subs = range(g0, g0 + group)
            rows = {u: slice(u * c, (u + 1) * c) for u in subs}
            pairs = [(u, hd) for u in subs for hd in heads]
            vv = {u: v_ref[rows[u], :] for u in subs}
            sc = {(u, hd): _mm_nt(qs_ref[rows[u], sl[hd]], ks_ref[rows[u], sl[hd]]) for u, hd in pairs}
            yield
            kv = {(u, hd): _mm_tn(ke_ref[rows[u], sl[hd]], vv[u][:, sl[hd]]) for u, hd in pairs}
            yield
            attn = {(u, hd): jnp.where(incl, sc[u, hd], 0.0).astype(BF16) for u, hd in pairs}
            yield
            st = [st_ref[hd] for hd in heads]
            oo = {}
            for u in subs:
                for hd in heads:
                    lhs = jnp.concatenate([qg_ref[rows[u], sl[hd]], attn[u, hd]], axis=1)
                    rhs = jnp.concatenate([st[hd].astype(BF16), vv[u][:, sl[hd]]], axis=0)
                    oo[u, hd] = jnp.dot(lhs, rhs, preferred_element_type=F32)
                    st[hd] = st[hd] * ge_ref[hd][:, u:u + 1] + kv[u, hd]
                if u == g0 + group // 2 - 1:
                    yield
            for hd in heads:
                st_ref[hd] = st[hd]
            yield
            for u, hd in pairs:
                o = oo[u, hd]
                o = o * lax.rsqrt(jnp.mean(o * o, axis=-1, keepdims=True) + EPS) * gnw * z_ref[rows[u], sl[hd]]
                o_ref[0, pl.ds(base + u * c, c), sl[hd]] = o.astype(o_ref.dtype)
            yield

    @pl.when(pl.program_id(1) == 0)
    def _():
        _alternate(prepare(0, sets[0]))

    def body(i, carry):
        _alternate(recur(2 * i, sets[0]), prepare(2 * i + 1, sets[1]))
        _alternate(recur(2 * i + 1, sets[1]), prepare(2 * i + 2, sets[0]))
        return carry

    lax.fori_loop(0, n_units // 2 - 1, body, 0)
    _alternate(recur(n_units - 2, sets[0]), prepare(n_units - 1, sets[1]))
    _alternate(recur(n_units - 1, sets[1]), prepare(0, sets[0], xn_ref))


def _hgrn_call(x, mod, nw, w, lbp, gnw, ts, layer):
    bsz, s, d = x.shape
    kw = HGRN_HEADS * HGRN_DK
    group = 8
    unit = min(256, ts // 2)
    assert s % ts == 0 and ts % (2 * unit) == 0 and unit % (group * HGRN_CHUNK) == 0
    one_set = ([pltpu.VMEM((unit, kw), BF16) for _ in range(5)]
               + [pltpu.VMEM((HGRN_HEADS, HGRN_DK, unit // HGRN_CHUNK), F32), pltpu.VMEM((unit, kw), F32)])
    return pl.pallas_call(
        functools.partial(_hgrn_kernel, ts=ts, layer=layer, unit=unit, group=group),
        grid=(bsz, s // ts),
        in_specs=[
            pl.BlockSpec((1, ts, d), lambda b, j: (b, j, 0)),
            pl.BlockSpec((1, unit, d), lambda b, j: (b, jnp.minimum((j + 1) * (ts // unit), s // unit - 1), 0)),
            pl.BlockSpec((1, 6, d), lambda b, j: (b, 0, 0)),
            _const_spec(nw.shape), _const_spec(w.shape), _const_spec(lbp.shape),
            _const_spec(gnw.shape),
        ],
        out_specs=pl.BlockSpec((1, ts, kw), lambda b, j: (b, j, 0)),
        out_shape=jax.ShapeDtypeStruct((bsz, s, kw), BF16),
        scratch_shapes=one_set + one_set + [pltpu.VMEM((HGRN_HEADS, HGRN_DK, HGRN_DK), F32)],
        compiler_params=_cparams(("parallel", "arbitrary")),
    )(x, x, mod, nw, w, lbp, gnw)


def _ssd_kernel(x0_ref, xa_ref, xb_ref, mod_ref, modn_ref, nw_ref, w_ref, wdt_ref, wdtt_ref, cw_ref,
                cb_ref, pcol_ref, prow_ref, dsk_ref, gnw_ref, o_ref, ext_ref, st_ref, *sets, ts, n_tiles):
    inner = SSD_HEADS * SSD_HEAD_DIM
    bc_w = 2 * SSD_GROUPS * SSD_STATE
    xbc_w = inner + bc_w
    hpg = SSD_HEADS // SSD_GROUPS
    p = SSD_HEAD_DIM
    set_a, set_b = sets[:7], sets[7:]

    m = mod_ref[0]
    nw = nw_ref[...]
    incl = _tri(ts)

    def prepare(x_ref, dst, m=m, history=None):
        xs_s, bc_s, zs_s, dtc_s, acc_s, acr_s, cb_s = dst
        h = _mod_norm(x_ref[0], nw, m[0:1], m[1:2]).astype(BF16)
        yield
        proj = jnp.dot(h, w_ref[...], preferred_element_type=F32)
        zs_s[...] = _silu(proj[:, :inner])
        yield
        if history is not None:
            ext_ref[0:HALO, :] = ext_ref[0:HALO, :] * history
        ext_ref[HALO:HALO + ts, :] = proj[:, inner:]
        cw = cw_ref[...]
        acc = cw[SSD_CONV - 1:SSD_CONV] * proj[:, inner:] + cb_ref[...]
        for i in range(SSD_CONV - 1):
            off = HALO - (SSD_CONV - 1) + i
            acc = acc + cw[i:i + 1] * ext_ref[off:off + ts, :]
        ext_ref[0:HALO, :] = ext_ref[ts:ts + HALO, :]
        xbc = _silu(acc)
        xs_s[...] = xbc[:, :inner]
        bc_s[...] = xbc[:, inner:]
        yield
        pcol = pcol_ref[...]
        prow = prow_ref[...]
        dt_cols = _softplus(jnp.dot(h, wdt_ref[...], preferred_element_type=F32) + pcol[1:2])
        dt_rows = _softplus(lax.dot_general(wdtt_ref[...], h, (((1,), (1,)), ((), ())),
                                            preferred_element_type=F32) + prow[:, 1:2])
        low = jnp.where(incl, 1.0, 0.0).astype(F32)
        upp = jnp.where(lax.broadcasted_iota(jnp.int32, (ts, ts), 0)
                        <= lax.broadcasted_iota(jnp.int32, (ts, ts), 1), 1.0, 0.0).astype(F32)
        dtc_s[...] = dt_cols
        acc_s[...] = _mask_mm(low, dt_cols * (-jnp.exp(pcol[0:1])))
        acr_s[...] = _mm_mask(dt_rows * (-jnp.exp(prow[:, 0:1])), upp)
        yield
        for g in range(SSD_GROUPS):
            bm = xbc[:, inner + g * SSD_STATE:inner + (g + 1) * SSD_STATE]
            cm = xbc[:, inner + (SSD_GROUPS + g) * SSD_STATE:inner + (SSD_GROUPS + g + 1) * SSD_STATE]
            cb_s[g] = _mm_nt(cm, bm)
            yield

    def finish(src, out_rows):
        xs_s, bc_s, zs_s, dtc_s, acc_s, acr_s, cb_s = src
        first = lax.broadcasted_iota(jnp.int32, (ts, 2 * p), 1) < p
        first_rows = lax.broadcasted_iota(jnp.int32, (2 * p, SSD_STATE), 0) < p
        ys = []
        for g in range(SSD_GROUPS):
            bm = bc_s[:, g * SSD_STATE:(g + 1) * SSD_STATE]
            cm = bc_s[:, (SSD_GROUPS + g) * SSD_STATE:(SSD_GROUPS + g + 1) * SSD_STATE]
            cb = cb_s[g]
            for pr in range(hpg // 2):
                h0 = g * hpg + 2 * pr
                h1 = h0 + 1
                a0, a1 = acc_s[:, h0:h0 + 1], acc_s[:, h1:h1 + 1]
                seg0 = jnp.where(incl, jnp.exp(a0 - acr_s[h0:h0 + 1, :]), 0.0)
                seg1 = jnp.where(incl, jnp.exp(a1 - acr_s[h1:h1 + 1, :]), 0.0)
                xdt = xs_s[:, h0 * p:(h1 + 1) * p] * jnp.where(first, dtc_s[:, h0:h0 + 1], dtc_s[:, h1:h1 + 1])
                st = st_ref[g * (hpg // 2) + pr]
                y = (_mm(cb * seg0, jnp.where(first, xdt, 0.0)) + _mm(cb * seg1, jnp.where(first, 0.0, xdt))
                     + _mm_nt(cm, st) * jnp.where(first, jnp.exp(a0), jnp.exp(a1)))
                yield
                last0, last1 = a0[ts - 1:ts, :], a1[ts - 1:ts, :]
                w_in = jnp.where(first, jnp.exp(last0 - a0), jnp.exp(last1 - a1))
                st_ref[g * (hpg // 2) + pr] = (st * jnp.where(first_rows, jnp.exp(last0), jnp.exp(last1))
                                               + _mm_tn(xdt * w_in, bm))
                ys.append(y)
                yield
        gnw = gnw_ref[...]
        gw = hpg * p
        for g in range(SSD_GROUPS):
            cols = slice(g * gw, (g + 1) * gw)
            yg = jnp.concatenate(ys[g * (hpg // 2):(g + 1) * (hpg // 2)], axis=1)
            yg = (yg + dsk_ref[:, cols] * xs_s[:, cols]) * zs_s[:, cols]
            yg = yg * lax.rsqrt(jnp.mean(yg * yg, axis=-1, keepdims=True) + EPS) * gnw[:, cols]
            o_ref[0, out_rows:out_rows + ts, cols] = yg.astype(o_ref.dtype)
            yield

    j = pl.program_id(1)

    @pl.when(j == 0)
    def _():
        st_ref[...] = jnp.zeros_like(st_ref)

    @pl.when((pl.program_id(0) == 0) & (j == 0))
    def _():
        ext_ref[0:HALO, :] = jnp.zeros((HALO, xbc_w), F32)
        _alternate(prepare(x0_ref, set_a))

    _alternate(finish(set_a, 0), prepare(xa_ref, set_b))
    new_seq = 2 * j + 2 >= n_tiles
    mn = modn_ref[0]
    m_next = jnp.where(new_seq, mn, m)
    _alternate(finish(set_b, ts),
               prepare(xb_ref, set_a, m=m_next, history=jnp.where(new_seq, 0.0, 1.0).astype(F32)))


def _ssd_call(x, mod, nw, w, wdt, wdtt, cw, cb, pcol, prow, dsk, gnw, ts):
    bsz, s, d = x.shape
    inner = SSD_HEADS * SSD_HEAD_DIM
    bc_w = 2 * SSD_GROUPS * SSD_STATE
    xbc_w = inner + bc_w
    n_tiles = s // ts
    assert s % (2 * ts) == 0, "the SSD kernel takes two tiles per grid step"
    one_set = [pltpu.VMEM((ts, inner), F32), pltpu.VMEM((ts, bc_w), F32), pltpu.VMEM((ts, inner), F32),
               pltpu.VMEM((ts, 128), F32), pltpu.VMEM((ts, 128), F32), pltpu.VMEM((8, ts), F32),
               pltpu.VMEM((SSD_GROUPS, ts, ts), F32)]
    tile = lambda f: pl.BlockSpec((1, ts, d), f)
    next_b = lambda b, j: jnp.minimum(b + (2 * j + 2) // n_tiles, bsz - 1)
    return pl.pallas_call(
        functools.partial(_ssd_kernel, ts=ts, n_tiles=n_tiles),
        grid=(bsz, n_tiles // 2),
        in_specs=[
            tile(lambda b, j: (b, 0, 0)),
            tile(lambda b, j: (b, 2 * j + 1, 0)),
            tile(lambda b, j: (next_b(b, j), (2 * j + 2) % n_tiles, 0)),
            pl.BlockSpec((1, 6, d), lambda b, j: (b, 0, 0)),
            pl.BlockSpec((1, 6, d), lambda b, j: (next_b(b, j), 0, 0)),
            _const_spec(nw.shape), _const_spec(w.shape), _const_spec(wdt.shape),
            _const_spec(wdtt.shape), _const_spec(cw.shape), _const_spec(cb.shape),
            _const_spec(pcol.shape), _const_spec(prow.shape), _const_spec(dsk.shape),
            _const_spec(gnw.shape),
        ],
        out_specs=pl.BlockSpec((1, 2 * ts, inner), lambda b, j: (b, j, 0)),
        out_shape=jax.ShapeDtypeStruct((bsz, s, inner), BF16),
        scratch_shapes=[pltpu.VMEM((ts + HALO, xbc_w), F32),
                        pltpu.VMEM((SSD_HEADS // 2, 2 * SSD_HEAD_DIM, SSD_STATE), F32)] + one_set + one_set,
        compiler_params=_cparams(("arbitrary", "arbitrary")),
    )(x, x, x, mod, mod, nw, w, wdt, wdtt, cw, cb, pcol, prow, dsk, gnw)


def _merge_kernel(x_ref, mod_ref, nw_ref, oa_ref, ob_ref, oc_ref, wg_ref, wa_ref, wb_ref, wc_ref,
                  wo_ref, o_ref):
    d = x_ref.shape[-1]
    x = x_ref[0]
    m = mod_ref[0]
    h = _mod_norm(x, nw_ref[...], m[0:1], m[1:2]).astype(BF16)
    gates = _sigmoid(jnp.dot(h, wg_ref[...], preferred_element_type=F32))
    merged = (gates[:, :d] * jnp.dot(oa_ref[0], wa_ref[...], preferred_element_type=F32)
              + gates[:, d:2 * d] * jnp.dot(ob_ref[0], wb_ref[...], preferred_element_type=F32)
              + gates[:, 2 * d:] * jnp.dot(oc_ref[0], wc_ref[...], preferred_element_type=F32))
    mix = jnp.dot(merged.astype(BF16), wo_ref[...], preferred_element_type=F32)
    o_ref[0] = x + m[2:3] * mix


def _merge_call(x, mod, nw, oa, ob, oc, wg, wa, wb, wc, wo, tm):
    bsz, s, d = x.shape
    tok = lambda w: pl.BlockSpec((1, tm, w), lambda b, j: (b, j, 0))
    return pl.pallas_call(
        _merge_kernel,
        grid=(bsz, s // tm),
        in_specs=[
            tok(d), pl.BlockSpec((1, 6, d), lambda b, j: (b, 0, 0)), _const_spec(nw.shape),
            tok(oa.shape[-1]), tok(ob.shape[-1]), tok(oc.shape[-1]),
            _const_spec(wg.shape), _const_spec(wa.shape), _const_spec(wb.shape),
            _const_spec(wc.shape), _const_spec(wo.shape),
        ],
        out_specs=tok(d),
        out_shape=jax.ShapeDtypeStruct((bsz, s, d), F32),
        compiler_params=_cparams(("parallel", "parallel")),
    )(x, mod, nw, oa, ob, oc, wg, wa, wb, wc, wo)


def _ffn_kernel(x_ref, mod_ref, nw_ref, wup_ref, cw_ref, cb_ref, wdn_ref, fnw_ref, o_ref,
                ext_ref, tail_ref, *, tm, splits, final):
    hidden = wdn_ref.shape[0]

    @pl.when(pl.program_id(1) == 0)
    def _():
        tail_ref[...] = jnp.zeros_like(tail_ref)

    x = x_ref[0]
    m = mod_ref[0]
    h = _mod_norm(x, nw_ref[...], m[3:4], m[4:5]).astype(BF16)

    def conv_part(c0, fc):
        u = jnp.dot(h, wup_ref[:, c0:c0 + fc], preferred_element_type=F32)
        ext_ref[0:HALO, 0:fc] = tail_ref[:, c0:c0 + fc]
        ext_ref[HALO:HALO + tm, 0:fc] = u
        cw = cw_ref[:, c0:c0 + fc]
        acc = cw[FFN_CONV - 1:FFN_CONV] * u + cb_ref[:, c0:c0 + fc]
        for i in range(FFN_CONV - 1):
            off = HALO - (FFN_CONV - 1) + i
            acc = acc + cw[i:i + 1] * ext_ref[off:off + tm, 0:fc]
        tail_ref[:, c0:c0 + fc] = ext_ref[tm:tm + HALO, 0:fc]
        return acc

    y = jnp.zeros_like(x)
    for c0, fc in splits:
        gate = conv_part(c0, fc)
        val = conv_part(hidden + c0, fc)
        act = (_silu(gate) * val).astype(BF16)
        y = y + jnp.dot(act, wdn_ref[c0:c0 + fc, :], preferred_element_type=F32)
    out = x + m[5:6] * y
    if final:
        out = out * lax.rsqrt(jnp.mean(out * out, axis=-1, keepdims=True) + EPS) * fnw_ref[...]
    o_ref[0] = out


def _ffn_call(x, mod, nw, wup, cw, cb, wdn, fnw, tm, final):
    bsz, s, d = x.shape
    hidden = wdn.shape[0]
    first = -(-(hidden // MXU_TILE) // 2) * MXU_TILE
    splits = ((0, first), (first, hidden - first))
    tok = pl.BlockSpec((1, tm, d), lambda b, j: (b, j, 0))
    single = lambda a: pl.BlockSpec(a.shape, lambda *_: (0,) * a.ndim, pipeline_mode=pl.Buffered(1))
    return pl.pallas_call(
        functools.partial(_ffn_kernel, tm=tm, splits=splits, final=final),
        grid=(bsz, s // tm),
        in_specs=[
            tok, pl.BlockSpec((1, 6, d), lambda b, j: (b, 0, 0)), _const_spec(nw.shape),
            single(wup), _const_spec(cw.shape), _const_spec(cb.shape), single(wdn),
            _const_spec(fnw.shape),
        ],
        out_specs=tok,
        out_shape=jax.ShapeDtypeStruct((bsz, s, d), F32),
        scratch_shapes=[pltpu.VMEM((tm + HALO, first), F32),
                        pltpu.VMEM((HALO, 2 * hidden), F32)],
        compiler_params=_cparams(("parallel", "arbitrary")),
    )(x, mod, nw, wup, cw, cb, wdn, fnw)


def _pad_lanes(a, width=128):
    return jnp.pad(a, ((0, 0), (0, width - a.shape[-1])))


def kernel(x, c, w_ada, b_ada, norm1_w, w_in, gdn_conv_w, gdn_a_log, gdn_dt_bias, gdn_norm_w,
           hgrn_lb_param, hgrn_norm_w, ssd_conv_w, ssd_conv_b, ssd_a_log, ssd_dt_bias, ssd_d,
           ssd_norm_w, w_br_a, w_br_b, w_br_c, w_out, norm2_w, ffn_w_up, ffn_conv_w, ffn_conv_b,
           ffn_w_down, final_norm_w):
    bsz, s, d = x.shape
    depth = w_in.shape[0]
    gk = GDN_HEADS * GDN_DK
    gv = GDN_HEADS * GDN_DV
    hk = HGRN_HEADS * HGRN_DK
    inner = SSD_HEADS * SSD_HEAD_DIM
    xbc_w = inner + 2 * SSD_GROUPS * SSD_STATE

    sizes = (2 * gk + gv, GDN_HEADS, GDN_HEADS, gv, hk, hk, hk, hk, inner, xbc_w, SSD_HEADS, 3 * d)
    offs = [0]
    for sz in sizes:
        offs.append(offs[-1] + sz)
    (o_qkv, o_a, o_b, o_z, o_hq, _, _, _, o_sz, o_xbc, o_dt, o_gate, o_end) = offs

    nb_gdn = 4 if bsz % 4 == 0 else (2 if bsz % 2 == 0 else 1)
    rows_gdn = min(max(GDN_CHUNK, 256 // nb_gdn), s)
    ts_hgrn = min(2048, s)
    ts_ssd = min(256, s)
    tm_merge = min(512, s)
    tm_ffn = min(512, s)

    mod = _ada_call(c, w_ada, b_ada).reshape(depth, bsz, 6, d)
    fnw = final_norm_w.reshape(1, d)

    for l in range(depth):
        wl = w_in[l]
        mod_l = mod[l]
        nw1 = norm1_w[l].reshape(1, d)

        w_gdn = jnp.concatenate([wl[:, o_qkv:o_a], wl[:, o_z:o_hq]], axis=1).astype(BF16)
        w_ab = wl[:, o_a:o_z]
        o_ga = _gdn_call(
            x, mod_l, nw1, w_gdn, _pad_lanes(w_ab).astype(BF16), w_ab.T.astype(BF16),
            gdn_conv_w[l],
            _pad_lanes(jnp.stack([gdn_a_log[l], gdn_dt_bias[l]])),
            jnp.pad(jnp.stack([gdn_a_log[l], gdn_dt_bias[l]], axis=1), ((0, GDN_HEADS), (0, 0))),
            gdn_norm_w[l].reshape(1, GDN_DV), nb_gdn, rows_gdn)

        o_hg = _hgrn_call(x, mod_l, nw1, wl[:, o_hq:o_sz].astype(BF16), hgrn_lb_param,
                          hgrn_norm_w[l].reshape(1, HGRN_DK), ts_hgrn, l)

        w_dt = wl[:, o_dt:o_gate]
        o_sd = _ssd_call(
            x, mod_l, nw1, wl[:, o_sz:o_dt].astype(BF16), _pad_lanes(w_dt).astype(BF16),
            w_dt.T.astype(BF16), ssd_conv_w[l], ssd_conv_b[l].reshape(1, xbc_w),
            _pad_lanes(jnp.stack([ssd_a_log[l], ssd_dt_bias[l]])),
            jnp.stack([ssd_a_log[l], ssd_dt_bias[l]], axis=1),
            jnp.repeat(ssd_d[l], SSD_HEAD_DIM).reshape(1, inner),
            ssd_norm_w[l].reshape(1, inner), ts_ssd)

        x = _merge_call(x, mod_l, nw1, o_ga, o_hg, o_sd, wl[:, o_gate:o_end].astype(BF16),
                        w_br_a[l].astype(BF16), w_br_b[l].astype(BF16), w_br_c[l].astype(BF16),
                        w_out[l].astype(BF16), tm_merge)

        x = _ffn_call(x, mod_l, norm2_w[l].reshape(1, d), ffn_w_up[l].astype(BF16), ffn_conv_w[l],
                      ffn_conv_b[l].reshape(1, -1), ffn_w_down[l].astype(BF16), fnw, tm_ffn,
                      final=(l == depth - 1))
    return x
```

```python
import functools

import jax
import jax.numpy as jnp
from jax import lax
from jax.experimental import pallas as pl
from jax.experimental.pallas import tpu as pltpu

F32 = jnp.float32
BF16 = jnp.bfloat16
EPS = 1e-6

GDN_HEADS = 4
GDN_DK = 128
GDN_DV = 128
GDN_CONV = 4
GDN_CHUNK = 64
GDN_SOLVE_BASE = 8
HGRN_HEADS = 4
HGRN_DK = 128
HGRN_CHUNK = 16
SSD_HEADS = 8
SSD_HEAD_DIM = 64
SSD_GROUPS = 2
SSD_STATE = 128
SSD_CONV = 4
FFN_CONV = 3

HALO = 8
MXU_TILE = 256
VMEM_LIMIT = 56 * 1024 * 1024

_HI = lax.Precision.HIGHEST


def _mm(a, b):
    return jnp.dot(a.astype(BF16), b.astype(BF16), preferred_element_type=F32)


def _mm_nt(a, b):
    return lax.dot_general(a.astype(BF16), b.astype(BF16), (((1,), (1,)), ((), ())),
                           preferred_element_type=F32)


def _mm_tn(a, b):
    return lax.dot_general(a.astype(BF16), b.astype(BF16), (((0,), (0,)), ((), ())),
                           preferred_element_type=F32)


def _mm_hi(a, b):
    return jnp.dot(a, b, precision=_HI, preferred_element_type=F32)


def _split_exact(a):
    p1 = a.astype(BF16)
    r1 = a - p1.astype(F32)
    p2 = r1.astype(BF16)
    p3 = (r1 - p2.astype(F32)).astype(BF16)
    return p1, p2, p3


def _mask_mm(mask, a):
    mb = mask.astype(BF16)
    return sum(jnp.dot(mb, p, preferred_element_type=F32) for p in _split_exact(a))


def _mm_mask(a, mask):
    mb = mask.astype(BF16)
    return sum(jnp.dot(p, mb, preferred_element_type=F32) for p in _split_exact(a))


def _sigmoid(x):
    return jax.nn.sigmoid(x)


def _silu(x):
    return x * jax.nn.sigmoid(x)


def _softplus(x):
    return jnp.maximum(x, 0.0) + jnp.log(1.0 + jnp.exp(-jnp.abs(x)))


def _mod_norm(x, nw, shift, scale):
    y = x * lax.rsqrt(jnp.mean(x * x, axis=-1, keepdims=True) + EPS) * nw
    return y * (1.0 + scale) + shift


def _tri(n, strict=False):
    r = lax.broadcasted_iota(jnp.int32, (n, n), 0)
    c = lax.broadcasted_iota(jnp.int32, (n, n), 1)
    return (r > c) if strict else (r >= c)


def _cparams(sem):
    return pltpu.CompilerParams(dimension_semantics=sem, vmem_limit_bytes=VMEM_LIMIT)


def _const_spec(shape):
    nd = len(shape)
    return pl.BlockSpec(shape, lambda *_: (0,) * nd)


def _alternate(*gens):
    live = list(gens)
    while live:
        for gen in list(live):
            if next(gen, live) is live:
                live.remove(gen)


def _ada_kernel(c_ref, w_ref, b_ref, o_ref):
    c = c_ref[...]
    o_ref[0] = _mm_hi(_silu(c), w_ref[0]) + b_ref[0]


def _ada_call(c, w_ada, b_ada):
    depth, d, n = w_ada.shape
    bsz = c.shape[0]
    tn = 1536
    return pl.pallas_call(
        _ada_kernel,
        grid=(depth, n // tn),
        in_specs=[
            pl.BlockSpec((bsz, d), lambda l, j: (0, 0)),
            pl.BlockSpec((1, d, tn), lambda l, j: (l, 0, j)),
            pl.BlockSpec((1, 1, tn), lambda l, j: (l, 0, j)),
        ],
        out_specs=pl.BlockSpec((1, bsz, tn), lambda l, j: (l, 0, j)),
        out_shape=jax.ShapeDtypeStruct((depth, bsz, n), F32),
        compiler_params=_cparams(("arbitrary", "arbitrary")),
    )(c, w_ada, b_ada.reshape(depth, 1, n))


def _split3(a):
    hi = a.astype(BF16)
    lo = (a - hi.astype(F32)).astype(BF16)
    return hi, lo


def _dot3(a_hi, a_lo, b_hi, b_lo):
    f = lambda p, q: jnp.dot(p, q, preferred_element_type=F32)
    return f(a_hi, b_hi) + (f(a_lo, b_hi) + f(a_hi, b_lo))


def _unit_lower_inverse(ms, bd_mask, base):
    c = ms[0].shape[0]
    nb = ms[0].shape[1] // c
    f = lambda p, q: jnp.dot(p, q, preferred_element_type=F32)
    row = lax.broadcasted_iota(jnp.int32, (c, nb * c), 0)
    col = lax.broadcasted_iota(jnp.int32, (c, nb * c), 1) % c
    eye = jnp.where(row == col, 1.0, 0.0).astype(F32)

    def blockdiag(x):
        return jnp.concatenate([x] * nb, axis=0) * bd_mask

    def dot3(a_hi, a_lo, b_hi, b_lo):
        r = a_hi.shape[0]
        t = f(jnp.concatenate([a_hi, a_lo], axis=0), blockdiag(b_hi))
        return t[:r] + (t[r:] + f(a_hi, blockdiag(b_lo)))

    n_items = range(len(ms))
    n_steps = base.bit_length() - 1
    in_base = (row // base) == (col // base)
    ds = [jnp.where(in_base, m, 0.0) for m in ms]
    xs = [eye - d for d in ds]
    if n_steps > 1:
        qs = []
        for d in ds:
            hi, lo = _split3(d)
            qs.append(dot3(hi, lo, hi, lo))
        for step in range(1, n_steps):
            for i in n_items:
                if step == n_steps - 1:
                    p_hi, p_lo = _split3(xs[i])
                    q_hi, q_lo = _split3(qs[i])
                    xs[i] = xs[i] + dot3(p_hi, p_lo, q_hi, q_lo)
                else:
                    hi, lo = _split3(jnp.concatenate([qs[i], xs[i]], axis=0))
                    r = dot3(hi, lo, hi[:c], lo[:c])
                    qs[i] = r[:c]
                    xs[i] = xs[i] + r[c:]
            yield
    b = base
    while b < c:
        level = ((row // (2 * b)) == (col // (2 * b))) & ((row // b) != (col // b))
        splits = [_split3(x) for x in xs]
        ts = []
        for i in n_items:
            l_hi, l_lo = _split3(jnp.where(level, ms[i], 0.0))
            ts.append(dot3(splits[i][0], splits[i][1], l_hi, l_lo))
        yield
        for i in n_items:
            t_hi, t_lo = _split3(ts[i])
            xs[i] = xs[i] - dot3(t_hi, t_lo, splits[i][0], splits[i][1])
        yield
        b *= 2
    return xs


def _gdn_kernel(x0_ref, xa_ref, xb_ref, mod_ref, modn_ref, nw_ref, w_ref, wsm_ref, wsmt_ref, cw_ref,
                pcol_ref, prow_ref, gnw_ref, o_ref, ext_ref, st_ref, *sets, nb, rows, n_tiles):
    kw = GDN_HEADS * GDN_DK
    vw = GDN_HEADS * GDN_DV
    qkv_w = 2 * kw + vw
    c = GDN_CHUNK
    ts = nb * rows
    n_chunks = ts // c
    heads = range(GDN_HEADS)
    hc = GDN_HEADS * c
    set_a, set_b = sets[:7], sets[7:]
    wd = GDN_DV + GDN_DK

    m = mod_ref[...]
    nw = nw_ref[...]
    gnw = gnw_ref[...]
    row_i = lax.broadcasted_iota(jnp.int32, (c, hc), 0)
    lane_i = lax.broadcasted_iota(jnp.int32, (c, hc), 1)
    lane_head = lane_i // c
    col_i = lane_i - lane_head * c
    incl_bd = row_i >= col_i
    strict_bd = row_i > col_i

    def prepare(x_ref, dst, m=m, history=None):
        qg_s, kn_s, rhs_s, z_s, gcol_s, m_s, qk_s = dst
        h = _mod_norm(x_ref[...], nw, m[:, 0:1, :], m[:, 1:2, :])
        h = h.reshape(ts, h.shape[-1]).astype(BF16)
        yield
        proj = jnp.dot(h, w_ref[...], preferred_element_type=F32)
        z_s[...] = _silu(proj[:, qkv_w:])
        yield
        cw = cw_ref[...]
        accs = []
        for i in range(nb):
            p_i = proj[i * rows:(i + 1) * rows, :qkv_w]
            if history is not None:
                ext_ref[i, 0:HALO, :] = ext_ref[i, 0:HALO, :] * history
            ext_ref[i, HALO:HALO + rows, :] = p_i
            acc = cw[GDN_CONV - 1:GDN_CONV] * p_i
            for t in range(GDN_CONV - 1):
                off = HALO - (GDN_CONV - 1) + t
                acc = acc + cw[t:t + 1] * ext_ref[i, off:off + rows, :]
            ext_ref[i, 0:HALO, :] = ext_ref[i, rows:rows + HALO, :]
            accs.append(acc)
            yield
        qkv = _silu(jnp.concatenate(accs, axis=0))

        sm = jnp.dot(h, wsm_ref[...], preferred_element_type=F32)
        smt = lax.dot_general(wsmt_ref[...], h, (((1,), (1,)), ((), ())), preferred_element_type=F32)
        pcol = pcol_ref[...]
        prow = prow_ref[...]
        g_cols = -jnp.exp(pcol[0:1]) * _softplus(sm + pcol[1:2])
        beta_cols = _sigmoid(sm)
        g_rows = -jnp.exp(prow[:, 0:1]) * _softplus(smt + prow[:, 1:2])
        rr = lax.broadcasted_iota(jnp.int32, (ts, ts), 0)
        cc = lax.broadcasted_iota(jnp.int32, (ts, ts), 1)
        same = (rr // c) == (cc // c)
        gc_all = _mask_mm(jnp.where(same & (rr >= cc), 1.0, 0.0), g_cols)
        gr_all = _mm_mask(g_rows, jnp.where(same & (rr <= cc), 1.0, 0.0))
        gcol_s[...] = gc_all
        yield
        for ci in range(n_chunks):
            r0 = ci * c
            g_col_bd = jnp.zeros((c, hc), F32)
            kks, qks = [], []
            for hd in heads:
                q = qkv[r0:r0 + c, hd * GDN_DK:(hd + 1) * GDN_DK]
                k = qkv[r0:r0 + c, kw + hd * GDN_DK:kw + (hd + 1) * GDN_DK]
                v = qkv[r0:r0 + c, 2 * kw + hd * GDN_DV:2 * kw + (hd + 1) * GDN_DV]
                q = q * lax.rsqrt(jnp.sum(q * q, axis=-1, keepdims=True) + EPS) * (GDN_DK ** -0.5)
                k = k * lax.rsqrt(jnp.sum(k * k, axis=-1, keepdims=True) + EPS)
                g_col = gc_all[r0:r0 + c, hd:hd + 1]
                beta = beta_cols[r0:r0 + c, GDN_HEADS + hd:GDN_HEADS + hd + 1]
                eg = jnp.exp(g_col)
                kb = k * beta
                qg_s[r0:r0 + c, hd * GDN_DK:(hd + 1) * GDN_DK] = q * eg
                kn_s[r0:r0 + c, hd * GDN_DK:(hd + 1) * GDN_DK] = k
                rhs_s[r0:r0 + c, hd * wd:hd * wd + GDN_DV] = v * beta
                rhs_s[r0:r0 + c, hd * wd + GDN_DV:(hd + 1) * wd] = kb * eg
                kks.append(_mm_nt(kb, k))
                qks.append(_mm_nt(q, k))
                g_col_bd = jnp.where(lane_head == hd, g_col, g_col_bd)
            g_row_bd = jnp.concatenate([gr_all[hd:hd + 1, r0:r0 + c] for hd in heads], axis=1)
            decay = jnp.where(incl_bd, jnp.exp(g_col_bd - g_row_bd), 0.0)
            m_s[r0:r0 + c, :] = jnp.where(strict_bd, jnp.concatenate(kks, axis=1) * decay, 0.0)
            qk_s[r0:r0 + c, :] = jnp.concatenate(qks, axis=1) * decay
            yield

    def finish(src, out_rows):
        qg_s, kn_s, rhs_s, z_s, gcol_s, m_s, qk_s = src
        bd_mask = jnp.where(lax.broadcasted_iota(jnp.int32, (hc, hc), 0) // c
                            == lax.broadcasted_iota(jnp.int32, (hc, hc), 1) // c, 1.0, 0.0).astype(BF16)
        ms = [m_s[ci * c:(ci + 1) * c, :] for ci in range(n_chunks)]
        invs = yield from _unit_lower_inverse(ms, bd_mask, GDN_SOLVE_BASE)
        pairs = [(ci, hd) for ci in range(n_chunks) for hd in heads]
        uw = {}
        for ci, hd in pairs:
            r0 = ci * c
            rhs_hi, rhs_lo = _split3(rhs_s[r0:r0 + c, hd * wd:(hd + 1) * wd])
            inv_hi, inv_lo = _split3(invs[ci][:, hd * c:(hd + 1) * c])
            uw[ci, hd] = _dot3(inv_hi, inv_lo, rhs_hi, rhs_lo)
            if hd == GDN_HEADS - 1:
                yield
        cps = rows // c
        items = [(i, hd) for i in range(nb) for hd in heads]
        for j in range(cps):
            ci = {i: i * cps + j for i in range(nb)}
            hsl = {hd: slice(hd * GDN_DK, (hd + 1) * GDN_DK) for hd in heads}
            st = {(i, hd): st_ref[i, hd] for i, hd in items}
            ws = {(i, hd): _mm(jnp.concatenate([uw[ci[i], hd][:, GDN_DV:],
                                                qg_s[ci[i] * c:(ci[i] + 1) * c, hsl[hd]]], axis=0), st[i, hd])
                  for i, hd in items}
            yield
            v_new = {(i, hd): uw[ci[i], hd][:, :GDN_DV] - ws[i, hd][:c] for i, hd in items}
            for i, hd in items:
                r0 = ci[i] * c
                g_col = gcol_s[r0:r0 + c, hd:hd + 1]
                g_end = g_col[c - 1:c, :]
                k_end = kn_s[r0:r0 + c, hsl[hd]] * jnp.exp(g_end - g_col)
                st_ref[i, hd] = st[i, hd] * jnp.exp(g_end) + _mm_tn(k_end, v_new[i, hd])
            yield
            for i, hd in items:
                r0 = ci[i] * c
                o = ws[i, hd][c:] + _mm(qk_s[r0:r0 + c, hd * c:(hd + 1) * c], v_new[i, hd])
                o = o * lax.rsqrt(jnp.mean(o * o, axis=-1, keepdims=True) + EPS) * gnw * z_s[r0:r0 + c, hsl[hd]]
                o_ref[i, out_rows + j * c:out_rows + (j + 1) * c, hsl[hd]] = o.astype(o_ref.dtype)
                if hd == GDN_HEADS - 1:
                    yield

    j = pl.program_id(1)

    @pl.when(j == 0)
    def _():
        st_ref[...] = jnp.zeros_like(st_ref)

    @pl.when((pl.program_id(0) == 0) & (j == 0))
    def _():
        ext_ref[:, 0:HALO, :] = jnp.zeros((nb, HALO, qkv_w), F32)
        _alternate(prepare(x0_ref, set_a))

    _alternate(finish(set_a, 0), prepare(xa_ref, set_b))
    new_seq = 2 * j + 2 >= n_tiles
    m_next = jnp.where(new_seq, modn_ref[...], m)
    _alternate(finish(set_b, rows),
               prepare(xb_ref, set_a, m=m_next, history=jnp.where(new_seq, 0.0, 1.0).astype(F32)))


def _gdn_call(x, mod, nw, w, wsm, wsmt, cw, pcol, prow, gnw, nb, rows):
    bsz, s, d = x.shape
    kw = GDN_HEADS * GDN_DK
    vw = GDN_HEADS * GDN_DV
    qkv_w = 2 * kw + vw
    ts = nb * rows
    n_tiles = s // rows
    assert s % (2 * rows) == 0 and bsz % nb == 0, "the GDN kernel takes two tiles per grid step"
    hc = GDN_HEADS * GDN_CHUNK
    one_set = [pltpu.VMEM((ts, kw), F32), pltpu.VMEM((ts, kw), F32),
               pltpu.VMEM((ts, vw + kw), F32), pltpu.VMEM((ts, vw), F32),
               pltpu.VMEM((ts, 128), F32), pltpu.VMEM((ts, hc), F32), pltpu.VMEM((ts, hc), F32)]
    tile = lambda f: pl.BlockSpec((nb, rows, d), f)
    return pl.pallas_call(
        functools.partial(_gdn_kernel, nb=nb, rows=rows, n_tiles=n_tiles),
        grid=(bsz // nb, n_tiles // 2),
        in_specs=[
            tile(lambda b, j: (b, 0, 0)),
            tile(lambda b, j: (b, 2 * j + 1, 0)),
            tile(lambda b, j: (jnp.minimum(b + (2 * j + 2) // n_tiles, bsz // nb - 1),
                               (2 * j + 2) % n_tiles, 0)),
            pl.BlockSpec((nb, 6, d), lambda b, j: (b, 0, 0)),
            pl.BlockSpec((nb, 6, d), lambda b, j: (jnp.minimum(b + (2 * j + 2) // n_tiles, bsz // nb - 1),
                                                   0, 0)),
            _const_spec(nw.shape), _const_spec(w.shape), _const_spec(wsm.shape),
            _const_spec(wsmt.shape), _const_spec(cw.shape), _const_spec(pcol.shape),
            _const_spec(prow.shape), _const_spec(gnw.shape),
        ],
        out_specs=pl.BlockSpec((nb, 2 * rows, vw), lambda b, j: (b, j, 0)),
        out_shape=jax.ShapeDtypeStruct((bsz, s, vw), BF16),
        scratch_shapes=[pltpu.VMEM((nb, rows + HALO, qkv_w), F32),
                        pltpu.VMEM((nb, GDN_HEADS, GDN_DK, GDN_DV), F32)] + one_set + one_set,
        compiler_params=_cparams(("arbitrary", "arbitrary")),
    )(x, x, x, mod, mod, nw, w, wsm, wsmt, cw, pcol, prow, gnw)


def _group_cumsum_rows(x, group):
    rows = lax.broadcasted_iota(jnp.int32, x.shape, 0) % group
    d = 1
    while d < group:
        x = x + jnp.where(rows >= d, pltpu.roll(x, d, axis=0), 0.0)
        d *= 2
    return x


def _hgrn_kernel(x_ref, xn_ref, mod_ref, nw_ref, w_ref, lbp_ref, gnw_ref, o_ref, *scratch,
                 ts, layer, unit, group):
    kw = HGRN_HEADS * HGRN_DK
    c = HGRN_CHUNK
    n_sub = unit // c
    n_units = ts // unit
    sets = (scratch[0:7], scratch[7:14])
    st_ref = scratch[14]

    @pl.when(pl.program_id(1) == 0)
    def _():
        st_ref[...] = jnp.zeros_like(st_ref)

    m = mod_ref[0]
    nw = nw_ref[...]
    gnw = gnw_ref[...]
    incl = _tri(c)
    heads = range(HGRN_HEADS)
    sl = [slice(hd * HGRN_DK, (hd + 1) * HGRN_DK) for hd in heads]

    lbp = lbp_ref[...]
    e = jnp.exp(lbp - jnp.max(lbp, axis=0, keepdims=True))
    soft = e / jnp.sum(e, axis=0, keepdims=True)
    lb = jnp.zeros((1, kw), F32)
    for i in range(1, layer + 1):
        lb = lb + soft[i:i + 1]

    def prepare(u_idx, dst, src_ref=x_ref):
        qs_ref, ks_ref, qg_ref, ke_ref, v_ref, ge_ref, z_ref = dst
        rows = pl.ds(pl.multiple_of(u_idx * unit, unit), unit)
        h = _mod_norm(src_ref[0, rows, :], nw, m[0:1], m[1:2]).astype(BF16)
        yield
        f_raw = jnp.dot(h, w_ref[:, kw:2 * kw], preferred_element_type=F32)
        sig = _sigmoid(f_raw)
        g = _group_cumsum_rows(jnp.log(lb + (1.0 - lb) * sig), c).reshape(n_sub, c, kw)
        g_mid = g[:, c // 2 - 1:c // 2, :]
        g_last = g[:, c - 1:c, :]
        g_end = jnp.exp(g_last).reshape(n_sub, kw)
        for hd in heads:
            ge_ref[hd] = g_end[:, sl[hd]].T
        yield
        k = ((1.0 - lb) * _sigmoid(-f_raw)).reshape(n_sub, c, kw)
        ks_ref[...] = (k * jnp.exp(g_mid - g)).reshape(unit, kw).astype(BF16)
        ke_ref[...] = (k * jnp.exp(g_last - g)).reshape(unit, kw).astype(BF16)
        yield
        q = _silu(jnp.dot(h, w_ref[:, :kw], preferred_element_type=F32)).reshape(n_sub, c, kw)
        qs_ref[...] = (q * jnp.exp(g - g_mid)).reshape(unit, kw).astype(BF16)
        qg_ref[...] = (q * jnp.exp(g)).reshape(unit, kw).astype(BF16)
        yield
        v_ref[...] = jnp.dot(h, w_ref[:, 2 * kw:3 * kw], preferred_element_type=F32).astype(BF16)
        yield
        z_ref[...] = _silu(jnp.dot(h, w_ref[:, 3 * kw:], preferred_element_type=F32))

    def recur(u_idx, src):
        qs_ref, ks_ref, qg_ref, ke_ref, v_ref, ge_ref, z_ref = src
        base = pl.multiple_of(u_idx * unit, unit)
        for g0 in range(0, n_sub, group):
            subs = range(g0, g0 + group)
            rows = {u: slice(u * c, (u + 1) * c) for u in subs}
            pairs = [(u, hd) for u in subs for hd in heads]
            vv = {u: v_ref[rows[u], :] for u in subs}
            sc = {(u, hd): _mm_nt(qs_ref[rows[u], sl[hd]], ks_ref[rows[u], sl[hd]]) for u, hd in pairs}
            yield
            kv = {(u, hd): _mm_tn(ke_ref[rows[u], sl[hd]], vv[u][:, sl[hd]]) for u, hd in pairs}
            yield
            attn = {(u, hd): jnp.where(incl, sc[u, hd], 0.0).astype(BF16) for u, hd in pairs}
            yield
            st = [st_ref[hd] for hd in heads]
            oo = {}
            for u in subs:
                for hd in heads:
                    lhs = jnp.concatenate([qg_ref[rows[u], sl[hd]], attn[u, hd]], axis=1)
                    rhs = jnp.concatenate([st[hd].astype(BF16), vv[u][:, sl[hd]]], axis=0)
                    oo[u, hd] = jnp.dot(lhs, rhs, preferred_element_type=F32)
                    st[hd] = st[hd] * ge_ref[hd][:, u:u + 1] + kv[u, hd]
                if u == g0 + group // 2 - 1:
                    yield
            for hd in heads:
                st_ref[hd] = st[hd]
            yield
            for u, hd in pairs:
                o = oo[u, hd]
                o = o * lax.rsqrt(jnp.mean(o * o, axis=-1, keepdims=True) + EPS) * gnw * z_ref[rows[u], sl[hd]]
                o_ref[0, pl.ds(base + u * c, c), sl[hd]] = o.astype(o_ref.dtype)
            yield

    @pl.when(pl.program_id(1) == 0)
    def _():
        _alternate(prepare(0, sets[0]))

    def body(i, carry):
        _alternate(recur(2 * i, sets[0]), prepare(2 * i + 1, sets[1]))
        _alternate(recur(2 * i + 1, sets[1]), prepare(2 * i + 2, sets[0]))
        return carry

    lax.fori_loop(0, n_units // 2 - 1, body, 0)
    _alternate(recur(n_units - 2, sets[0]), prepare(n_units - 1, sets[1]))
    _alternate(recur(n_units - 1, sets[1]), prepare(0, sets[0], xn_ref))


def _hgrn_call(x, mod, nw, w, lbp, gnw, ts, layer):
    bsz, s, d = x.shape
    kw = HGRN_HEADS * HGRN_DK
    group = 8
    unit = min(256, ts // 2)
    assert s % ts == 0 and ts % (2 * unit) == 0 and unit % (group * HGRN_CHUNK) == 0
    one_set = ([pltpu.VMEM((unit, kw), BF16) for _ in range(5)]
               + [pltpu.VMEM((HGRN_HEADS, HGRN_DK, unit // HGRN_CHUNK), F32), pltpu.VMEM((unit, kw), F32)])
    return pl.pallas_call(
        functools.partial(_hgrn_kernel, ts=ts, layer=layer, unit=unit, group=group),
        grid=(bsz, s // ts),
        in_specs=[
            pl.BlockSpec((1, ts, d), lambda b, j: (b, j, 0)),
            pl.BlockSpec((1, unit, d), lambda b, j: (b, jnp.minimum((j + 1) * (ts // unit), s // unit - 1), 0)),
            pl.BlockSpec((1, 6, d), lambda b, j: (b, 0, 0)),
            _const_spec(nw.shape), _const_spec(w.shape), _const_spec(lbp.shape),
            _const_spec(gnw.shape),
        ],
        out_specs=pl.BlockSpec((1, ts, kw), lambda b, j: (b, j, 0)),
        out_shape=jax.ShapeDtypeStruct((bsz, s, kw), BF16),
        scratch_shapes=one_set + one_set + [pltpu.VMEM((HGRN_HEADS, HGRN_DK, HGRN_DK), F32)],
        compiler_params=_cparams(("parallel", "arbitrary")),
    )(x, x, mod, nw, w, lbp, gnw)


def _ssd_kernel(x0_ref, xa_ref, xb_ref, mod_ref, modn_ref, nw_ref, w_ref, wdt_ref, wdtt_ref, cw_ref,
                cb_ref, pcol_ref, prow_ref, dsk_ref, gnw_ref, o_ref, ext_ref, st_ref, *sets, ts, n_tiles):
    inner = SSD_HEADS * SSD_HEAD_DIM
    bc_w = 2 * SSD_GROUPS * SSD_STATE
    xbc_w = inner + bc_w
    hpg = SSD_HEADS // SSD_GROUPS
    p = SSD_HEAD_DIM
    set_a, set_b = sets[:7], sets[7:]

    m = mod_ref[0]
    nw = nw_ref[...]
    incl = _tri(ts)

    def prepare(x_ref, dst, m=m, history=None):
        xs_s, bc_s, zs_s, dtc_s, acc_s, acr_s, cb_s = dst
        h = _mod_norm(x_ref[0], nw, m[0:1], m[1:2]).astype(BF16)
        yield
        proj = jnp.dot(h, w_ref[...], preferred_element_type=F32)
        zs_s[...] = _silu(proj[:, :inner])
        yield
        if history is not None:
            ext_ref[0:HALO, :] = ext_ref[0:HALO, :] * history
        ext_ref[HALO:HALO + ts, :] = proj[:, inner:]
        cw = cw_ref[...]
        acc = cw[SSD_CONV - 1:SSD_CONV] * proj[:, inner:] + cb_ref[...]
        for i in range(SSD_CONV - 1):
            off = HALO - (SSD_CONV - 1) + i
            acc = acc + cw[i:i + 1] * ext_ref[off:off + ts, :]
        ext_ref[0:HALO, :] = ext_ref[ts:ts + HALO, :]
        xbc = _silu(acc)
        xs_s[...] = xbc[:, :inner]
        bc_s[...] = xbc[:, inner:]
        yield
        pcol = pcol_ref[...]
        prow = prow_ref[...]
        dt_cols = _softplus(jnp.dot(h, wdt_ref[...], preferred_element_type=F32) + pcol[1:2])
        dt_rows = _softplus(lax.dot_general(wdtt_ref[...], h, (((1,), (1,)), ((), ())),
                                            preferred_element_type=F32) + prow[:, 1:2])
        low = jnp.where(incl, 1.0, 0.0).astype(F32)
        upp = jnp.where(lax.broadcasted_iota(jnp.int32, (ts, ts), 0)
                        <= lax.broadcasted_iota(jnp.int32, (ts, ts), 1), 1.0, 0.0).astype(F32)
        dtc_s[...] = dt_cols
        acc_s[...] = _mask_mm(low, dt_cols * (-jnp.exp(pcol[0:1])))
        acr_s[...] = _mm_mask(dt_rows * (-jnp.exp(prow[:, 0:1])), upp)
        yield
        for g in range(SSD_GROUPS):
            bm = xbc[:, inner + g * SSD_STATE:inner + (g + 1) * SSD_STATE]
            cm = xbc[:, inner + (SSD_GROUPS + g) * SSD_STATE:inner + (SSD_GROUPS + g + 1) * SSD_STATE]
            cb_s[g] = _mm_nt(cm, bm)
            yield

    def finish(src, out_rows):
        xs_s, bc_s, zs_s, dtc_s, acc_s, acr_s, cb_s = src
        first = lax.broadcasted_iota(jnp.int32, (ts, 2 * p), 1) < p
        first_rows = lax.broadcasted_iota(jnp.int32, (2 * p, SSD_STATE), 0) < p
        ys = []
        for g in range(SSD_GROUPS):
            bm = bc_s[:, g * SSD_STATE:(g + 1) * SSD_STATE]
            cm = bc_s[:, (SSD_GROUPS + g) * SSD_STATE:(SSD_GROUPS + g + 1) * SSD_STATE]
            cb = cb_s[g]
            for pr in range(hpg // 2):
                h0 = g * hpg + 2 * pr
                h1 = h0 + 1
                a0, a1 = acc_s[:, h0:h0 + 1], acc_s[:, h1:h1 + 1]
                seg0 = jnp.where(incl, jnp.exp(a0 - acr_s[h0:h0 + 1, :]), 0.0)
                seg1 = jnp.where(incl, jnp.exp(a1 - acr_s[h1:h1 + 1, :]), 0.0)
                xdt = xs_s[:, h0 * p:(h1 + 1) * p] * jnp.where(first, dtc_s[:, h0:h0 + 1], dtc_s[:, h1:h1 + 1])
                st = st_ref[g * (hpg // 2) + pr]
                y = (_mm(cb * seg0, jnp.where(first, xdt, 0.0)) + _mm(cb * seg1, jnp.where(first, 0.0, xdt))
                     + _mm_nt(cm, st) * jnp.where(first, jnp.exp(a0), jnp.exp(a1)))
                yield
                last0, last1 = a0[ts - 1:ts, :], a1[ts - 1:ts, :]
                w_in = jnp.where(first, jnp.exp(last0 - a0), jnp.exp(last1 - a1))
                st_ref[g * (hpg // 2) + pr] = (st * jnp.where(first_rows, jnp.exp(last0), jnp.exp(last1))
                                               + _mm_tn(xdt * w_in, bm))
                ys.append(y)
                yield
        gnw = gnw_ref[...]
        gw = hpg * p
        for g in range(SSD_GROUPS):
            cols = slice(g * gw, (g + 1) * gw)
            yg = jnp.concatenate(ys[g * (hpg // 2):(g + 1) * (hpg // 2)], axis=1)
            yg = (yg + dsk_ref[:, cols] * xs_s[:, cols]) * zs_s[:, cols]
            yg = yg * lax.rsqrt(jnp.mean(yg * yg, axis=-1, keepdims=True) + EPS) * gnw[:, cols]
            o_ref[0, out_rows:out_rows + ts, cols] = yg.astype(o_ref.dtype)
            yield

    j = pl.program_id(1)

    @pl.when(j == 0)
    def _():
        st_ref[...] = jnp.zeros_like(st_ref)

    @pl.when((pl.program_id(0) == 0) & (j == 0))
    def _():
        ext_ref[0:HALO, :] = jnp.zeros((HALO, xbc_w), F32)
        _alternate(prepare(x0_ref, set_a))

    _alternate(finish(set_a, 0), prepare(xa_ref, set_b))
    new_seq = 2 * j + 2 >= n_tiles
    mn = modn_ref[0]
    m_next = jnp.where(new_seq, mn, m)
    _alternate(finish(set_b, ts),
               prepare(xb_ref, set_a, m=m_next, history=jnp.where(new_seq, 0.0, 1.0).astype(F32)))


def _ssd_call(x, mod, nw, w, wdt, wdtt, cw, cb, pcol, prow, dsk, gnw, ts):
    bsz, s, d = x.shape
    inner = SSD_HEADS * SSD_HEAD_DIM
    bc_w = 2 * SSD_GROUPS * SSD_STATE
    xbc_w = inner + bc_w
    n_tiles = s // ts
    assert s % (2 * ts) == 0, "the SSD kernel takes two tiles per grid step"
    one_set = [pltpu.VMEM((ts, inner), F32), pltpu.VMEM((ts, bc_w), F32), pltpu.VMEM((ts, inner), F32),
               pltpu.VMEM((ts, 128), F32), pltpu.VMEM((ts, 128), F32), pltpu.VMEM((8, ts), F32),
               pltpu.VMEM((SSD_GROUPS, ts, ts), F32)]
    tile = lambda f: pl.BlockSpec((1, ts, d), f)
    next_b = lambda b, j: jnp.minimum(b + (2 * j + 2) // n_tiles, bsz - 1)
    return pl.pallas_call(
        functools.partial(_ssd_kernel, ts=ts, n_tiles=n_tiles),
        grid=(bsz, n_tiles // 2),
        in_specs=[
            tile(lambda b, j: (b, 0, 0)),
            tile(lambda b, j: (b, 2 * j + 1, 0)),
            tile(lambda b, j: (next_b(b, j), (2 * j + 2) % n_tiles, 0)),
            pl.BlockSpec((1, 6, d), lambda b, j: (b, 0, 0)),
            pl.BlockSpec((1, 6, d), lambda b, j: (next_b(b, j), 0, 0)),
            _const_spec(nw.shape), _const_spec(w.shape), _const_spec(wdt.shape),
            _const_spec(wdtt.shape), _const_spec(cw.shape), _const_spec(cb.shape),
            _const_spec(pcol.shape), _const_spec(prow.shape), _const_spec(dsk.shape),
            _const_spec(gnw.shape),
        ],
        out_specs=pl.BlockSpec((1, 2 * ts, inner), lambda b, j: (b, j, 0)),
        out_shape=jax.ShapeDtypeStruct((bsz, s, inner), BF16),
        scratch_shapes=[pltpu.VMEM((ts + HALO, xbc_w), F32),
                        pltpu.VMEM((SSD_HEADS // 2, 2 * SSD_HEAD_DIM, SSD_STATE), F32)] + one_set + one_set,
        compiler_params=_cparams(("arbitrary", "arbitrary")),
    )(x, x, x, mod, mod, nw, w, wdt, wdtt, cw, cb, pcol, prow, dsk, gnw)


def _merge_kernel(x_ref, mod_ref, nw_ref, oa_ref, ob_ref, oc_ref, wg_ref, wa_ref, wb_ref, wc_ref,
                  wo_ref, o_ref):
    d = x_ref.shape[-1]
    x = x_ref[0]
    m = mod_ref[0]
    h = _mod_norm(x, nw_ref[...], m[0:1], m[1:2]).astype(BF16)
    gates = _sigmoid(jnp.dot(h, wg_ref[...], preferred_element_type=F32))
    merged = (gates[:, :d] * jnp.dot(oa_ref[0], wa_ref[...], preferred_element_type=F32)
              + gates[:, d:2 * d] * jnp.dot(ob_ref[0], wb_ref[...], preferred_element_type=F32)
              + gates[:, 2 * d:] * jnp.dot(oc_ref[0], wc_ref[...], preferred_element_type=F32))
    mix = jnp.dot(merged.astype(BF16), wo_ref[...], preferred_element_type=F32)
    o_ref[0] = x + m[2:3] * mix


def _merge_call(x, mod, nw, oa, ob, oc, wg, wa, wb, wc, wo, tm):
    bsz, s, d = x.shape
    tok = lambda w: pl.BlockSpec((1, tm, w), lambda b, j: (b, j, 0))
    return pl.pallas_call(
        _merge_kernel,
        grid=(bsz, s // tm),
        in_specs=[
            tok(d), pl.BlockSpec((1, 6, d), lambda b, j: (b, 0, 0)), _const_spec(nw.shape),
            tok(oa.shape[-1]), tok(ob.shape[-1]), tok(oc.shape[-1]),
            _const_spec(wg.shape), _const_spec(wa.shape), _const_spec(wb.shape),
            _const_spec(wc.shape), _const_spec(wo.shape),
        ],
        out_specs=tok(d),
        out_shape=jax.ShapeDtypeStruct((bsz, s, d), F32),
        compiler_params=_cparams(("parallel", "parallel")),
    )(x, mod, nw, oa, ob, oc, wg, wa, wb, wc, wo)


def _ffn_kernel(x_ref, mod_ref, nw_ref, wup_ref, cw_ref, cb_ref, wdn_ref, fnw_ref, o_ref,
                ext_ref, tail_ref, *, tm, splits, final):
    hidden = wdn_ref.shape[0]

    @pl.when(pl.program_id(1) == 0)
    def _():
        tail_ref[...] = jnp.zeros_like(tail_ref)

    x = x_ref[0]
    m = mod_ref[0]
    h = _mod_norm(x, nw_ref[...], m[3:4], m[4:5]).astype(BF16)

    def conv_part(c0, fc):
        u = jnp.dot(h, wup_ref[:, c0:c0 + fc], preferred_element_type=F32)
        ext_ref[0:HALO, 0:fc] = tail_ref[:, c0:c0 + fc]
        ext_ref[HALO:HALO + tm, 0:fc] = u
        cw = cw_ref[:, c0:c0 + fc]
        acc = cw[FFN_CONV - 1:FFN_CONV] * u + cb_ref[:, c0:c0 + fc]
        for i in range(FFN_CONV - 1):
            off = HALO - (FFN_CONV - 1) + i
            acc = acc + cw[i:i + 1] * ext_ref[off:off + tm, 0:fc]
        tail_ref[:, c0:c0 + fc] = ext_ref[tm:tm + HALO, 0:fc]
        return acc

    y = jnp.zeros_like(x)
    for c0, fc in splits:
        gate = conv_part(c0, fc)
        val = conv_part(hidden + c0, fc)
        act = (_silu(gate) * val).astype(BF16)
        y = y + jnp.dot(act, wdn_ref[c0:c0 + fc, :], preferred_element_type=F32)
    out = x + m[5:6] * y
    if final:
        out = out * lax.rsqrt(jnp.mean(out * out, axis=-1, keepdims=True) + EPS) * fnw_ref[...]
    o_ref[0] = out


def _ffn_call(x, mod, nw, wup, cw, cb, wdn, fnw, tm, final):
    bsz, s, d = x.shape
    hidden = wdn.shape[0]
    first = -(-(hidden // MXU_TILE) // 2) * MXU_TILE
    splits = ((0, first), (first, hidden - first))
    tok = pl.BlockSpec((1, tm, d), lambda b, j: (b, j, 0))
    single = lambda a: pl.BlockSpec(a.shape, lambda *_: (0,) * a.ndim, pipeline_mode=pl.Buffered(1))
    return pl.pallas_call(
        functools.partial(_ffn_kernel, tm=tm, splits=splits, final=final),
        grid=(bsz, s // tm),
        in_specs=[
            tok, pl.BlockSpec((1, 6, d), lambda b, j: (b, 0, 0)), _const_spec(nw.shape),
            single(wup), _const_spec(cw.shape), _const_spec(cb.shape), single(wdn),
            _const_spec(fnw.shape),
        ],
        out_specs=tok,
        out_shape=jax.ShapeDtypeStruct((bsz, s, d), F32),
        scratch_shapes=[pltpu.VMEM((tm + HALO, first), F32),
                        pltpu.VMEM((HALO, 2 * hidden), F32)],
        compiler_params=_cparams(("parallel", "arbitrary")),
    )(x, mod, nw, wup, cw, cb, wdn, fnw)


def _pad_lanes(a, width=128):
    return jnp.pad(a, ((0, 0), (0, width - a.shape[-1])))


def kernel(x, c, w_ada, b_ada, norm1_w, w_in, gdn_conv_w, gdn_a_log, gdn_dt_bias, gdn_norm_w,
           hgrn_lb_param, hgrn_norm_w, ssd_conv_w, ssd_conv_b, ssd_a_log, ssd_dt_bias, ssd_d,
           ssd_norm_w, w_br_a, w_br_b, w_br_c, w_out, norm2_w, ffn_w_up, ffn_conv_w, ffn_conv_b,
           ffn_w_down, final_norm_w):
    bsz, s, d = x.shape
    depth = w_in.shape[0]
    gk = GDN_HEADS * GDN_DK
    gv = GDN_HEADS * GDN_DV
    hk = HGRN_HEADS * HGRN_DK
    inner = SSD_HEADS * SSD_HEAD_DIM
    xbc_w = inner + 2 * SSD_GROUPS * SSD_STATE

    sizes = (2 * gk + gv, GDN_HEADS, GDN_HEADS, gv, hk, hk, hk, hk, inner, xbc_w, SSD_HEADS, 3 * d)
    offs = [0]
    for sz in sizes:
        offs.append(offs[-1] + sz)
    (o_qkv, o_a, o_b, o_z, o_hq, _, _, _, o_sz, o_xbc, o_dt, o_gate, o_end) = offs

    nb_gdn = 4 if bsz % 4 == 0 else (2 if bsz % 2 == 0 else 1)
    rows_gdn = min(max(GDN_CHUNK, 256 // nb_gdn), s)
    ts_hgrn = min(2048, s)
    ts_ssd = min(256, s)
    tm_merge = min(512, s)
    tm_ffn = min(512, s)

    mod = _ada_call(c, w_ada, b_ada).reshape(depth, bsz, 6, d)
    fnw = final_norm_w.reshape(1, d)

    for l in range(depth):
        wl = w_in[l]
        mod_l = mod[l]
        nw1 = norm1_w[l].reshape(1, d)

        w_gdn = jnp.concatenate([wl[:, o_qkv:o_a], wl[:, o_z:o_hq]], axis=1).astype(BF16)
        w_ab = wl[:, o_a:o_z]
        o_ga = _gdn_call(
            x, mod_l, nw1, w_gdn, _pad_lanes(w_ab).astype(BF16), w_ab.T.astype(BF16),
            gdn_conv_w[l],
            _pad_lanes(jnp.stack([gdn_a_log[l], gdn_dt_bias[l]])),
            jnp.pad(jnp.stack([gdn_a_log[l], gdn_dt_bias[l]], axis=1), ((0, GDN_HEADS), (0, 0))),
            gdn_norm_w[l].reshape(1, GDN_DV), nb_gdn, rows_gdn)

        o_hg = _hgrn_call(x, mod_l, nw1, wl[:, o_hq:o_sz].astype(BF16), hgrn_lb_param,
                          hgrn_norm_w[l].reshape(1, HGRN_DK), ts_hgrn, l)

        w_dt = wl[:, o_dt:o_gate]
        o_sd = _ssd_call(
            x, mod_l, nw1, wl[:, o_sz:o_dt].astype(BF16), _pad_lanes(w_dt).astype(BF16),
            w_dt.T.astype(BF16), ssd_conv_w[l], ssd_conv_b[l].reshape(1, xbc_w),
            _pad_lanes(jnp.stack([ssd_a_log[l], ssd_dt_bias[l]])),
            jnp.stack([ssd_a_log[l], ssd_dt_bias[l]], axis=1),
            jnp.repeat(ssd_d[l], SSD_HEAD_DIM).reshape(1, inner),
            ssd_norm_w[l].reshape(1, inner), ts_ssd)

        x = _merge_call(x, mod_l, nw1, o_ga, o_hg, o_sd, wl[:, o_gate:o_end].astype(BF16),
                        w_br_a[l].astype(BF16), w_br_b[l].astype(BF16), w_br_c[l].astype(BF16),
                        w_out[l].astype(BF16), tm_merge)

        x = _ffn_call(x, mod_l, norm2_w[l].reshape(1, d), ffn_w_up[l].astype(BF16), ffn_conv_w[l],
                      ffn_conv_b[l].reshape(1, -1), ffn_w_down[l].astype(BF16), fnw, tm_ffn,
                      final=(l == depth - 1))
    return x
```
